```python
import jax
import jax.numpy as jnp
from jax import lax
import numpy as np

D_MODEL = 1024
BATCH = 16
SEQ = 2048
DEPTH = 1
DEC_BATCH = 8
DEC_SEQ = 32
PAST_LEN = 4096

CHUNK = 64
HEAD_DIM = 64
N_RW_HEADS = 8
D_RW = N_RW_HEADS * HEAD_DIM
N_SB_HEADS = 8
D_SB = N_SB_HEADS * HEAD_DIM
RW_DECAY_LORA = 64
RW_ICLR_LORA = 64
RW_GATE_LORA = 128
N_BRANCH = 2
N_MEM = 256
N_X_HEADS = 4
X_HEAD_DIM = D_MODEL // N_X_HEADS
N_EXPERTS = 32
TOP_K = 4
D_FF = D_MODEL
SWIGLU_LIMIT = 7.0
SWIGLU_ALPHA = 1.702
SB_BLOCK = 128
MOE_BLOCK = 256
RMS_EPS = 1e-5
GN_EPS = 64e-5

D_SB_COLS = 3 * D_SB
D_RW_COLS = 3 * D_RW + RW_DECAY_LORA + RW_ICLR_LORA + RW_GATE_LORA
D_IN = D_SB_COLS + D_RW_COLS + N_BRANCH * D_MODEL
IN_SPLITS = (D_SB_COLS, D_SB_COLS + D_RW_COLS)
RW_SPLITS = (D_RW, 2 * D_RW, 3 * D_RW, 3 * D_RW + RW_DECAY_LORA, 3 * D_RW + RW_DECAY_LORA + RW_ICLR_LORA)

kernel_name = 'rwkv7_stickbreak_moe_stream_step'


def rmsnorm(x, g):
    xf = x.astype(jnp.float32)
    xf = xf * lax.rsqrt(jnp.mean(xf * xf, axis=-1, keepdims=True) + RMS_EPS)
    return (xf * g.astype(jnp.float32)).astype(x.dtype)


def split_heads(t, n_heads):
    return t.reshape(t.shape[:-1] + (n_heads, t.shape[-1] // n_heads))


def merge_heads(t):
    return t.reshape(t.shape[:-2] + (t.shape[-2] * t.shape[-1],))


def rwkv_recurrence(S0, r, decay, k, v, kk, a):
    f32 = jnp.float32
    b = kk * a
    seq = tuple(jnp.moveaxis(t.astype(f32), 1, 0) for t in (r, decay, k, v, kk, b))

    def step(S, inp):
        r_t, w_t, k_t, v_t, kk_t, b_t = inp
        S = (S * w_t[:, :, None, :]
             - jnp.einsum('bhvk,bhk->bhv', S, kk_t)[..., None] * b_t[:, :, None, :]
             + v_t[..., None] * k_t[:, :, None, :])
        return S, jnp.einsum('bhvk,bhk->bhv', S, r_t)

    S, ys = lax.scan(step, S0.astype(f32), seq)
    return S, jnp.moveaxis(ys, 0, 1)


def rwkv_time_mix(p_rw, p_rw_prev, S0, mu, w0, w_up, a0, a_up, g_up, k_k, k_a, r_k, gn_w, gn_b):
    xs = p_rw + mu * (p_rw_prev - p_rw)
    r, k, v, wd, ad, gd = jnp.split(xs, RW_SPLITS, axis=-1)
    w_log = -jax.nn.softplus(-(w0 + jnp.tanh(wd) @ w_up)) - 0.5
    decay = jnp.exp(-jnp.exp(w_log.astype(jnp.float32)))
    a = jax.nn.sigmoid(a0 + ad @ a_up)
    g = jax.nn.sigmoid(gd) @ g_up
    kk = split_heads((k * k_k).astype(jnp.float32), N_RW_HEADS)
    kk = kk / jnp.maximum(jnp.sqrt(jnp.sum(kk * kk, axis=-1, keepdims=True)), 1e-12)
    k = k * (1.0 + (a - 1.0) * k_a)
    rh, kh, vh = (split_heads(t, N_RW_HEADS) for t in (r, k, v))
    S_new, y = rwkv_recurrence(S0, rh, split_heads(decay, N_RW_HEADS), kh, vh, kk,
                               split_heads(a, N_RW_HEADS))
    mu_y = jnp.mean(y, axis=-1, keepdims=True)
    var_y = jnp.mean(jnp.square(y - mu_y), axis=-1, keepdims=True)
    y = merge_heads((y - mu_y) * lax.rsqrt(var_y + GN_EPS)).astype(p_rw.dtype) * gn_w + gn_b
    bonus = jnp.sum((rh * kh * r_k).astype(jnp.float32), axis=-1, keepdims=True).astype(vh.dtype) * vh
    return (y + merge_heads(bonus)) * g, S_new


def stick_breaking(q, k_all, v_all, past_len):
    T, N = q.shape[1], q.shape[-1]
    outs = []
    for start in range(0, T, SB_BLOCK):
        stop = min(start + SB_BLOCK, T)
        k_end = past_len + stop
        q_pos = past_len + jnp.arange(start, stop)
        k_pos = jnp.arange(k_end)
        z = jnp.einsum('bqhd,bkhd->bhqk', q[:, start:stop], k_all[:, :k_end]).astype(jnp.float32) * (N ** -0.5)
        before = k_pos[None, :] < q_pos[:, None]
        log_keep = jnp.where(before, jax.nn.log_sigmoid(-z), 0.0)
        log_stick = lax.cumsum(log_keep, axis=3, reverse=True) - log_keep
        att = jnp.where(before, jnp.exp(jax.nn.log_sigmoid(z) + log_stick), 0.0)
        outs.append(jnp.einsum('bhqk,bkhd->bqhd', att.astype(v_all.dtype), v_all[:, :k_end]))
    return jnp.concatenate(outs, axis=1)


def memory_kv(mem, g, w_k, w_v):
    m = rmsnorm(mem, g)
    return split_heads(m @ w_k, N_X_HEADS), split_heads(m @ w_v, N_X_HEADS)


def cross_attention(h, mem_k, mem_v, w_q, w_o):
    q = split_heads(h @ w_q, N_X_HEADS)
    s = jnp.einsum('bthd,bmhd->bhtm', q, mem_k).astype(jnp.float32) * (X_HEAD_DIM ** -0.5)
    p = jax.nn.softmax(s, axis=-1).astype(mem_v.dtype)
    return merge_heads(jnp.einsum('bhtm,bmhd->bthd', p, mem_v)) @ w_o


def moe_ffn(h, w_router, b_router, w1, b1, w2, b2):
    lead = h.shape[:-1]
    xt = h.reshape(-1, D_MODEL)
    n = xt.shape[0]
    m = n * TOP_K
    logits = (xt @ w_router + b_router).astype(jnp.float32)
    top_val, top_idx = lax.top_k(logits, TOP_K)
    gates = jax.nn.softmax(top_val, axis=-1)
    e_flat = top_idx.reshape(-1).astype(jnp.int32)
    tok_flat = jnp.arange(m, dtype=jnp.int32) // TOP_K
    order = jnp.argsort(e_flat)
    e_sorted = e_flat[order]
    counts = jnp.bincount(e_flat, length=N_EXPERTS).astype(jnp.int32)
    padded = (counts + MOE_BLOCK - 1) // MOE_BLOCK * MOE_BLOCK
    starts = jnp.cumsum(counts) - counts
    pends = jnp.cumsum(padded)
    dest = (pends - padded)[e_sorted] + jnp.arange(m, dtype=jnp.int32) - starts[e_sorted]
    n_blocks = -(-(m + N_EXPERTS * (MOE_BLOCK - 1)) // MOE_BLOCK)
    n_rows = n_blocks * MOE_BLOCK
    row_tok = jnp.full((n_rows,), n, jnp.int32).at[dest].set(tok_flat[order])
    row_gate = jnp.zeros((n_rows,), jnp.float32).at[dest].set(gates.reshape(-1)[order])
    block_expert = jnp.minimum(
        jnp.searchsorted(pends, jnp.arange(n_blocks, dtype=jnp.int32) * MOE_BLOCK, side='right'),
        N_EXPERTS - 1)
    x_pad = jnp.concatenate([xt, jnp.zeros((1, D_MODEL), xt.dtype)], axis=0)

    def expert_block(args):
        toks, e = args
        hb = x_pad[toks] @ w1[e] + b1[e]
        glu = jnp.minimum(hb[:, :D_FF], SWIGLU_LIMIT)
        lin = jnp.clip(hb[:, D_FF:], -SWIGLU_LIMIT, SWIGLU_LIMIT)
        return (glu * jax.nn.sigmoid(SWIGLU_ALPHA * glu) * (lin + 1.0)) @ w2[e] + b2[e]

    out = lax.map(expert_block, (row_tok.reshape(n_blocks, MOE_BLOCK), block_expert))
    out = out.reshape(n_rows, D_MODEL) * row_gate[:, None].astype(out.dtype)
    y = jax.ops.segment_sum(out, row_tok, num_segments=n + 1)[:n]
    return y.reshape(lead + (D_MODEL,))


def setup_inputs(seed: int = 0) -> dict:
    key = jax.random.key(seed)
    ks = iter(jax.random.split(key, 64))

    def nrm(shape, scale):
        return jax.random.normal(next(ks), shape, jnp.float32) * scale

    def gain(shape):
        return 1.0 + nrm(shape, 0.02)

    L = DEPTH
    return {
        'x_prompt': nrm((BATCH, SEQ, D_MODEL), 1.0),
        'x_sample': nrm((DEC_BATCH, DEC_SEQ, D_MODEL), 1.0),
        'state_rw_shift': nrm((L, DEC_BATCH, D_MODEL), 1.0),
        'state_rw_wkv': nrm((L, DEC_BATCH, N_RW_HEADS, HEAD_DIM, HEAD_DIM), 0.5),
        'cache_sb_k': nrm((L, DEC_BATCH, PAST_LEN, N_SB_HEADS, HEAD_DIM), 1.0),
        'cache_sb_v': nrm((L, DEC_BATCH, PAST_LEN, N_SB_HEADS, HEAD_DIM), 1.0),
        'cache_mem_k': nrm((L, DEC_BATCH, N_MEM, N_X_HEADS, X_HEAD_DIM), 1.0),
        'cache_mem_v': nrm((L, DEC_BATCH, N_MEM, N_X_HEADS, X_HEAD_DIM), 1.0),
        'mem_prompt': nrm((BATCH, N_MEM, D_MODEL), 1.0),
        'g_mix': gain((L, D_MODEL)),
        'w_in': nrm((L, D_MODEL, D_IN), D_MODEL ** -0.5),
        'rw_mu': jax.random.uniform(next(ks), (L, D_RW_COLS), jnp.float32),
        'rw_w0': nrm((L, D_RW), 0.5),
        'rw_w_up': nrm((L, RW_DECAY_LORA, D_RW), RW_DECAY_LORA ** -0.5),
        'rw_a0': nrm((L, D_RW), 0.1),
        'rw_a_up': nrm((L, RW_ICLR_LORA, D_RW), 0.5 * RW_ICLR_LORA ** -0.5),
        'rw_g_up': nrm((L, RW_GATE_LORA, D_RW), RW_GATE_LORA ** -0.5),
        'rw_k_k': 0.85 + nrm((L, D_RW), 0.02),
        'rw_k_a': gain((L, D_RW)),
        'rw_r_k': nrm((L, N_RW_HEADS, HEAD_DIM), 0.1),
        'rw_gn_w': gain((L, D_RW)),
        'rw_gn_b': nrm((L, D_RW), 0.02),
        'w_branch': nrm((L, N_BRANCH, D_RW, D_MODEL), D_RW ** -0.5),
        'w_out': nrm((L, D_MODEL, D_MODEL), D_MODEL ** -0.5),
        'g_xattn': gain((L, D_MODEL)),
        'g_mem': gain((L, D_MODEL)),
        'w_xq': nrm((L, D_MODEL, N_X_HEADS * X_HEAD_DIM), D_MODEL ** -0.5),
        'w_mk': nrm((L, D_MODEL, N_X_HEADS * X_HEAD_DIM), D_MODEL ** -0.5),
        'w_mv': nrm((L, D_MODEL, N_X_HEADS * X_HEAD_DIM), D_MODEL ** -0.5),
        'w_xo': nrm((L, N_X_HEADS * X_HEAD_DIM, D_MODEL), D_MODEL ** -0.5),
        'g_ffn': gain((L, D_MODEL)),
        'w_router': nrm((L, D_MODEL, N_EXPERTS), D_MODEL ** -0.5),
        'b_router': nrm((L, N_EXPERTS), 0.01),
        'w_e1': nrm((L, N_EXPERTS, D_MODEL, 2 * D_FF), D_MODEL ** -0.5),
        'b_e1': nrm((L, N_EXPERTS, 2 * D_FF), 0.01),
        'w_e2': nrm((L, N_EXPERTS, D_FF, D_MODEL), D_FF ** -0.5),
        'b_e2': nrm((L, N_EXPERTS, D_MODEL), 0.01),
        'g_final': gain((D_MODEL,)),
    }


def reference(x_prompt, x_sample, state_rw_shift, state_rw_wkv, cache_sb_k, cache_sb_v, cache_mem_k, cache_mem_v,
              mem_prompt, g_mix, w_in, rw_mu, rw_w0, rw_w_up, rw_a0, rw_a_up, rw_g_up, rw_k_k, rw_k_a, rw_r_k,
              rw_gn_w, rw_gn_b, w_branch, w_out, g_xattn, g_mem, w_xq, w_mk, w_mv, w_xo, g_ffn,
              w_router, b_router, w_e1, b_e1, w_e2, b_e2, g_final):

    def layer(l, x, rw_prev, S0, k_past, v_past, mem_k, mem_v):
        B, T, _ = x.shape
        h = rmsnorm(x, g_mix[l])
        p_sb, p_rw, p_gate = jnp.split(h @ w_in[l], IN_SPLITS, axis=-1)
        p_rw_prev = jnp.concatenate([rw_prev.astype(p_rw.dtype), p_rw[:, :-1]], axis=1)
        o_rw, S_new = rwkv_time_mix(p_rw, p_rw_prev, S0, rw_mu[l], rw_w0[l], rw_w_up[l], rw_a0[l], rw_a_up[l],
                                    rw_g_up[l], rw_k_k[l], rw_k_a[l], rw_r_k[l], rw_gn_w[l], rw_gn_b[l])
        q_sb, k_sb, v_sb = (split_heads(t, N_SB_HEADS) for t in jnp.split(p_sb, 3, axis=-1))
        k_all = jnp.concatenate([k_past.astype(k_sb.dtype), k_sb], axis=1)
        v_all = jnp.concatenate([v_past.astype(v_sb.dtype), v_sb], axis=1)
        o_sb = merge_heads(stick_breaking(q_sb, k_all, v_all, k_past.shape[1]))
        gates = jax.nn.sigmoid(p_gate).reshape(B, T, N_BRANCH, D_MODEL)
        mixed = gates[:, :, 0] * (o_rw @ w_branch[l, 0]) + gates[:, :, 1] * (o_sb @ w_branch[l, 1])
        x = x + mixed @ w_out[l]
        x = x + cross_attention(rmsnorm(x, g_xattn[l]), mem_k, mem_v, w_xq[l], w_xo[l])
        x = x + moe_ffn(rmsnorm(x, g_ffn[l]), w_router[l], b_router[l], w_e1[l], b_e1[l], w_e2[l], b_e2[l])
        return x, h[:, -1], S_new.astype(x.dtype), k_sb, v_sb

    bp = x_prompt.shape[0]
    x_p, x_s = x_prompt, x_sample
    p_shift, p_wkv, p_k, p_v, p_mk, p_mv = [], [], [], [], [], []
    s_shift, s_wkv, s_k, s_v = [], [], [], []
    for l in range(DEPTH):
        mk, mv = memory_kv(mem_prompt, g_mem[l], w_mk[l], w_mv[l])
        empty_kv = jnp.zeros((bp, 0, N_SB_HEADS, HEAD_DIM), x_p.dtype)
        x_p, sh, S, kr, vr = layer(l, x_p, jnp.zeros((bp, 1, D_RW_COLS), x_p.dtype),
                                   jnp.zeros((bp, N_RW_HEADS, HEAD_DIM, HEAD_DIM), jnp.float32),
                                   empty_kv, empty_kv, mk, mv)
        p_shift.append(sh); p_wkv.append(S); p_k.append(kr); p_v.append(vr); p_mk.append(mk); p_mv.append(mv)
        rw_prev = (state_rw_shift[l] @ w_in[l][:, IN_SPLITS[0]:IN_SPLITS[1]])[:, None, :]
        x_s, sh, S, kr, vr = layer(l, x_s, rw_prev, state_rw_wkv[l], cache_sb_k[l], cache_sb_v[l],
                                   cache_mem_k[l], cache_mem_v[l])
        s_shift.append(sh); s_wkv.append(S); s_k.append(kr); s_v.append(vr)

    y_prompt = rmsnorm(x_p, g_final)
    y_sample = rmsnorm(x_s, g_final)
    return (y_prompt, y_sample,
            jnp.stack(p_shift), jnp.stack(p_wkv), jnp.stack(p_k), jnp.stack(p_v), jnp.stack(p_mk), jnp.stack(p_mv),
            jnp.stack(s_shift), jnp.stack(s_wkv), jnp.stack(s_k), jnp.stack(s_v))
```

```python
import functools

import jax
import jax.numpy as jnp
from jax import lax
from jax.experimental import pallas as pl
from jax.experimental.pallas import tpu as pltpu

F32 = jnp.float32
BF16 = jnp.bfloat16

D_MODEL = 1024
HEAD_DIM = 64
N_HEADS = 8
D_MIX = N_HEADS * HEAD_DIM
LORA_W, LORA_A, LORA_G = 64, 64, 128
D_RW_COLS = 3 * D_MIX + LORA_W + LORA_A + LORA_G
N_X_HEADS = 4
X_HEAD_DIM = 256
N_EXPERTS = 32
TOP_K = 4
D_FF = 1024
SWIGLU_LIMIT = 7.0
SWIGLU_ALPHA = 1.702
MOE_BLOCK = 256
RMS_EPS = 1e-5
GN_EPS = 64e-5

ROW_TILE = 256
RW_GROUP = 4
RW_LANES = RW_GROUP * HEAD_DIM
VMEM_LIMIT = 56 * 1024 * 1024


def _cparams(*sem):
    return pltpu.CompilerParams(dimension_semantics=sem, vmem_limit_bytes=VMEM_LIMIT)


def _resident(shape):
    nd = len(shape)
    return pl.BlockSpec(shape, lambda *_: (0,) * nd)


_NN = ((1,), (0,))
_NT = ((1,), (1,))
_TN = ((0,), (0,))


def _dot(a, b, dims=_NN):
    return lax.dot_general(a.astype(BF16), b.astype(BF16), (dims, ((), ())),
                           preferred_element_type=F32)


def _split(x, n):
    parts, rem = [], x
    for i in range(n):
        p = rem.astype(BF16)
        parts.append(p)
        if i + 1 < n:
            rem = rem - p.astype(F32)
    return parts


def _dot_exact_rhs(a, b_bf16, dims=_NN, n=3):
    out = None
    for p in _split(a, n):
        t = lax.dot_general(p, b_bf16, (dims, ((), ())), preferred_element_type=F32)
        out = t if out is None else out + t
    return out


def _dot_exact_lhs(a_bf16, b, dims=_NN, n=3):
    out = None
    for p in _split(b, n):
        t = lax.dot_general(a_bf16, p, (dims, ((), ())), preferred_element_type=F32)
        out = t if out is None else out + t
    return out


def _dot3(a, b, dims=_NN):
    ah, al = _split(a, 2)
    bh, bl = _split(b, 2)
    dn = (dims, ((), ()))
    return (lax.dot_general(ah, bh, dn, preferred_element_type=F32)
            + (lax.dot_general(ah, bl, dn, preferred_element_type=F32)
               + lax.dot_general(al, bh, dn, preferred_element_type=F32)))


def _rms(x, g):
    return x * lax.rsqrt(jnp.mean(x * x, axis=-1, keepdims=True) + RMS_EPS) * g


def _iota(shape, dim):
    return lax.broadcasted_iota(jnp.int32, shape, dim)


def _block_ones(n, width):
    return (_iota((n, n), 0) // width == _iota((n, n), 1) // width).astype(BF16)


def _bd_rows(x, nblk, width):
    blk = _iota(x.shape, 1) // width
    zero = jnp.zeros_like(x)
    return jnp.concatenate([jnp.where(blk == h, x, zero) for h in range(nblk)], axis=0)


def _norm_rows_kernel(x_ref, g_ref, o_ref):
    o_ref[...] = _rms(x_ref[...], g_ref[...])


def norm_rows(x, g):
    r, d = x.shape
    return pl.pallas_call(
        _norm_rows_kernel,
        out_shape=jax.ShapeDtypeStruct((r, d), F32),
        grid=(1,),
        in_specs=[_resident((r, d)), _resident((1, d))],
        out_specs=_resident((r, d)),
        name="norm_rows",
    )(x, g.reshape(1, d))


SB_COLS = 3 * D_MIX
GATE_COLS = 2 * D_MODEL
D_IN = SB_COLS + D_RW_COLS + GATE_COLS


def _in_proj_kernel(x_ref, g_ref, w_ref, q_ref, k_ref, v_ref, rw_ref, gate_ref):
    h = _rms(x_ref[...], g_ref[...]).astype(BF16)

    def proj(lo, hi):
        return jnp.dot(h, w_ref[:, lo:hi], preferred_element_type=F32)

    q_ref[...] = proj(0, D_MIX).astype(BF16)
    k_ref[...] = proj(D_MIX, 2 * D_MIX)
    v_ref[...] = proj(2 * D_MIX, SB_COLS)
    rw_ref[...] = proj(SB_COLS, SB_COLS + D_RW_COLS)
    gate_ref[...] = proj(SB_COLS + D_RW_COLS, D_IN).astype(BF16)


def in_proj(x, g, w_bf16):
    n = x.shape[0]
    tm = min(ROW_TILE, n)
    row = lambda width: pl.BlockSpec((tm, width), lambda i: (i, 0))
    return pl.pallas_call(
        _in_proj_kernel,
        out_shape=(jax.ShapeDtypeStruct((n, D_MIX), BF16),
                   jax.ShapeDtypeStruct((n, D_MIX), F32),
                   jax.ShapeDtypeStruct((n, D_MIX), F32),
                   jax.ShapeDtypeStruct((n, D_RW_COLS), F32),
                   jax.ShapeDtypeStruct((n, GATE_COLS), BF16)),
        grid=(n // tm,),
        in_specs=[row(D_MODEL), _resident((1, D_MODEL)), _resident((D_MODEL, D_IN))],
        out_specs=(row(D_MIX), row(D_MIX), row(D_MIX), row(D_RW_COLS), row(GATE_COLS)),
        compiler_params=_cparams("parallel"),
        name="in_proj",
    )(x, g.reshape(1, D_MODEL), w_bf16)


def _plain_proj_kernel(x_ref, w_ref, o_ref):
    o_ref[...] = _dot(x_ref[...], w_ref[...])


def plain_proj(x, w_bf16):
    r, m = x.shape[0], w_bf16.shape[1]
    return pl.pallas_call(
        _plain_proj_kernel,
        out_shape=jax.ShapeDtypeStruct((r, m), F32),
        grid=(1,),
        in_specs=[_resident(x.shape), _resident(w_bf16.shape)],
        out_specs=_resident((r, m)),
        name="plain_proj",
    )(x, w_bf16)


def _rw_prep_kernel(chunk, rw_ref, prev0_ref, mu_ref, w0_ref, a0_ref, kk_ref, ka_ref, rk_ref,
                    wup_ref, aup_ref, gup_ref,
                    rt_ref, at_ref, bt_ref, kt_ref, bh_ref, kh_ref, v_ref, g_ref, bonus_ref, wc_ref,
                    carry_ref):
    tt = rw_ref.shape[1]

    @pl.when(pl.program_id(1) == 0)
    def _():
        carry_ref[...] = prev0_ref[0]

    p = rw_ref[0]
    prev = jnp.where(_iota(p.shape, 0) == 0, carry_ref[...], pltpu.roll(p, 1, 0))
    carry_ref[...] = p[tt - 1:tt]
    xs = p + mu_ref[...] * (prev - p)
    r = xs[:, 0:D_MIX]
    k = xs[:, D_MIX:2 * D_MIX]
    v = xs[:, 2 * D_MIX:3 * D_MIX]
    lo = 3 * D_MIX
    wd = xs[:, lo:lo + LORA_W]
    ad = xs[:, lo + LORA_W:lo + LORA_W + LORA_A]
    gd = xs[:, lo + LORA_W + LORA_A:D_RW_COLS]

    pre_w = w0_ref[...] + _dot(jnp.tanh(wd), wup_ref[...])
    w_log = -jax.nn.softplus(-pre_w) - 0.5
    logw = -jnp.exp(w_log)
    a = jax.nn.sigmoid(a0_ref[...] + _dot(ad, aup_ref[...]))
    g_ref[0] = _dot(jax.nn.sigmoid(gd), gup_ref[...])

    head_ones = _block_ones(D_MIX, HEAD_DIM)
    kk = k * kk_ref[...]
    ss = _dot_exact_rhs(kk * kk, head_ones)
    kk = kk / jnp.maximum(jnp.sqrt(ss), 1e-12)
    k2 = k * (1.0 + (a - 1.0) * ka_ref[...])
    nb = -(kk * a)
    bonus_ref[0] = _dot_exact_rhs(r * k2 * rk_ref[...], head_ones) * v
    v_ref[0] = v.astype(BF16)

    ti, tj = _iota((tt, tt), 0), _iota((tt, tt), 1)
    same = ti // chunk == tj // chunk
    cum = _dot_exact_lhs((same & (tj <= ti)).astype(BF16), logw)
    tot = _dot_exact_lhs(same.astype(BF16), logw)
    e_in = jnp.exp(cum)
    e_out = jnp.exp(-cum)
    e_end = jnp.exp(tot - cum)
    rt_ref[0] = (r * e_in).astype(BF16)
    at_ref[0] = (kk * jnp.exp(cum - logw)).astype(BF16)
    bt_ref[0] = (nb * e_out).astype(BF16)
    kt_ref[0] = (k2 * e_out).astype(BF16)
    bh_ref[0] = (nb * e_end).astype(BF16)
    kh_ref[0] = (k2 * e_end).astype(BF16)
    etot = jnp.exp(tot)
    for c in range(tt // chunk):
        wc_ref[0, c] = etot[c * chunk:c * chunk + 1]


def rw_prep(rw, prev0, p, chunk, tt):
    b, t, _ = rw.shape
    seq = lambda width, dt: jax.ShapeDtypeStruct((b, t, width), dt)
    tile = lambda width: pl.BlockSpec((1, tt, width), lambda i, j: (i, j, 0))
    vec = lambda width: _resident((1, width))
    n_c = tt // chunk
    return pl.pallas_call(
        functools.partial(_rw_prep_kernel, chunk),
        out_shape=tuple(seq(D_MIX, BF16) for _ in range(7))
        + (seq(D_MIX, F32), seq(D_MIX, F32), jax.ShapeDtypeStruct((b, t // chunk, 1, D_MIX), F32)),
        grid=(b, t // tt),
        in_specs=[tile(D_RW_COLS), pl.BlockSpec((1, 1, D_RW_COLS), lambda i, j: (i, 0, 0)),
                  vec(D_RW_COLS), vec(D_MIX), vec(D_MIX), vec(D_MIX), vec(D_MIX), vec(D_MIX),
                  _resident((LORA_W, D_MIX)), _resident((LORA_A, D_MIX)), _resident((LORA_G, D_MIX))],
        out_specs=tuple(tile(D_MIX) for _ in range(9))
        + (pl.BlockSpec((1, n_c, 1, D_MIX), lambda i, j: (i, j, 0, 0)),),
        scratch_shapes=[pltpu.VMEM((1, D_RW_COLS), F32)],
        compiler_params=_cparams("parallel", "arbitrary"),
        name="rw_prep",
    )(rw, prev0, p["mu"], p["w0"], p["a0"], p["k_k"], p["k_a"], p["r_k"],
      p["w_up"], p["a_up"], p["g_up"])


def _unit_lower_inverse(a, chunk):
    shape = a.shape
    eye = (_iota(shape, 1) % chunk == _iota(shape, 0)).astype(F32)
    res = eye + a
    power = a
    steps = chunk.bit_length() - 2
    for _ in range(steps):
        power = _dot3(power, _bd_rows(power, RW_GROUP, chunk))
        res = res + _dot3(res, _bd_rows(power, RW_GROUP, chunk))
    return res


def _rw_chunk_kernel(chunk, rt_ref, at_ref, bt_ref, kt_ref, bh_ref, kh_ref, v_ref, g_ref, bonus_ref,
                     wc_ref, s0_ref, gnw_ref, gnb_ref, o_ref, s_out_ref, st_ref):
    c = pl.program_id(1)

    @pl.when(c == 0)
    def _():
        st_ref[...] = s0_ref[0]

    cat = (chunk, RW_GROUP * chunk)
    col_t = _iota(cat, 1) % chunk
    strict = col_t < _iota(cat, 0)
    incl = col_t <= _iota(cat, 0)
    head_ones = _block_ones(RW_LANES, HEAD_DIM)
    y_parts = []
    for grp in range(N_HEADS // RW_GROUP):
        sl = slice(grp * RW_LANES, (grp + 1) * RW_LANES)
        rt, at, bt, kt = rt_ref[0, :, sl], at_ref[0, :, sl], bt_ref[0, :, sl], kt_ref[0, :, sl]
        bh, kh, v = bh_ref[0, :, sl], kh_ref[0, :, sl], v_ref[0, :, sl]
        st = st_ref[:, sl]

        lhs = jnp.concatenate([at, rt], axis=0)
        g_b = _dot(lhs, _bd_rows(bt, RW_GROUP, HEAD_DIM), _NT)
        g_k = _dot(lhs, _bd_rows(kt, RW_GROUP, HEAD_DIM), _NT)
        zero = jnp.zeros(cat, F32)
        a_ab = jnp.where(strict, g_b[:chunk], zero)
        a_rb = jnp.where(incl, g_b[chunk:], zero)
        a_ak = jnp.where(strict, g_k[:chunk], zero)
        a_rk = jnp.where(incl, g_k[chunk:], zero)

        t_inv = _unit_lower_inverse(a_ab, chunk)
        v_bd = _bd_rows(v, RW_GROUP, HEAD_DIM)
        x = _dot(a_ak, v_bd)
        u_hat = _dot3(t_inv, _bd_rows(x, RW_GROUP, HEAD_DIM))
        a_hat = _dot3(t_inv, _bd_rows(at.astype(F32), RW_GROUP, HEAD_DIM))
        st_bd = _bd_rows(st, RW_GROUP, HEAD_DIM)
        u = u_hat + _dot3(a_hat, st_bd)
        y_parts.append(_dot3(rt.astype(F32), st_bd) + _dot(a_rb, _bd_rows(u, RW_GROUP, HEAD_DIM))
                       + _dot(a_rk, v_bd))

        m = _dot(bh, u, _TN) + _dot(kh, v, _TN)
        lane_head = _iota((HEAD_DIM, RW_LANES), 1) // HEAD_DIM
        fold = jnp.zeros((HEAD_DIM, RW_LANES), F32)
        for h in range(RW_GROUP):
            fold = fold + jnp.where(lane_head == h, m[h * HEAD_DIM:(h + 1) * HEAD_DIM], 0.0)
        diag = _iota((HEAD_DIM, RW_LANES), 1) % HEAD_DIM == _iota((HEAD_DIM, RW_LANES), 0)
        w_rows = jnp.where(diag, wc_ref[0, 0, :, sl], 0.0)
        w_t = _dot_exact_rhs(w_rows, head_ones)
        st_ref[:, sl] = st * w_t + fold

    y = jnp.concatenate(y_parts, axis=1)
    ones512 = _block_ones(D_MIX, HEAD_DIM)
    mean = _dot_exact_rhs(y, ones512) * (1.0 / HEAD_DIM)
    d = y - mean
    var = _dot_exact_rhs(d * d, ones512) * (1.0 / HEAD_DIM)
    yn = d * lax.rsqrt(var + GN_EPS) * gnw_ref[...] + gnb_ref[...]
    o_ref[0] = ((yn + bonus_ref[0]) * g_ref[0]).astype(BF16)

    @pl.when(c == pl.num_programs(1) - 1)
    def _():
        s_out_ref[0] = st_ref[...]


def rw_chunks(prep, s0_t, gn_w, gn_b, chunk):
    rt, at, bt, kt, bh, kh, v, g, bonus, wc = prep
    b, t, _ = rt.shape
    tile = pl.BlockSpec((1, chunk, D_MIX), lambda i, j: (i, j, 0))
    state = pl.BlockSpec((1, HEAD_DIM, D_MIX), lambda i, j: (i, 0, 0))
    return pl.pallas_call(
        functools.partial(_rw_chunk_kernel, chunk),
        out_shape=(jax.ShapeDtypeStruct((b, t, D_MIX), BF16),
                   jax.ShapeDtypeStruct((b, HEAD_DIM, D_MIX), F32)),
        grid=(b, t // chunk),
        in_specs=[tile] * 9 + [pl.BlockSpec((1, 1, 1, D_MIX), lambda i, j: (i, j, 0, 0)), state,
                               _resident((1, D_MIX)), _resident((1, D_MIX))],
        out_specs=(tile, state),
        scratch_shapes=[pltpu.VMEM((HEAD_DIM, D_MIX), F32)],
        compiler_params=_cparams("parallel", "arbitrary"),
        name="rw_chunks",
    )(rt, at, bt, kt, bh, kh, v, g, bonus, wc, s0_t, gn_w, gn_b)


SB_PAIR = 2 * HEAD_DIM
SB_PAST_BLOCK = 256


def _sb_kernel(tq, n_past, q_ref, kn_ref, vn_ref, *rest):
    if n_past:
        kp_ref, vp_ref, o_ref, acc_ref, c_ref = rest
    else:
        o_ref, acc_ref, c_ref = rest
    qi = pl.program_id(2)
    q2 = q_ref[0]
    lane = _iota(q2.shape, 1)
    zero_q = jnp.zeros_like(q2)
    q_heads = (jnp.where(lane < HEAD_DIM, q2, zero_q), jnp.where(lane >= HEAD_DIM, q2, zero_q))
    acc_ref[...] = jnp.zeros_like(acc_ref)
    c_ref[...] = jnp.zeros_like(c_ref)

    def visit(k_blk, v_blk, masked):
        tk = k_blk.shape[0]
        kb = k_blk.astype(BF16)
        vb = v_blk.astype(BF16)
        later = (_iota((tk, tk), 0) > _iota((tk, tk), 1)).astype(BF16)
        before = _iota((tq, tk), 1) < _iota((tq, tk), 0)
        for hx in range(2):
            z = _dot(q_heads[hx], kb, _NT) * (HEAD_DIM ** -0.5)
            sp = jnp.maximum(z, 0.0) + jnp.log1p(jnp.exp(-jnp.abs(z)))
            lk = jnp.where(before, -sp, 0.0) if masked else -sp
            lrev = _dot_exact_rhs(lk, later, n=2)
            c = c_ref[hx]
            att = jnp.exp((z - sp) + (c + lrev))
            if masked:
                att = jnp.where(before, att, 0.0)
            acc_ref[hx] += _dot(att, vb)
            c_ref[hx] = c + (lrev[:, 0:1] + lk[:, 0:1])

    start = pl.multiple_of(qi * tq, tq)
    visit(kn_ref[0, pl.ds(start, tq), :], vn_ref[0, pl.ds(start, tq), :], True)

    def earlier_new(i, carry):
        s = pl.multiple_of((qi - 1 - i) * tq, tq)
        visit(kn_ref[0, pl.ds(s, tq), :], vn_ref[0, pl.ds(s, tq), :], False)
        return carry

    lax.fori_loop(0, qi, earlier_new, 0)

    if n_past:
        def past(i, carry):
            s = pl.multiple_of((n_past - 1 - i) * SB_PAST_BLOCK, SB_PAST_BLOCK)
            visit(kp_ref[0, pl.ds(s, SB_PAST_BLOCK), :], vp_ref[0, pl.ds(s, SB_PAST_BLOCK), :], False)
            return carry

        lax.fori_loop(0, n_past, past, 0)

    o_ref[0] = jnp.where(lane < HEAD_DIM, acc_ref[0], acc_ref[1]).astype(BF16)


def stick_breaking(q, k_new, v_new, k_past, v_past, tq):
    b, t, _ = q.shape
    n_pairs = D_MIX // SB_PAIR
    qspec = pl.BlockSpec((1, tq, SB_PAIR), lambda i, p, j: (i, j, p))
    seq = lambda length: pl.BlockSpec((1, length, SB_PAIR), lambda i, p, j: (i, 0, p))
    args, specs, n_past = [q, k_new, v_new], [qspec, seq(t), seq(t)], 0
    if k_past is not None:
        past_len = k_past.shape[1]
        n_past = past_len // SB_PAST_BLOCK
        args += [k_past, v_past]
        specs += [seq(past_len), seq(past_len)]
    return pl.pallas_call(
        functools.partial(_sb_kernel, tq, n_past),
        out_shape=jax.ShapeDtypeStruct((b, t, D_MIX), BF16),
        grid=(b, n_pairs, t // tq),
        in_specs=specs,
        out_specs=qspec,
        scratch_shapes=[pltpu.VMEM((2, tq, SB_PAIR), F32), pltpu.VMEM((2, tq, 1), F32)],
        compiler_params=_cparams("parallel", "parallel", "arbitrary"),
        name="stick_breaking",
    )(*args)


def _merge_kernel(orw_ref, osb_ref, gate_ref, x_ref, wb0_ref, wb1_ref, wout_ref, gx_ref, wxq_ref,
                  x1_ref, qx_ref):
    g = jax.nn.sigmoid(gate_ref[...].astype(F32))
    mixed = (g[:, :D_MODEL] * jnp.dot(orw_ref[...], wb0_ref[...], preferred_element_type=F32)
             + g[:, D_MODEL:] * jnp.dot(osb_ref[...], wb1_ref[...], preferred_element_type=F32))
    x1 = x_ref[...] + _dot(mixed, wout_ref[...])
    x1_ref[...] = x1
    qx_ref[...] = _dot(_rms(x1, gx_ref[...]), wxq_ref[...]).astype(BF16)


def merge_out(o_rw, o_sb, gate, x, wb0, wb1, wout, gx, wxq):
    n = x.shape[0]
    tm = min(ROW_TILE, n)
    row = lambda width: pl.BlockSpec((tm, width), lambda i: (i, 0))
    return pl.pallas_call(
        _merge_kernel,
        out_shape=(jax.ShapeDtypeStruct((n, D_MODEL), F32), jax.ShapeDtypeStruct((n, D_MODEL), BF16)),
        grid=(n // tm,),
        in_specs=[row(D_MIX), row(D_MIX), row(GATE_COLS), row(D_MODEL),
                  _resident((D_MIX, D_MODEL)), _resident((D_MIX, D_MODEL)), _resident((D_MODEL, D_MODEL)),
                  _resident((1, D_MODEL)), _resident((D_MODEL, D_MODEL))],
        out_specs=(row(D_MODEL), row(D_MODEL)),
        compiler_params=_cparams("parallel"),
        name="merge_out",
    )(o_rw, o_sb, gate, x, wb0, wb1, wout, gx.reshape(1, D_MODEL), wxq)


def _mem_kv_kernel(m_ref, g_ref, wk_ref, wv_ref, k_ref, v_ref):
    h = _rms(m_ref[...], g_ref[...]).astype(BF16)
    k_ref[...] = jnp.dot(h, wk_ref[...], preferred_element_type=F32)
    v_ref[...] = jnp.dot(h, wv_ref[...], preferred_element_type=F32)


def mem_kv(mem, g, wk, wv):
    n = mem.shape[0]
    tm = min(ROW_TILE, n)
    row = pl.BlockSpec((tm, D_MODEL), lambda i: (i, 0))
    return pl.pallas_call(
        _mem_kv_kernel,
        out_shape=(jax.ShapeDtypeStruct((n, D_MODEL), F32),) * 2,
        grid=(n // tm,),
        in_specs=[row, _resident((1, D_MODEL)), _resident((D_MODEL, D_MODEL)), _resident((D_MODEL, D_MODEL))],
        out_specs=(row, row),
        compiler_params=_cparams("parallel"),
        name="mem_kv",
    )(mem, g.reshape(1, D_MODEL), wk, wv)


ROUTER_LANES = 128


def _xattn_kernel(q_ref, x1_ref, mk_ref, mv_ref, wxo_ref, gf_ref, wr_ref, br_ref,
                  x2_ref, hf_ref, idx_ref, gates_ref):
    q = q_ref[0]
    heads = []
    for h in range(N_X_HEADS):
        sl = slice(h * X_HEAD_DIM, (h + 1) * X_HEAD_DIM)
        s = _dot(q[:, sl], mk_ref[0, :, sl], _NT) * (X_HEAD_DIM ** -0.5)
        e = jnp.exp(s - jnp.max(s, axis=-1, keepdims=True))
        p = e / jnp.sum(e, axis=-1, keepdims=True)
        heads.append(_dot(p, mv_ref[0, :, sl]))
    x2 = x1_ref[0] + _dot(jnp.concatenate(heads, axis=1), wxo_ref[...])
    x2_ref[0] = x2
    hf = _rms(x2, gf_ref[...])
    hf_ref[0] = hf
    logits = _dot3(hf, wr_ref[...]) + br_ref[...]
    lane = _iota(logits.shape, 1)
    vals, idx_out = [], jnp.zeros(logits.shape, jnp.int32)
    for j in range(TOP_K):
        m = jnp.max(logits, axis=-1, keepdims=True)
        pick = jnp.min(jnp.where(logits == m, lane, ROUTER_LANES), axis=-1, keepdims=True)
        vals.append(m)
        idx_out = jnp.where(lane == j, pick, idx_out)
        logits = jnp.where(lane == pick, -jnp.inf, logits)
    exps = [jnp.exp(v - vals[0]) for v in vals]
    denom = exps[0] + exps[1] + exps[2] + exps[3]
    gates = jnp.zeros(logits.shape, F32)
    for j in range(TOP_K):
        gates = jnp.where(lane == j, exps[j] / denom, gates)
    idx_ref[0] = idx_out
    gates_ref[0] = gates


def xattn_router(qx, x1, mk, mv, wxo, g_ffn, w_router, b_router):
    b, t, _ = x1.shape
    tm = min(ROW_TILE, t)
    n_mem = mk.shape[1]
    wr = jnp.zeros((D_MODEL, ROUTER_LANES), F32).at[:, :N_EXPERTS].set(w_router)
    br = jnp.full((1, ROUTER_LANES), -jnp.inf, F32).at[0, :N_EXPERTS].set(b_router)
    tile = pl.BlockSpec((1, tm, D_MODEL), lambda i, j: (i, j, 0))
    mem = pl.BlockSpec((1, n_mem, D_MODEL), lambda i, j: (i, 0, 0))
    small = pl.BlockSpec((1, tm, ROUTER_LANES), lambda i, j: (i, j, 0))
    return pl.pallas_call(
        _xattn_kernel,
        out_shape=(jax.ShapeDtypeStruct((b, t, D_MODEL), F32), jax.ShapeDtypeStruct((b, t, D_MODEL), F32),
                   jax.ShapeDtypeStruct((b, t, ROUTER_LANES), jnp.int32),
                   jax.ShapeDtypeStruct((b, t, ROUTER_LANES), F32)),
        grid=(b, t // tm),
        in_specs=[tile, tile, mem, mem, _resident((D_MODEL, D_MODEL)), _resident((1, D_MODEL)),
                  _resident((D_MODEL, ROUTER_LANES)), _resident((1, ROUTER_LANES))],
        out_specs=(tile, tile, small, small),
        compiler_params=_cparams("parallel", "parallel"),
        name="xattn_router",
    )(qx, x1, mk, mv, wxo, g_ffn.reshape(1, D_MODEL), wr, br)


def _moe_kernel(be_ref, nused_ref, tok_ref, hf_hbm, w1_ref, b1_ref, w2_ref, b2_ref, o_ref, xbuf, sem):
    i = pl.program_id(0)

    @pl.when(i < nused_ref[0])
    def _():
        def issue(r, carry):
            pltpu.make_async_copy(hf_hbm.at[pl.ds(tok_ref[0, 0, r], 1)], xbuf.at[pl.ds(r, 1)], sem).start()
            return carry

        lax.fori_loop(0, MOE_BLOCK, issue, 0)

        def drain(r, carry):
            pltpu.make_async_copy(hf_hbm.at[pl.ds(0, 1)], xbuf.at[pl.ds(r, 1)], sem).wait()
            return carry

        lax.fori_loop(0, MOE_BLOCK, drain, 0)
        hb = _dot(xbuf[...], w1_ref[0]) + b1_ref[0]
        glu = jnp.minimum(hb[:, :D_FF], SWIGLU_LIMIT)
        lin = jnp.clip(hb[:, D_FF:], -SWIGLU_LIMIT, SWIGLU_LIMIT)
        act = glu * jax.nn.sigmoid(SWIGLU_ALPHA * glu) * (lin + 1.0)
        o_ref[...] = _dot(act, w2_ref[0]) + b2_ref[0]

    @pl.when(i >= nused_ref[0])
    def _():
        o_ref[...] = jnp.zeros_like(o_ref)


def moe_experts(hf, row_tok, block_expert, n_used, w1, b1, w2, b2):
    n_blocks = block_expert.shape[0]
    grid_spec = pltpu.PrefetchScalarGridSpec(
        num_scalar_prefetch=2,
        grid=(n_blocks,),
        in_specs=[
            pl.BlockSpec((1, 1, MOE_BLOCK), lambda i, be, nu: (i, 0, 0), memory_space=pltpu.SMEM),
            pl.BlockSpec(memory_space=pl.ANY),
            pl.BlockSpec((1, D_MODEL, 2 * D_FF), lambda i, be, nu: (be[i], 0, 0)),
            pl.BlockSpec((1, 1, 2 * D_FF), lambda i, be, nu: (be[i], 0, 0)),
            pl.BlockSpec((1, D_FF, D_MODEL), lambda i, be, nu: (be[i], 0, 0)),
            pl.BlockSpec((1, 1, D_MODEL), lambda i, be, nu: (be[i], 0, 0)),
        ],
        out_specs=pl.BlockSpec((MOE_BLOCK, D_MODEL), lambda i, be, nu: (i, 0)),
        scratch_shapes=[pltpu.VMEM((MOE_BLOCK, D_MODEL), F32), pltpu.SemaphoreType.DMA(())],
    )
    return pl.pallas_call(
        _moe_kernel,
        out_shape=jax.ShapeDtypeStruct((n_blocks * MOE_BLOCK, D_MODEL), F32),
        grid_spec=grid_spec,
        compiler_params=_cparams("arbitrary"),
        name="moe_experts",
    )(block_expert, n_used, row_tok.reshape(n_blocks, 1, MOE_BLOCK), hf, w1,
      b1.reshape(N_EXPERTS, 1, 2 * D_FF), w2, b2.reshape(N_EXPERTS, 1, D_MODEL))


COMBINE_TILE = 128


def _combine_kernel(dest_ref, os_hbm, x2_ref, gates_ref, gfin_ref, y_ref, buf, sem):
    tm = x2_ref.shape[0]

    def issue(r, carry):
        pltpu.make_async_copy(os_hbm.at[pl.ds(dest_ref[0, 0, r], 1)], buf.at[pl.ds(r, 1)], sem).start()
        return carry

    lax.fori_loop(0, TOP_K * tm, issue, 0)

    def drain(r, carry):
        pltpu.make_async_copy(os_hbm.at[pl.ds(0, 1)], buf.at[pl.ds(r, 1)], sem).wait()
        return carry

    lax.fori_loop(0, TOP_K * tm, drain, 0)
    y = x2_ref[...]
    gates = gates_ref[...]
    for j in range(TOP_K):
        y = y + gates[:, j:j + 1] * buf[j * tm:(j + 1) * tm, :]
    y_ref[...] = _rms(y, gfin_ref[...])


def moe_combine(os_rows, dest, x2, gates, g_final):
    n = x2.shape[0]
    tm = min(COMBINE_TILE, n)
    dest_t = dest.reshape(n // tm, tm, TOP_K).transpose(0, 2, 1).reshape(n // tm, 1, TOP_K * tm)
    return pl.pallas_call(
        _combine_kernel,
        out_shape=jax.ShapeDtypeStruct((n, D_MODEL), F32),
        grid=(n // tm,),
        in_specs=[pl.BlockSpec((1, 1, TOP_K * tm), lambda i: (i, 0, 0), memory_space=pltpu.SMEM),
                  pl.BlockSpec(memory_space=pl.ANY),
                  pl.BlockSpec((tm, D_MODEL), lambda i: (i, 0)),
                  pl.BlockSpec((tm, ROUTER_LANES), lambda i: (i, 0)),
                  _resident((1, D_MODEL))],
        out_specs=pl.BlockSpec((tm, D_MODEL), lambda i: (i, 0)),
        scratch_shapes=[pltpu.VMEM((TOP_K * tm, D_MODEL), F32), pltpu.SemaphoreType.DMA(())],
        compiler_params=_cparams("arbitrary"),
        name="moe_combine",
    )(dest_t, os_rows, x2, gates, g_final.reshape(1, D_MODEL))


def _routing(top_idx, n_blocks):
    n = top_idx.shape[0]
    m = n * TOP_K
    e_flat = top_idx.reshape(-1)
    onehot = (e_flat[:, None] == jnp.arange(N_EXPERTS, dtype=jnp.int32)[None, :]).astype(jnp.int32)
    csum = jnp.cumsum(onehot, axis=0)
    counts = csum[-1]
    rank = jnp.take_along_axis(csum, e_flat[:, None], axis=1)[:, 0] - 1
    padded = (counts + MOE_BLOCK - 1) // MOE_BLOCK * MOE_BLOCK
    pends = jnp.cumsum(padded)
    dest = (pends - padded)[e_flat] + rank
    row_tok = jnp.zeros((n_blocks * MOE_BLOCK,), jnp.int32).at[dest].set(
        jnp.arange(m, dtype=jnp.int32) // TOP_K)
    block_expert = jnp.minimum(
        jnp.searchsorted(pends, jnp.arange(n_blocks, dtype=jnp.int32) * MOE_BLOCK, side='right'),
        N_EXPERTS - 1).astype(jnp.int32)
    n_used = (pends[-1] // MOE_BLOCK).astype(jnp.int32).reshape(1)
    return row_tok, block_expert, n_used, dest.reshape(n, TOP_K)


def _layer(x, prev0, s0_t, k_past, v_past, mk, mv, w, chunk, prep_tile, tq):
    b, t, _ = x.shape
    n = b * t
    xf = x.reshape(n, D_MODEL)
    q, k_sb, v_sb, rw, gate = in_proj(xf, w["g_mix"], w["w_in"])
    shift = norm_rows(x[:, -1], w["g_mix"])
    prep = rw_prep(rw.reshape(b, t, D_RW_COLS), prev0, w["rw"], chunk, prep_tile)
    o_rw, s_t = rw_chunks(prep, s0_t, w["gn_w"], w["gn_b"], chunk)
    seq = lambda a: a.reshape(b, t, D_MIX)
    o_sb = stick_breaking(seq(q), seq(k_sb), seq(v_sb), k_past, v_past, tq)
    x1, qx = merge_out(o_rw.reshape(n, D_MIX), o_sb.reshape(n, D_MIX), gate, xf,
                       w["wb0"], w["wb1"], w["w_out"], w["g_xattn"], w["w_xq"])
    x2, hf, idx, gates = xattn_router(qx.reshape(b, t, D_MODEL), x1.reshape(b, t, D_MODEL), mk, mv,
                                      w["w_xo"], w["g_ffn"], w["w_router"], w["b_router"])
    n_blocks = -(-(n * TOP_K + N_EXPERTS * (MOE_BLOCK - 1)) // MOE_BLOCK)
    row_tok, block_expert, n_used, dest = _routing(idx.reshape(n, ROUTER_LANES)[:, :TOP_K], n_blocks)
    os_rows = moe_experts(hf.reshape(n, D_MODEL), row_tok, block_expert, n_used,
                          w["w_e1"], w["b_e1"], w["w_e2"], w["b_e2"])
    y = moe_combine(os_rows, dest, x2.reshape(n, D_MODEL), gates.reshape(n, ROUTER_LANES), w["g_final"])
    return y.reshape(b, t, D_MODEL), shift, s_t, k_sb, v_sb


def _state_to_t(s):
    b = s.shape[0]
    return s.transpose(0, 3, 1, 2).reshape(b, HEAD_DIM, D_MIX)


def _state_from_t(s_t):
    b = s_t.shape[0]
    return s_t.reshape(b, HEAD_DIM, N_HEADS, HEAD_DIM).transpose(0, 2, 3, 1)


def kernel(x_prompt, x_sample, state_rw_shift, state_rw_wkv, cache_sb_k, cache_sb_v, cache_mem_k, cache_mem_v, mem_prompt, g_mix, w_in, rw_mu, rw_w0, rw_w_up, rw_a0, rw_a_up, rw_g_up, rw_k_k, rw_k_a, rw_r_k, rw_gn_w, rw_gn_b, w_branch, w_out, g_xattn, g_mem, w_xq, w_mk, w_mv, w_xo, g_ffn, w_router, b_router, w_e1, b_e1, w_e2, b_e2, g_final):
    assert g_mix.shape[0] == 1, "single-layer trunk"
    row = lambda a: a.reshape(1, -1)
    w = dict(
        g_mix=g_mix[0], w_in=w_in[0].astype(BF16),
        rw=dict(mu=row(rw_mu[0]), w0=row(rw_w0[0]), a0=row(rw_a0[0]), k_k=row(rw_k_k[0]), k_a=row(rw_k_a[0]),
                r_k=row(rw_r_k[0]), w_up=rw_w_up[0].astype(BF16), a_up=rw_a_up[0].astype(BF16),
                g_up=rw_g_up[0].astype(BF16)),
        gn_w=row(rw_gn_w[0]), gn_b=row(rw_gn_b[0]),
        wb0=w_branch[0, 0].astype(BF16), wb1=w_branch[0, 1].astype(BF16), w_out=w_out[0].astype(BF16),
        g_xattn=g_xattn[0], w_xq=w_xq[0].astype(BF16), w_xo=w_xo[0].astype(BF16),
        g_ffn=g_ffn[0], w_router=w_router[0], b_router=b_router[0],
        w_e1=w_e1[0].astype(BF16), b_e1=b_e1[0], w_e2=w_e2[0].astype(BF16), b_e2=b_e2[0],
        g_final=g_final,
    )
    bp, t, _ = x_prompt.shape
    bs, ts, _ = x_sample.shape
    n_mem = mem_prompt.shape[1]

    mk_p, mv_p = mem_kv(mem_prompt.reshape(bp * n_mem, D_MODEL), g_mem[0],
                        w_mk[0].astype(BF16), w_mv[0].astype(BF16))
    mk_p = mk_p.reshape(bp, n_mem, D_MODEL)
    mv_p = mv_p.reshape(bp, n_mem, D_MODEL)
    y_p, sh_p, st_p, k_p, v_p = _layer(
        x_prompt, jnp.zeros((bp, 1, D_RW_COLS), F32), jnp.zeros((bp, HEAD_DIM, D_MIX), F32),
        None, None, mk_p, mv_p, w, chunk=64, prep_tile=256, tq=256)

    prev_s = plain_proj(state_rw_shift[0], w["w_in"][:, SB_COLS:SB_COLS + D_RW_COLS])
    past = cache_sb_k.shape[2]
    y_s, sh_s, st_s, k_s, v_s = _layer(
        x_sample, prev_s.reshape(bs, 1, D_RW_COLS), _state_to_t(state_rw_wkv[0]),
        cache_sb_k[0].reshape(bs, past, D_MIX), cache_sb_v[0].reshape(bs, past, D_MIX),
        cache_mem_k[0].reshape(bs, n_mem, D_MODEL), cache_mem_v[0].reshape(bs, n_mem, D_MODEL),
        w, chunk=ts, prep_tile=ts, tq=ts)

    heads = lambda a, b_, t_: a.reshape(1, b_, t_, N_HEADS, HEAD_DIM)
    xh = lambda a: a.reshape(1, bp, n_mem, N_X_HEADS, X_HEAD_DIM)
    return (y_p, y_s,
            sh_p[None], _state_from_t(st_p)[None], heads(k_p, bp, t), heads(v_p, bp, t), xh(mk_p), xh(mv_p),
            sh_s[None], _state_from_t(st_s)[None], heads(k_s, bs, ts), heads(v_s, bs, ts))
```

```python
import functools

import jax
import jax.numpy as jnp
from jax import lax
from jax.experimental import pallas as pl
from jax.experimental.pallas import tpu as pltpu

F32 = jnp.float32
BF16 = jnp.bfloat16

D_MODEL = 1024
HEAD_DIM = 64
N_HEADS = 8
D_MIX = N_HEADS * HEAD_DIM
LORA_W, LORA_A, LORA_G = 64, 64, 128
D_RW_COLS = 3 * D_MIX + LORA_W + LORA_A + LORA_G
N_X_HEADS = 4
X_HEAD_DIM = 256
N_EXPERTS = 32
TOP_K = 4
D_FF = 1024
SWIGLU_LIMIT = 7.0
SWIGLU_ALPHA = 1.702
MOE_BLOCK = 256
RMS_EPS = 1e-5
GN_EPS = 64e-5

ROW_TILE = 256
RW_GROUP = 4
RW_LANES = RW_GROUP * HEAD_DIM
VMEM_LIMIT = 56 * 1024 * 1024


def _cparams(*sem):
    return pltpu.CompilerParams(dimension_semantics=sem, vmem_limit_bytes=VMEM_LIMIT)


def _cparams_gather():
    return pltpu.CompilerParams(dimension_semantics=("arbitrary",), vmem_limit_bytes=VMEM_LIMIT,
                                disable_bounds_checks=True)


def _resident(shape):
    nd = len(shape)
    return pl.BlockSpec(shape, lambda *_: (0,) * nd)


_NN = ((1,), (0,))
_NT = ((1,), (1,))
_TN = ((0,), (0,))


def _dot(a, b, dims=_NN):
    return lax.dot_general(a.astype(BF16), b.astype(BF16), (dims, ((), ())),
                           preferred_element_type=F32)


def _split(x, n):
    parts, rem = [], x
    for i in range(n):
        p = rem.astype(BF16)
        parts.append(p)
        if i + 1 < n:
            rem = rem - p.astype(F32)
    return parts


def _dot_exact_rhs(a, b_bf16, dims=_NN, n=3):
    out = None
    for p in _split(a, n):
        t = lax.dot_general(p, b_bf16, (dims, ((), ())), preferred_element_type=F32)
        out = t if out is None else out + t
    return out


def _dot_exact_lhs(a_bf16, b, dims=_NN, n=3):
    out = None
    for p in _split(b, n):
        t = lax.dot_general(a_bf16, p, (dims, ((), ())), preferred_element_type=F32)
        out = t if out is None else out + t
    return out


def _dot3(a, b, dims=_NN):
    ah, al = _split(a, 2)
    bh, bl = _split(b, 2)
    dn = (dims, ((), ()))
    return (lax.dot_general(ah, bh, dn, preferred_element_type=F32)
            + (lax.dot_general(ah, bl, dn, preferred_element_type=F32)
               + lax.dot_general(al, bh, dn, preferred_element_type=F32)))


def _rms(x, g):
    return x * lax.rsqrt(jnp.mean(x * x, axis=-1, keepdims=True) + RMS_EPS) * g


def _iota(shape, dim):
    return lax.broadcasted_iota(jnp.int32, shape, dim)


def _block_ones(n, width):
    return (_iota((n, n), 0) // width == _iota((n, n), 1) // width).astype(BF16)


def _bd_rows(x, nblk, width):
    blk = _iota(x.shape, 1) // width
    zero = jnp.zeros_like(x)
    return jnp.concatenate([jnp.where(blk == h, x, zero) for h in range(nblk)], axis=0)


def _norm_rows_kernel(x_ref, g_ref, o_ref):
    o_ref[...] = _rms(x_ref[...], g_ref[...])


def norm_rows(x, g):
    r, d = x.shape
    return pl.pallas_call(
        _norm_rows_kernel,
        out_shape=jax.ShapeDtypeStruct((r, d), F32),
        grid=(1,),
        in_specs=[_resident((r, d)), _resident((1, d))],
        out_specs=_resident((r, d)),
        name="norm_rows",
    )(x, g.reshape(1, d))


SB_COLS = 3 * D_MIX
GATE_COLS = 2 * D_MODEL
D_IN = SB_COLS + D_RW_COLS + GATE_COLS


def _in_proj_kernel(x_ref, g_ref, w_ref, q_ref, k_ref, v_ref, rw_ref, gate_ref):
    h = _rms(x_ref[...], g_ref[...]).astype(BF16)

    def proj(lo, hi):
        return jnp.dot(h, w_ref[:, lo:hi], preferred_element_type=F32)

    q_ref[...] = proj(0, D_MIX).astype(BF16)
    k_ref[...] = proj(D_MIX, 2 * D_MIX)
    v_ref[...] = proj(2 * D_MIX, SB_COLS)
    rw_ref[...] = proj(SB_COLS, SB_COLS + D_RW_COLS)
    gate_ref[...] = proj(SB_COLS + D_RW_COLS, D_IN).astype(BF16)


def in_proj(x, g, w_bf16):
    n = x.shape[0]
    tm = min(ROW_TILE, n)
    row = lambda width: pl.BlockSpec((tm, width), lambda i: (i, 0))
    return pl.pallas_call(
        _in_proj_kernel,
        out_shape=(jax.ShapeDtypeStruct((n, D_MIX), BF16),
                   jax.ShapeDtypeStruct((n, D_MIX), F32),
                   jax.ShapeDtypeStruct((n, D_MIX), F32),
                   jax.ShapeDtypeStruct((n, D_RW_COLS), F32),
                   jax.ShapeDtypeStruct((n, GATE_COLS), BF16)),
        grid=(n // tm,),
        in_specs=[row(D_MODEL), _resident((1, D_MODEL)), _resident((D_MODEL, D_IN))],
        out_specs=(row(D_MIX), row(D_MIX), row(D_MIX), row(D_RW_COLS), row(GATE_COLS)),
        compiler_params=_cparams("parallel"),
        name="in_proj",
    )(x, g.reshape(1, D_MODEL), w_bf16)


def _plain_proj_kernel(x_ref, w_ref, o_ref):
    o_ref[...] = _dot(x_ref[...], w_ref[...])


def plain_proj(x, w_bf16):
    r, m = x.shape[0], w_bf16.shape[1]
    return pl.pallas_call(
        _plain_proj_kernel,
        out_shape=jax.ShapeDtypeStruct((r, m), F32),
        grid=(1,),
        in_specs=[_resident(x.shape), _resident(w_bf16.shape)],
        out_specs=_resident((r, m)),
        name="plain_proj",
    )(x, w_bf16)


def _rw_prep_kernel(chunk, rw_ref, prev0_ref, mu_ref, w0_ref, a0_ref, kk_ref, ka_ref, rk_ref,
                    wup_ref, aup_ref, gup_ref,
                    rt_ref, at_ref, bt_ref, kt_ref, bh_ref, kh_ref, v_ref, g_ref, bonus_ref, wc_ref,
                    carry_ref):
    tt = rw_ref.shape[1]

    @pl.when(pl.program_id(1) == 0)
    def _():
        carry_ref[...] = prev0_ref[0]

    p = rw_ref[0]
    prev = jnp.where(_iota(p.shape, 0) == 0, carry_ref[...], pltpu.roll(p, 1, 0))
    carry_ref[...] = p[tt - 1:tt]
    xs = p + mu_ref[...] * (prev - p)
    r = xs[:, 0:D_MIX]
    k = xs[:, D_MIX:2 * D_MIX]
    v = xs[:, 2 * D_MIX:3 * D_MIX]
    lo = 3 * D_MIX
    wd = xs[:, lo:lo + LORA_W]
    ad = xs[:, lo + LORA_W:lo + LORA_W + LORA_A]
    gd = xs[:, lo + LORA_W + LORA_A:D_RW_COLS]

    pre_w = w0_ref[...] + _dot(jnp.tanh(wd), wup_ref[...])
    w_log = -jax.nn.softplus(-pre_w) - 0.5
    logw = -jnp.exp(w_log)
    a = jax.nn.sigmoid(a0_ref[...] + _dot(ad, aup_ref[...]))
    g_ref[0] = _dot(jax.nn.sigmoid(gd), gup_ref[...])

    head_ones = _block_ones(D_MIX, HEAD_DIM)
    kk = k * kk_ref[...]
    ss = _dot_exact_rhs(kk * kk, head_ones)
    kk = kk / jnp.maximum(jnp.sqrt(ss), 1e-12)
    k2 = k * (1.0 + (a - 1.0) * ka_ref[...])
    nb = -(kk * a)
    bonus_ref[0] = _dot_exact_rhs(r * k2 * rk_ref[...], head_ones) * v
    v_ref[0] = v.astype(BF16)

    ti, tj = _iota((tt, tt), 0), _iota((tt, tt), 1)
    same = ti // chunk == tj // chunk
    cum = _dot_exact_lhs((same & (tj <= ti)).astype(BF16), logw)
    tot = _dot_exact_lhs(same.astype(BF16), logw)
    e_in = jnp.exp(cum)
    e_out = jnp.exp(-cum)
    e_end = jnp.exp(tot - cum)
    rt_ref[0] = (r * e_in).astype(BF16)
    at_ref[0] = (kk * jnp.exp(cum - logw)).astype(BF16)
    bt_ref[0] = (nb * e_out).astype(BF16)
    kt_ref[0] = (k2 * e_out).astype(BF16)
    bh_ref[0] = (nb * e_end).astype(BF16)
    kh_ref[0] = (k2 * e_end).astype(BF16)
    etot = jnp.exp(tot)
    for c in range(tt // chunk):
        wc_ref[0, c] = etot[c * chunk:c * chunk + 1]


def rw_prep(rw, prev0, p, chunk, tt):
    b, t, _ = rw.shape
    seq = lambda width, dt: jax.ShapeDtypeStruct((b, t, width), dt)
    tile = lambda width: pl.BlockSpec((1, tt, width), lambda i, j: (i, j, 0))
    vec = lambda width: _resident((1, width))
    n_c = tt // chunk
    return pl.pallas_call(
        functools.partial(_rw_prep_kernel, chunk),
        out_shape=tuple(seq(D_MIX, BF16) for _ in range(7))
        + (seq(D_MIX, F32), seq(D_MIX, F32), jax.ShapeDtypeStruct((b, t // chunk, 1, D_MIX), F32)),
        grid=(b, t // tt),
        in_specs=[tile(D_RW_COLS), pl.BlockSpec((1, 1, D_RW_COLS), lambda i, j: (i, 0, 0)),
                  vec(D_RW_COLS), vec(D_MIX), vec(D_MIX), vec(D_MIX), vec(D_MIX), vec(D_MIX),
                  _resident((LORA_W, D_MIX)), _resident((LORA_A, D_MIX)), _resident((LORA_G, D_MIX))],
        out_specs=tuple(tile(D_MIX) for _ in range(9))
        + (pl.BlockSpec((1, n_c, 1, D_MIX), lambda i, j: (i, j, 0, 0)),),
        scratch_shapes=[pltpu.VMEM((1, D_RW_COLS), F32)],
        compiler_params=_cparams("parallel", "arbitrary"),
        name="rw_prep",
    )(rw, prev0, p["mu"], p["w0"], p["a0"], p["k_k"], p["k_a"], p["r_k"],
      p["w_up"], p["a_up"], p["g_up"])


def _unit_lower_inverse(a, chunk):
    shape = a.shape
    eye = (_iota(shape, 1) % chunk == _iota(shape, 0)).astype(F32)
    res = eye + a
    power = a.astype(BF16)
    for _ in range(chunk.bit_length() - 2):
        power = _dot(power, _bd_rows(power, RW_GROUP, chunk)).astype(BF16)
        res = res + _dot(res, _bd_rows(power, RW_GROUP, chunk))
    return res


def _rw_chunk_kernel(chunk, rt_ref, at_ref, bt_ref, kt_ref, bh_ref, kh_ref, v_ref, g_ref, bonus_ref,
                     wc_ref, s0_ref, gnw_ref, gnb_ref, o_ref, s_out_ref, st_ref):
    c = pl.program_id(1)

    @pl.when(c == 0)
    def _():
        st_ref[...] = s0_ref[0]

    cat = (chunk, RW_GROUP * chunk)
    col_t = _iota(cat, 1) % chunk
    strict = col_t < _iota(cat, 0)
    incl = col_t <= _iota(cat, 0)
    head_ones = _block_ones(RW_LANES, HEAD_DIM)
    bd = lambda x: _bd_rows(x, RW_GROUP, HEAD_DIM)
    n_cat = RW_GROUP * chunk
    for grp in range(N_HEADS // RW_GROUP):
        sl = slice(grp * RW_LANES, (grp + 1) * RW_LANES)
        rt, at, bt, kt = rt_ref[0, :, sl], at_ref[0, :, sl], bt_ref[0, :, sl], kt_ref[0, :, sl]
        bh, kh, v = bh_ref[0, :, sl], kh_ref[0, :, sl], v_ref[0, :, sl]
        st = st_ref[:, sl]

        gram = _dot(jnp.concatenate([at, rt], axis=0),
                    jnp.concatenate([bd(bt), bd(kt)], axis=0), _NT)
        zero = jnp.zeros(cat, F32)
        a_ab = jnp.where(strict, gram[:chunk, :n_cat], zero)
        a_ak = jnp.where(strict, gram[:chunk, n_cat:], zero)
        a_r = jnp.concatenate([jnp.where(incl, gram[chunk:, :n_cat], zero),
                               jnp.where(incl, gram[chunk:, n_cat:], zero)], axis=1).astype(BF16)

        t_inv = _unit_lower_inverse(a_ab, chunk)
        v_bd = bd(v)
        x = _dot(a_ak, v_bd).astype(BF16)
        ua = _dot(t_inv, jnp.concatenate([bd(x), bd(at)], axis=1))
        st_bd = bd(st.astype(BF16))
        from_state = _dot(jnp.concatenate([ua[:, RW_LANES:].astype(BF16), rt], axis=0), st_bd)
        u = (ua[:, :RW_LANES] + from_state[:chunk]).astype(BF16)
        y = from_state[chunk:] + _dot(a_r, jnp.concatenate([bd(u), v_bd], axis=0))

        m = _dot(jnp.concatenate([bh, kh], axis=0), jnp.concatenate([u, v], axis=0), _TN)
        lane_head = _iota((HEAD_DIM, RW_LANES), 1) // HEAD_DIM
        fold = jnp.zeros((HEAD_DIM, RW_LANES), F32)
        for h in range(RW_GROUP):
            fold = fold + jnp.where(lane_head == h, m[h * HEAD_DIM:(h + 1) * HEAD_DIM], 0.0)
        diag = _iota((HEAD_DIM, RW_LANES), 1) % HEAD_DIM == _iota((HEAD_DIM, RW_LANES), 0)
        w_rows = jnp.where(diag, wc_ref[0, 0, :, sl], 0.0)
        w_t = _dot_exact_rhs(w_rows, head_ones)
        st_ref[:, sl] = st * w_t + fold

        mean = _dot_exact_rhs(y, head_ones, n=2) * (1.0 / HEAD_DIM)
        d = y - mean
        var = _dot_exact_rhs(d * d, head_ones, n=2) * (1.0 / HEAD_DIM)
        yn = d * lax.rsqrt(var + GN_EPS) * gnw_ref[:, sl] + gnb_ref[:, sl]
        o_ref[0, :, sl] = ((yn + bonus_ref[0, :, sl]) * g_ref[0, :, sl]).astype(BF16)

    @pl.when(c == pl.num_programs(1) - 1)
    def _():
        s_out_ref[0] = st_ref[...]


def rw_chunks(prep, s0_t, gn_w, gn_b, chunk):
    rt, at, bt, kt, bh, kh, v, g, bonus, wc = prep
    b, t, _ = rt.shape
    tile = pl.BlockSpec((1, chunk, D_MIX), lambda i, j: (i, j, 0))
    state = pl.BlockSpec((1, HEAD_DIM, D_MIX), lambda i, j: (i, 0, 0))
    return pl.pallas_call(
        functools.partial(_rw_chunk_kernel, chunk),
        out_shape=(jax.ShapeDtypeStruct((b, t, D_MIX), BF16),
                   jax.ShapeDtypeStruct((b, HEAD_DIM, D_MIX), F32)),
        grid=(b, t // chunk),
        in_specs=[tile] * 9 + [pl.BlockSpec((1, 1, 1, D_MIX), lambda i, j: (i, j, 0, 0)), state,
                               _resident((1, D_MIX)), _resident((1, D_MIX))],
        out_specs=(tile, state),
        scratch_shapes=[pltpu.VMEM((HEAD_DIM, D_MIX), F32)],
        compiler_params=_cparams("parallel", "arbitrary"),
        name="rw_chunks",
    )(rt, at, bt, kt, bh, kh, v, g, bonus, wc, s0_t, gn_w, gn_b)


SB_PAIR = 2 * HEAD_DIM
SB_PAST_BLOCK = 256


def _sb_kernel(tq, n_past, q_ref, kn_ref, vn_ref, *rest):
    if n_past:
        kp_ref, vp_ref, o_ref, acc_ref, c_ref = rest
    else:
        o_ref, acc_ref, c_ref = rest
    qi = pl.program_id(2)
    q2 = q_ref[0]
    lane = _iota(q2.shape, 1)
    zero_q = jnp.zeros_like(q2)
    q_heads = (jnp.where(lane < HEAD_DIM, q2, zero_q), jnp.where(lane >= HEAD_DIM, q2, zero_q))
    acc_ref[...] = jnp.zeros_like(acc_ref)
    c_ref[...] = jnp.zeros_like(c_ref)

    def visit(k_blk, v_blk, masked):
        tk = k_blk.shape[0]
        kb = k_blk.astype(BF16)
        vb = v_blk.astype(BF16)
        later = (_iota((tk, tk), 0) > _iota((tk, tk), 1)).astype(BF16)
        before = _iota((tq, tk), 1) < _iota((tq, tk), 0)
        for hx in range(2):
            z = _dot(q_heads[hx], kb, _NT) * (HEAD_DIM ** -0.5)
            sp = jnp.maximum(z, 0.0) + jnp.log1p(jnp.exp(-jnp.abs(z)))
            lk = jnp.where(before, -sp, 0.0) if masked else -sp
            lrev = _dot_exact_rhs(lk, later, n=2)
            c = c_ref[hx]
            att = jnp.exp((z - sp) + (c + lrev))
            if masked:
                att = jnp.where(before, att, 0.0)
            acc_ref[hx] += _dot(att, vb)
            c_ref[hx] = c + (lrev[:, 0:1] + lk[:, 0:1])

    start = pl.multiple_of(qi * tq, tq)
    visit(kn_ref[0, pl.ds(start, tq), :], vn_ref[0, pl.ds(start, tq), :], True)

    def earlier_new(i, carry):
        s = pl.multiple_of((qi - 1 - i) * tq, tq)
        visit(kn_ref[0, pl.ds(s, tq), :], vn_ref[0, pl.ds(s, tq), :], False)
        return carry

    lax.fori_loop(0, qi, earlier_new, 0)

    if n_past:
        def past(i, carry):
            s = pl.multiple_of((n_past - 1 - i) * SB_PAST_BLOCK, SB_PAST_BLOCK)
            visit(kp_ref[0, pl.ds(s, SB_PAST_BLOCK), :], vp_ref[0, pl.ds(s, SB_PAST_BLOCK), :], False)
            return carry

        lax.fori_loop(0, n_past, past, 0)

    o_ref[0] = jnp.where(lane < HEAD_DIM, acc_ref[0], acc_ref[1]).astype(BF16)


def stick_breaking(q, k_new, v_new, k_past, v_past, tq):
    b, t, _ = q.shape
    n_pairs = D_MIX // SB_PAIR
    qspec = pl.BlockSpec((1, tq, SB_PAIR), lambda i, p, j: (i, j, p))
    seq = lambda length: pl.BlockSpec((1, length, SB_PAIR), lambda i, p, j: (i, 0, p))
    args, specs, n_past = [q, k_new, v_new], [qspec, seq(t), seq(t)], 0
    if k_past is not None:
        past_len = k_past.shape[1]
        n_past = past_len // SB_PAST_BLOCK
        args += [k_past, v_past]
        specs += [seq(past_len), seq(past_len)]
    return pl.pallas_call(
        functools.partial(_sb_kernel, tq, n_past),
        out_shape=jax.ShapeDtypeStruct((b, t, D_MIX), BF16),
        grid=(b, n_pairs, t // tq),
        in_specs=specs,
        out_specs=qspec,
        scratch_shapes=[pltpu.VMEM((2, tq, SB_PAIR), F32), pltpu.VMEM((2, tq, 1), F32)],
        compiler_params=_cparams("parallel", "parallel", "arbitrary"),
        name="stick_breaking",
    )(*args)


def _merge_kernel(orw_ref, osb_ref, gate_ref, x_ref, wb0_ref, wb1_ref, wout_ref, gx_ref, wxq_ref,
                  x1_ref, qx_ref):
    g = jax.nn.sigmoid(gate_ref[...].astype(F32))
    mixed = (g[:, :D_MODEL] * jnp.dot(orw_ref[...], wb0_ref[...], preferred_element_type=F32)
             + g[:, D_MODEL:] * jnp.dot(osb_ref[...], wb1_ref[...], preferred_element_type=F32))
    x1 = x_ref[...] + _dot(mixed, wout_ref[...])
    x1_ref[...] = x1
    qx_ref[...] = _dot(_rms(x1, gx_ref[...]), wxq_ref[...]).astype(BF16)


def merge_out(o_rw, o_sb, gate, x, wb0, wb1, wout, gx, wxq):
    n = x.shape[0]
    tm = min(ROW_TILE, n)
    row = lambda width: pl.BlockSpec((tm, width), lambda i: (i, 0))
    return pl.pallas_call(
        _merge_kernel,
        out_shape=(jax.ShapeDtypeStruct((n, D_MODEL), F32), jax.ShapeDtypeStruct((n, D_MODEL), BF16)),
        grid=(n // tm,),
        in_specs=[row(D_MIX), row(D_MIX), row(GATE_COLS), row(D_MODEL),
                  _resident((D_MIX, D_MODEL)), _resident((D_MIX, D_MODEL)), _resident((D_MODEL, D_MODEL)),
                  _resident((1, D_MODEL)), _resident((D_MODEL, D_MODEL))],
        out_specs=(row(D_MODEL), row(D_MODEL)),
        compiler_params=_cparams("parallel"),
        name="merge_out",
    )(o_rw, o_sb, gate, x, wb0, wb1, wout, gx.reshape(1, D_MODEL), wxq)


def _mem_kv_kernel(m_ref, g_ref, wk_ref, wv_ref, k_ref, v_ref):
    h = _rms(m_ref[...], g_ref[...]).astype(BF16)
    k_ref[...] = jnp.dot(h, wk_ref[...], preferred_element_type=F32)
    v_ref[...] = jnp.dot(h, wv_ref[...], preferred_element_type=F32)


def mem_kv(mem, g, wk, wv):
    n = mem.shape[0]
    tm = min(ROW_TILE, n)
    row = pl.BlockSpec((tm, D_MODEL), lambda i: (i, 0))
    return pl.pallas_call(
        _mem_kv_kernel,
        out_shape=(jax.ShapeDtypeStruct((n, D_MODEL), F32),) * 2,
        grid=(n // tm,),
        in_specs=[row, _resident((1, D_MODEL)), _resident((D_MODEL, D_MODEL)), _resident((D_MODEL, D_MODEL))],
        out_specs=(row, row),
        compiler_params=_cparams("parallel"),
        name="mem_kv",
    )(mem, g.reshape(1, D_MODEL), wk, wv)


ROUTER_LANES = 128


def _xattn_kernel(q_ref, x1_ref, mk_ref, mv_ref, wxo_ref, gf_ref, wr_ref, br_ref,
                  x2_ref, hf_ref, idx_ref, gates_ref):
    q = q_ref[0]
    heads = []
    for h in range(N_X_HEADS):
        sl = slice(h * X_HEAD_DIM, (h + 1) * X_HEAD_DIM)
        s = _dot(q[:, sl], mk_ref[0, :, sl], _NT) * (X_HEAD_DIM ** -0.5)
        e = jnp.exp(s - jnp.max(s, axis=-1, keepdims=True))
        p = e / jnp.sum(e, axis=-1, keepdims=True)
        heads.append(_dot(p, mv_ref[0, :, sl]))
    x2 = x1_ref[0] + _dot(jnp.concatenate(heads, axis=1), wxo_ref[...])
    x2_ref[0] = x2
    hf = _rms(x2, gf_ref[...])
    hf_ref[0] = hf
    logits = _dot3(hf, wr_ref[...]) + br_ref[...]
    lane = _iota(logits.shape, 1)
    vals, idx_out = [], jnp.zeros(logits.shape, jnp.int32)
    for j in range(TOP_K):
        m = jnp.max(logits, axis=-1, keepdims=True)
        pick = jnp.min(jnp.where(logits == m, lane, ROUTER_LANES), axis=-1, keepdims=True)
        vals.append(m)
        idx_out = jnp.where(lane == j, pick, idx_out)
        logits = jnp.where(lane == pick, -jnp.inf, logits)
    exps = [jnp.exp(v - vals[0]) for v in vals]
    denom = exps[0] + exps[1] + exps[2] + exps[3]
    gates = jnp.zeros(logits.shape, F32)
    for j in range(TOP_K):
        gates = jnp.where(lane == j, exps[j] / denom, gates)
    idx_ref[0] = idx_out
    gates_ref[0] = gates


def xattn_router(qx, x1, mk, mv, wxo, g_ffn, w_router, b_router):
    b, t, _ = x1.shape
    tm = min(ROW_TILE, t)
    n_mem = mk.shape[1]
    wr = jnp.zeros((D_MODEL, ROUTER_LANES), F32).at[:, :N_EXPERTS].set(w_router)
    br = jnp.full((1, ROUTER_LANES), -jnp.inf, F32).at[0, :N_EXPERTS].set(b_router)
    tile = pl.BlockSpec((1, tm, D_MODEL), lambda i, j: (i, j, 0))
    mem = pl.BlockSpec((1, n_mem, D_MODEL), lambda i, j: (i, 0, 0))
    small = pl.BlockSpec((1, tm, ROUTER_LANES), lambda i, j: (i, j, 0))
    return pl.pallas_call(
        _xattn_kernel,
        out_shape=(jax.ShapeDtypeStruct((b, t, D_MODEL), F32), jax.ShapeDtypeStruct((b, t, D_MODEL), F32),
                   jax.ShapeDtypeStruct((b, t, ROUTER_LANES), jnp.int32),
                   jax.ShapeDtypeStruct((b, t, ROUTER_LANES), F32)),
        grid=(b, t // tm),
        in_specs=[tile, tile, mem, mem, _resident((D_MODEL, D_MODEL)), _resident((1, D_MODEL)),
                  _resident((D_MODEL, ROUTER_LANES)), _resident((1, ROUTER_LANES))],
        out_specs=(tile, tile, small, small),
        compiler_params=_cparams("parallel", "parallel"),
        name="xattn_router",
    )(qx, x1, mk, mv, wxo, g_ffn.reshape(1, D_MODEL), wr, br)


GATHER_UNROLL = 8


def _start_row_gather(idx_ref, src_hbm, dst, sem):
    def issue(r, carry):
        pltpu.make_async_copy(src_hbm.at[pl.ds(idx_ref[0, 0, r], 1)], dst.at[pl.ds(r, 1)], sem).start()
        return carry

    lax.fori_loop(0, dst.shape[0], issue, 0, unroll=GATHER_UNROLL)


def _wait_row_gather(src_hbm, dst, sem):
    pltpu.make_async_copy(src_hbm.at[pl.ds(0, dst.shape[0])], dst, sem).wait()


def _moe_kernel(be_ref, nused_ref, tok_ref, tok_next_ref, hf_hbm, w1_ref, b1_ref, w2_ref, b2_ref, o_ref,
                xbuf, sem):
    i = pl.program_id(0)
    n_used = nused_ref[0]
    slot = i % 2

    @pl.when((i == 0) & (n_used > 0))
    def _():
        _start_row_gather(tok_ref, hf_hbm, xbuf.at[0], sem.at[0])

    @pl.when(i + 1 < n_used)
    def _():
        _start_row_gather(tok_next_ref, hf_hbm, xbuf.at[1 - slot], sem.at[1 - slot])

    @pl.when(i < n_used)
    def _():
        _wait_row_gather(hf_hbm, xbuf.at[slot], sem.at[slot])
        hb = _dot(xbuf[slot], w1_ref[0]) + b1_ref[0]
        glu = jnp.minimum(hb[:, :D_FF], SWIGLU_LIMIT)
        lin = jnp.clip(hb[:, D_FF:], -SWIGLU_LIMIT, SWIGLU_LIMIT)
        act = glu * jax.nn.sigmoid(SWIGLU_ALPHA * glu) * (lin + 1.0)
        o_ref[...] = _dot(act, w2_ref[0]) + b2_ref[0]

    @pl.when(i >= n_used)
    def _():
        o_ref[...] = jnp.zeros_like(o_ref)


def moe_experts(hf, row_tok, block_expert, n_used, w1, b1, w2, b2):
    n_blocks = block_expert.shape[0]
    tok_spec = lambda step: pl.BlockSpec(
        (1, 1, MOE_BLOCK), lambda i, be, nu: (jnp.minimum(i + step, n_blocks - 1), 0, 0),
        memory_space=pltpu.SMEM)
    grid_spec = pltpu.PrefetchScalarGridSpec(
        num_scalar_prefetch=2,
        grid=(n_blocks,),
        in_specs=[
            tok_spec(0), tok_spec(1),
            pl.BlockSpec(memory_space=pl.ANY),
            pl.BlockSpec((1, D_MODEL, 2 * D_FF), lambda i, be, nu: (be[i], 0, 0)),
            pl.BlockSpec((1, 1, 2 * D_FF), lambda i, be, nu: (be[i], 0, 0)),
            pl.BlockSpec((1, D_FF, D_MODEL), lambda i, be, nu: (be[i], 0, 0)),
            pl.BlockSpec((1, 1, D_MODEL), lambda i, be, nu: (be[i], 0, 0)),
        ],
        out_specs=pl.BlockSpec((MOE_BLOCK, D_MODEL), lambda i, be, nu: (i, 0)),
        scratch_shapes=[pltpu.VMEM((2, MOE_BLOCK, D_MODEL), F32), pltpu.SemaphoreType.DMA((2,))],
    )
    tok = row_tok.reshape(n_blocks, 1, MOE_BLOCK)
    return pl.pallas_call(
        _moe_kernel,
        out_shape=jax.ShapeDtypeStruct((n_blocks * MOE_BLOCK, D_MODEL), F32),
        grid_spec=grid_spec,
        compiler_params=_cparams_gather(),
        name="moe_experts",
    )(block_expert, n_used, tok, tok, hf, w1,
      b1.reshape(N_EXPERTS, 1, 2 * D_FF), w2, b2.reshape(N_EXPERTS, 1, D_MODEL))


COMBINE_TILE = 128


def _combine_kernel(dest_ref, dest_next_ref, os_hbm, x2_ref, gates_ref, gfin_ref, y_ref, buf, sem):
    tm = x2_ref.shape[0]
    i = pl.program_id(0)
    slot = i % 2

    @pl.when(i == 0)
    def _():
        _start_row_gather(dest_ref, os_hbm, buf.at[0], sem.at[0])

    @pl.when(i + 1 < pl.num_programs(0))
    def _():
        _start_row_gather(dest_next_ref, os_hbm, buf.at[1 - slot], sem.at[1 - slot])

    _wait_row_gather(os_hbm, buf.at[slot], sem.at[slot])
    y = x2_ref[...]
    gates = gates_ref[...]
    for j in range(TOP_K):
        y = y + gates[:, j:j + 1] * buf[slot, j * tm:(j + 1) * tm, :]
    y_ref[...] = _rms(y, gfin_ref[...])


def moe_combine(os_rows, dest, x2, gates, g_final):
    n = x2.shape[0]
    tm = min(COMBINE_TILE, n)
    n_tiles = n // tm
    dest_t = dest.reshape(n_tiles, tm, TOP_K).transpose(0, 2, 1).reshape(n_tiles, 1, TOP_K * tm)
    dest_spec = lambda step: pl.BlockSpec(
        (1, 1, TOP_K * tm), lambda i: (jnp.minimum(i + step, n_tiles - 1), 0, 0), memory_space=pltpu.SMEM)
    return pl.pallas_call(
        _combine_kernel,
        out_shape=jax.ShapeDtypeStruct((n, D_MODEL), F32),
        grid=(n_tiles,),
        in_specs=[dest_spec(0), dest_spec(1),
                  pl.BlockSpec(memory_space=pl.ANY),
                  pl.BlockSpec((tm, D_MODEL), lambda i: (i, 0)),
                  pl.BlockSpec((tm, ROUTER_LANES), lambda i: (i, 0)),
                  _resident((1, D_MODEL))],
        out_specs=pl.BlockSpec((tm, D_MODEL), lambda i: (i, 0)),
        scratch_shapes=[pltpu.VMEM((2, TOP_K * tm, D_MODEL), F32), pltpu.SemaphoreType.DMA((2,))],
        compiler_params=_cparams_gather(),
        name="moe_combine",
    )(dest_t, dest_t, os_rows, x2, gates, g_final.reshape(1, D_MODEL))


def _routing(top_idx, n_blocks):
    n = top_idx.shape[0]
    m = n * TOP_K
    e_flat = top_idx.reshape(-1)
    onehot = (e_flat[:, None] == jnp.arange(N_EXPERTS, dtype=jnp.int32)[None, :]).astype(jnp.int32)
    csum = jnp.cumsum(onehot, axis=0)
    counts = csum[-1]
    rank = jnp.take_along_axis(csum, e_flat[:, None], axis=1)[:, 0] - 1
    padded = (counts + MOE_BLOCK - 1) // MOE_BLOCK * MOE_BLOCK
    pends = jnp.cumsum(padded)
    dest = (pends - padded)[e_flat] + rank
    row_tok = jnp.zeros((n_blocks * MOE_BLOCK,), jnp.int32).at[dest].set(
        jnp.arange(m, dtype=jnp.int32) // TOP_K)
    block_expert = jnp.minimum(
        jnp.searchsorted(pends, jnp.arange(n_blocks, dtype=jnp.int32) * MOE_BLOCK, side='right'),
        N_EXPERTS - 1).astype(jnp.int32)
    n_used = (pends[-1] // MOE_BLOCK).astype(jnp.int32).reshape(1)
    return row_tok, block_expert, n_used, dest.reshape(n, TOP_K)


def _layer(x, prev0, s0_t, k_past, v_past, mk, mv, w, chunk, prep_tile, tq):
    b, t, _ = x.shape
    n = b * t
    xf = x.reshape(n, D_MODEL)
    q, k_sb, v_sb, rw, gate = in_proj(xf, w["g_mix"], w["w_in"])
    shift = norm_rows(x[:, -1], w["g_mix"])
    prep = rw_prep(rw.reshape(b, t, D_RW_COLS), prev0, w["rw"], chunk, prep_tile)
    o_rw, s_t = rw_chunks(prep, s0_t, w["gn_w"], w["gn_b"], chunk)
    seq = lambda a: a.reshape(b, t, D_MIX)
    o_sb = stick_breaking(seq(q), seq(k_sb), seq(v_sb), k_past, v_past, tq)
    x1, qx = merge_out(o_rw.reshape(n, D_MIX), o_sb.reshape(n, D_MIX), gate, xf,
                       w["wb0"], w["wb1"], w["w_out"], w["g_xattn"], w["w_xq"])
    x2, hf, idx, gates = xattn_router(qx.reshape(b, t, D_MODEL), x1.reshape(b, t, D_MODEL), mk, mv,
                                      w["w_xo"], w["g_ffn"], w["w_router"], w["b_router"])
    n_blocks = -(-(n * TOP_K + N_EXPERTS * (MOE_BLOCK - 1)) // MOE_BLOCK)
    row_tok, block_expert, n_used, dest = _routing(idx.reshape(n, ROUTER_LANES)[:, :TOP_K], n_blocks)
    os_rows = moe_experts(hf.reshape(n, D_MODEL), row_tok, block_expert, n_used,
                          w["w_e1"], w["b_e1"], w["w_e2"], w["b_e2"])
    y = moe_combine(os_rows, dest, x2.reshape(n, D_MODEL), gates.reshape(n, ROUTER_LANES), w["g_final"])
    return y.reshape(b, t, D_MODEL), shift, s_t, k_sb, v_sb


def _state_to_t(s):
    b = s.shape[0]
    return s.transpose(0, 3, 1, 2).reshape(b, HEAD_DIM, D_MIX)


def _state_from_t(s_t):
    b = s_t.shape[0]
    return s_t.reshape(b, HEAD_DIM, N_HEADS, HEAD_DIM).transpose(0, 2, 3, 1)


def kernel(x_prompt, x_sample, state_rw_shift, state_rw_wkv, cache_sb_k, cache_sb_v, cache_mem_k, cache_mem_v, mem_prompt, g_mix, w_in, rw_mu, rw_w0, rw_w_up, rw_a0, rw_a_up, rw_g_up, rw_k_k, rw_k_a, rw_r_k, rw_gn_w, rw_gn_b, w_branch, w_out, g_xattn, g_mem, w_xq, w_mk, w_mv, w_xo, g_ffn, w_router, b_router, w_e1, b_e1, w_e2, b_e2, g_final):
    assert g_mix.shape[0] == 1, "single-layer trunk"
    row = lambda a: a.reshape(1, -1)
    w = dict(
        g_mix=g_mix[0], w_in=w_in[0].astype(BF16),
        rw=dict(mu=row(rw_mu[0]), w0=row(rw_w0[0]), a0=row(rw_a0[0]), k_k=row(rw_k_k[0]), k_a=row(rw_k_a[0]),
                r_k=row(rw_r_k[0]), w_up=rw_w_up[0].astype(BF16), a_up=rw_a_up[0].astype(BF16),
                g_up=rw_g_up[0].astype(BF16)),
        gn_w=row(rw_gn_w[0]), gn_b=row(rw_gn_b[0]),
        wb0=w_branch[0, 0].astype(BF16), wb1=w_branch[0, 1].astype(BF16), w_out=w_out[0].astype(BF16),
        g_xattn=g_xattn[0], w_xq=w_xq[0].astype(BF16), w_xo=w_xo[0].astype(BF16),
        g_ffn=g_ffn[0], w_router=w_router[0], b_router=b_router[0],
        w_e1=w_e1[0].astype(BF16), b_e1=b_e1[0], w_e2=w_e2[0].astype(BF16), b_e2=b_e2[0],
        g_final=g_final,
    )
    bp, t, _ = x_prompt.shape
    bs, ts, _ = x_sample.shape
    n_mem = mem_prompt.shape[1]

    mk_p, mv_p = mem_kv(mem_prompt.reshape(bp * n_mem, D_MODEL), g_mem[0],
                        w_mk[0].astype(BF16), w_mv[0].astype(BF16))
    mk_p = mk_p.reshape(bp, n_mem, D_MODEL)
    mv_p = mv_p.reshape(bp, n_mem, D_MODEL)
    y_p, sh_p, st_p, k_p, v_p = _layer(
        x_prompt, jnp.zeros((bp, 1, D_RW_COLS), F32), jnp.zeros((bp, HEAD_DIM, D_MIX), F32),
        None, None, mk_p, mv_p, w, chunk=64, prep_tile=256, tq=256)

    prev_s = plain_proj(state_rw_shift[0], w["w_in"][:, SB_COLS:SB_COLS + D_RW_COLS])
    past = cache_sb_k.shape[2]
    y_s, sh_s, st_s, k_s, v_s = _layer(
        x_sample, prev_s.reshape(bs, 1, D_RW_COLS), _state_to_t(state_rw_wkv[0]),
        cache_sb_k[0].reshape(bs, past, D_MIX), cache_sb_v[0].reshape(bs, past, D_MIX),
        cache_mem_k[0].reshape(bs, n_mem, D_MODEL), cache_mem_v[0].reshape(bs, n_mem, D_MODEL),
        w, chunk=ts, prep_tile=ts, tq=ts)

    heads = lambda a, b_, t_: a.reshape(1, b_, t_, N_HEADS, HEAD_DIM)
    xh = lambda a: a.reshape(1, bp, n_mem, N_X_HEADS, X_HEAD_DIM)
    return (y_p, y_s,
            sh_p[None], _state_from_t(st_p)[None], heads(k_p, bp, t), heads(v_p, bp, t), xh(mk_p), xh(mv_p),
            sh_s[None], _state_from_t(st_s)[None], heads(k_s, bs, ts), heads(v_s, bs, ts))
```

```python
import functools

import jax
import jax.numpy as jnp
from jax import lax
from jax.experimental import pallas as pl
from jax.experimental.pallas import tpu as pltpu

F32 = jnp.float32
BF16 = jnp.bfloat16

D_MODEL = 1024
HEAD_DIM = 64
N_HEADS = 8
D_MIX = N_HEADS * HEAD_DIM
LORA_W, LORA_A, LORA_G = 64, 64, 128
D_RW_COLS = 3 * D_MIX + LORA_W + LORA_A + LORA_G
N_X_HEADS = 4
X_HEAD_DIM = 256
N_EXPERTS = 32
TOP_K = 4
D_FF = 1024
SWIGLU_LIMIT = 7.0
SWIGLU_ALPHA = 1.702
MOE_BLOCK = 256
RMS_EPS = 1e-5
GN_EPS = 64e-5

ROW_TILE = 256
RW_GROUP = 4
RW_LANES = RW_GROUP * HEAD_DIM
RW_SEQS = 2
VMEM_LIMIT = 56 * 1024 * 1024


def _cparams(*sem):
    return pltpu.CompilerParams(dimension_semantics=sem, vmem_limit_bytes=VMEM_LIMIT)


def _cparams_gather():
    return pltpu.CompilerParams(dimension_semantics=("arbitrary",), vmem_limit_bytes=VMEM_LIMIT,
                                disable_bounds_checks=True)


def _resident(shape):
    nd = len(shape)
    return pl.BlockSpec(shape, lambda *_: (0,) * nd)


_NN = ((1,), (0,))
_NT = ((1,), (1,))
_TN = ((0,), (0,))


def _dot(a, b, dims=_NN):
    return lax.dot_general(a.astype(BF16), b.astype(BF16), (dims, ((), ())),
                           preferred_element_type=F32)


def _split(x, n):
    parts, rem = [], x
    for i in range(n):
        p = rem.astype(BF16)
        parts.append(p)
        if i + 1 < n:
            rem = rem - p.astype(F32)
    return parts


def _dot_exact_rhs(a, b_bf16, dims=_NN, n=3):
    out = None
    for p in _split(a, n):
        t = lax.dot_general(p, b_bf16, (dims, ((), ())), preferred_element_type=F32)
        out = t if out is None else out + t
    return out


def _dot_exact_lhs(a_bf16, b, dims=_NN, n=3):
    out = None
    for p in _split(b, n):
        t = lax.dot_general(a_bf16, p, (dims, ((), ())), preferred_element_type=F32)
        out = t if out is None else out + t
    return out


def _dot3(a, b, dims=_NN):
    ah, al = _split(a, 2)
    bh, bl = _split(b, 2)
    dn = (dims, ((), ()))
    return (lax.dot_general(ah, bh, dn, preferred_element_type=F32)
            + (lax.dot_general(ah, bl, dn, preferred_element_type=F32)
               + lax.dot_general(al, bh, dn, preferred_element_type=F32)))


def _rms(x, g):
    return x * lax.rsqrt(jnp.mean(x * x, axis=-1, keepdims=True) + RMS_EPS) * g


def _iota(shape, dim):
    return lax.broadcasted_iota(jnp.int32, shape, dim)


def _block_ones(n, width):
    return (_iota((n, n), 0) // width == _iota((n, n), 1) // width).astype(BF16)


def _bd_rows(x, nblk, width):
    blk = _iota(x.shape, 1) // width
    zero = jnp.zeros_like(x)
    return jnp.concatenate([jnp.where(blk == h, x, zero) for h in range(nblk)], axis=0)


def _norm_rows_kernel(x_ref, g_ref, o_ref):
    o_ref[...] = _rms(x_ref[...], g_ref[...])


def norm_rows(x, g):
    r, d = x.shape
    return pl.pallas_call(
        _norm_rows_kernel,
        out_shape=jax.ShapeDtypeStruct((r, d), F32),
        grid=(1,),
        in_specs=[_resident((r, d)), _resident((1, d))],
        out_specs=_resident((r, d)),
        name="norm_rows",
    )(x, g.reshape(1, d))


SB_COLS = 3 * D_MIX
GATE_COLS = 2 * D_MODEL
D_IN = SB_COLS + D_RW_COLS + GATE_COLS


def _in_proj_kernel(x_ref, g_ref, w_ref, q_ref, k_ref, v_ref, rw_ref, gate_ref):
    h = _rms(x_ref[...], g_ref[...]).astype(BF16)

    def proj(lo, hi):
        return jnp.dot(h, w_ref[:, lo:hi], preferred_element_type=F32)

    q_ref[...] = proj(0, D_MIX).astype(BF16)
    k_ref[...] = proj(D_MIX, 2 * D_MIX)
    v_ref[...] = proj(2 * D_MIX, SB_COLS)
    rw_ref[...] = proj(SB_COLS, SB_COLS + D_RW_COLS)
    gate_ref[...] = proj(SB_COLS + D_RW_COLS, D_IN).astype(BF16)


def in_proj(x, g, w_bf16):
    n = x.shape[0]
    tm = min(ROW_TILE, n)
    row = lambda width: pl.BlockSpec((tm, width), lambda i: (i, 0))
    return pl.pallas_call(
        _in_proj_kernel,
        out_shape=(jax.ShapeDtypeStruct((n, D_MIX), BF16),
                   jax.ShapeDtypeStruct((n, D_MIX), F32),
                   jax.ShapeDtypeStruct((n, D_MIX), F32),
                   jax.ShapeDtypeStruct((n, D_RW_COLS), F32),
                   jax.ShapeDtypeStruct((n, GATE_COLS), BF16)),
        grid=(n // tm,),
        in_specs=[row(D_MODEL), _resident((1, D_MODEL)), _resident((D_MODEL, D_IN))],
        out_specs=(row(D_MIX), row(D_MIX), row(D_MIX), row(D_RW_COLS), row(GATE_COLS)),
        compiler_params=_cparams("parallel"),
        name="in_proj",
    )(x, g.reshape(1, D_MODEL), w_bf16)


def _plain_proj_kernel(x_ref, w_ref, o_ref):
    o_ref[...] = _dot(x_ref[...], w_ref[...])


def plain_proj(x, w_bf16):
    r, m = x.shape[0], w_bf16.shape[1]
    return pl.pallas_call(
        _plain_proj_kernel,
        out_shape=jax.ShapeDtypeStruct((r, m), F32),
        grid=(1,),
        in_specs=[_resident(x.shape), _resident(w_bf16.shape)],
        out_specs=_resident((r, m)),
        name="plain_proj",
    )(x, w_bf16)


def _rw_prep_kernel(chunk, rw_ref, prev0_ref, mu_ref, w0_ref, a0_ref, kk_ref, ka_ref, rk_ref,
                    wup_ref, aup_ref, gup_ref,
                    rt_ref, at_ref, bt_ref, kt_ref, bh_ref, kh_ref, v_ref, g_ref, bonus_ref, wc_ref,
                    carry_ref):
    tt = rw_ref.shape[1]

    @pl.when(pl.program_id(1) == 0)
    def _():
        carry_ref[...] = prev0_ref[0]

    p = rw_ref[0]
    prev = jnp.where(_iota(p.shape, 0) == 0, carry_ref[...], pltpu.roll(p, 1, 0))
    carry_ref[...] = p[tt - 1:tt]
    xs = p + mu_ref[...] * (prev - p)
    r = xs[:, 0:D_MIX]
    k = xs[:, D_MIX:2 * D_MIX]
    v = xs[:, 2 * D_MIX:3 * D_MIX]
    lo = 3 * D_MIX
    wd = xs[:, lo:lo + LORA_W]
    ad = xs[:, lo + LORA_W:lo + LORA_W + LORA_A]
    gd = xs[:, lo + LORA_W + LORA_A:D_RW_COLS]

    pre_w = w0_ref[...] + _dot(jnp.tanh(wd), wup_ref[...])
    w_log = -jax.nn.softplus(-pre_w) - 0.5
    logw = -jnp.exp(w_log)
    a = jax.nn.sigmoid(a0_ref[...] + _dot(ad, aup_ref[...]))
    g_ref[0] = _dot(jax.nn.sigmoid(gd), gup_ref[...])

    head_ones = _block_ones(D_MIX, HEAD_DIM)
    kk = k * kk_ref[...]
    ss = _dot_exact_rhs(kk * kk, head_ones)
    kk = kk / jnp.maximum(jnp.sqrt(ss), 1e-12)
    k2 = k * (1.0 + (a - 1.0) * ka_ref[...])
    nb = -(kk * a)
    bonus_ref[0] = _dot_exact_rhs(r * k2 * rk_ref[...], head_ones) * v
    v_ref[0] = v.astype(BF16)

    ti, tj = _iota((tt, tt), 0), _iota((tt, tt), 1)
    same = ti // chunk == tj // chunk
    cum = _dot_exact_lhs((same & (tj <= ti)).astype(BF16), logw)
    tot = _dot_exact_lhs(same.astype(BF16), logw)
    e_in = jnp.exp(cum)
    e_out = jnp.exp(-cum)
    e_end = jnp.exp(tot - cum)
    rt_ref[0] = (r * e_in).astype(BF16)
    at_ref[0] = (kk * jnp.exp(cum - logw)).astype(BF16)
    bt_ref[0] = (nb * e_out).astype(BF16)
    kt_ref[0] = (k2 * e_out).astype(BF16)
    bh_ref[0] = (nb * e_end).astype(BF16)
    kh_ref[0] = (k2 * e_end).astype(BF16)
    etot = jnp.exp(tot)
    for c in range(tt // chunk):
        wc_ref[0, c] = etot[c * chunk:c * chunk + 1]


def rw_prep(rw, prev0, p, chunk, tt):
    b, t, _ = rw.shape
    seq = lambda width, dt: jax.ShapeDtypeStruct((b, t, width), dt)
    tile = lambda width: pl.BlockSpec((1, tt, width), lambda i, j: (i, j, 0))
    vec = lambda width: _resident((1, width))
    n_c = tt // chunk
    return pl.pallas_call(
        functools.partial(_rw_prep_kernel, chunk),
        out_shape=tuple(seq(D_MIX, BF16) for _ in range(7))
        + (seq(D_MIX, F32), seq(D_MIX, F32), jax.ShapeDtypeStruct((b, t // chunk, 1, D_MIX), F32)),
        grid=(b, t // tt),
        in_specs=[tile(D_RW_COLS), pl.BlockSpec((1, 1, D_RW_COLS), lambda i, j: (i, 0, 0)),
                  vec(D_RW_COLS), vec(D_MIX), vec(D_MIX), vec(D_MIX), vec(D_MIX), vec(D_MIX),
                  _resident((LORA_W, D_MIX)), _resident((LORA_A, D_MIX)), _resident((LORA_G, D_MIX))],
        out_specs=tuple(tile(D_MIX) for _ in range(9))
        + (pl.BlockSpec((1, n_c, 1, D_MIX), lambda i, j: (i, j, 0, 0)),),
        scratch_shapes=[pltpu.VMEM((1, D_RW_COLS), F32)],
        compiler_params=_cparams("parallel", "arbitrary"),
        name="rw_prep",
    )(rw, prev0, p["mu"], p["w0"], p["a0"], p["k_k"], p["k_a"], p["r_k"],
      p["w_up"], p["a_up"], p["g_up"])


def _unit_lower_inverse(a, chunk):
    shape = a.shape
    eye = (_iota(shape, 1) % chunk == _iota(shape, 0)).astype(F32)
    res = eye + a
    power = a.astype(BF16)
    for _ in range(chunk.bit_length() - 2):
        power = _dot(power, _bd_rows(power, RW_GROUP, chunk)).astype(BF16)
        res = res + _dot(res, _bd_rows(power, RW_GROUP, chunk))
    return res


def _rw_chunk_kernel(chunk, rt_ref, at_ref, bt_ref, kt_ref, bh_ref, kh_ref, v_ref, g_ref, bonus_ref,
                     wc_ref, s0_ref, gnw_ref, gnb_ref, o_ref, s_out_ref, st_ref):
    c = pl.program_id(1)

    @pl.when(c == 0)
    def _():
        st_ref[...] = s0_ref[...]

    cat = (chunk, RW_GROUP * chunk)
    col_t = _iota(cat, 1) % chunk
    strict = col_t < _iota(cat, 0)
    incl = col_t <= _iota(cat, 0)
    head_ones = _block_ones(RW_LANES, HEAD_DIM)
    bd = lambda x: _bd_rows(x, RW_GROUP, HEAD_DIM)
    n_cat = RW_GROUP * chunk
    for bi, grp in [(bi, grp) for bi in range(st_ref.shape[0]) for grp in range(N_HEADS // RW_GROUP)]:
        sl = slice(grp * RW_LANES, (grp + 1) * RW_LANES)
        rt, at, bt, kt = rt_ref[bi, :, sl], at_ref[bi, :, sl], bt_ref[bi, :, sl], kt_ref[bi, :, sl]
        bh, kh, v = bh_ref[bi, :, sl], kh_ref[bi, :, sl], v_ref[bi, :, sl]
        st = st_ref[bi, :, sl]

        gram = _dot(jnp.concatenate([at, rt], axis=0),
                    jnp.concatenate([bd(bt), bd(kt)], axis=0), _NT)
        zero = jnp.zeros(cat, F32)
        a_ab = jnp.where(strict, gram[:chunk, :n_cat], zero)
        a_ak = jnp.where(strict, gram[:chunk, n_cat:], zero)
        a_r = jnp.concatenate([jnp.where(incl, gram[chunk:, :n_cat], zero),
                               jnp.where(incl, gram[chunk:, n_cat:], zero)], axis=1).astype(BF16)

        t_inv = _unit_lower_inverse(a_ab, chunk)
        v_bd = bd(v)
        x = _dot(a_ak, v_bd).astype(BF16)
        ua = _dot(t_inv, jnp.concatenate([bd(x), bd(at)], axis=1))
        st_bd = bd(st.astype(BF16))
        from_state = _dot(jnp.concatenate([ua[:, RW_LANES:].astype(BF16), rt], axis=0), st_bd)
        u = (ua[:, :RW_LANES] + from_state[:chunk]).astype(BF16)
        y = from_state[chunk:] + _dot(a_r, jnp.concatenate([bd(u), v_bd], axis=0))

        m = _dot(jnp.concatenate([bh, kh], axis=0), jnp.concatenate([u, v], axis=0), _TN)
        lane_head = _iota((HEAD_DIM, RW_LANES), 1) // HEAD_DIM
        fold = jnp.zeros((HEAD_DIM, RW_LANES), F32)
        for h in range(RW_GROUP):
            fold = fold + jnp.where(lane_head == h, m[h * HEAD_DIM:(h + 1) * HEAD_DIM], 0.0)
        diag = _iota((HEAD_DIM, RW_LANES), 1) % HEAD_DIM == _iota((HEAD_DIM, RW_LANES), 0)
        w_rows = jnp.where(diag, wc_ref[bi, 0, :, sl], 0.0)
        w_t = _dot_exact_rhs(w_rows, head_ones)
        st_ref[bi, :, sl] = st * w_t + fold

        mean = _dot_exact_rhs(y, head_ones, n=2) * (1.0 / HEAD_DIM)
        d = y - mean
        var = _dot_exact_rhs(d * d, head_ones, n=2) * (1.0 / HEAD_DIM)
        yn = d * lax.rsqrt(var + GN_EPS) * gnw_ref[:, sl] + gnb_ref[:, sl]
        o_ref[bi, :, sl] = ((yn + bonus_ref[bi, :, sl]) * g_ref[bi, :, sl]).astype(BF16)

    @pl.when(c == pl.num_programs(1) - 1)
    def _():
        s_out_ref[...] = st_ref[...]


def rw_chunks(prep, s0_t, gn_w, gn_b, chunk):
    rt, at, bt, kt, bh, kh, v, g, bonus, wc = prep
    b, t, _ = rt.shape
    nb = RW_SEQS if b % RW_SEQS == 0 else 1
    tile = pl.BlockSpec((nb, chunk, D_MIX), lambda i, j: (i, j, 0))
    state = pl.BlockSpec((nb, HEAD_DIM, D_MIX), lambda i, j: (i, 0, 0))
    return pl.pallas_call(
        functools.partial(_rw_chunk_kernel, chunk),
        out_shape=(jax.ShapeDtypeStruct((b, t, D_MIX), BF16),
                   jax.ShapeDtypeStruct((b, HEAD_DIM, D_MIX), F32)),
        grid=(b // nb, t // chunk),
        in_specs=[tile] * 9 + [pl.BlockSpec((nb, 1, 1, D_MIX), lambda i, j: (i, j, 0, 0)), state,
                               _resident((1, D_MIX)), _resident((1, D_MIX))],
        out_specs=(tile, state),
        scratch_shapes=[pltpu.VMEM((nb, HEAD_DIM, D_MIX), F32)],
        compiler_params=_cparams("parallel", "arbitrary"),
        name="rw_chunks",
    )(rt, at, bt, kt, bh, kh, v, g, bonus, wc, s0_t, gn_w, gn_b)


SB_PAIR = 2 * HEAD_DIM
SB_PAST_BLOCK = 256


def _sb_kernel(tq, n_past, q_ref, kn_ref, vn_ref, *rest):
    if n_past:
        kp_ref, vp_ref, o_ref, acc_ref, c_ref = rest
    else:
        o_ref, acc_ref, c_ref = rest
    qi = pl.program_id(2)
    q2 = q_ref[0] * (HEAD_DIM ** -0.5)
    lane = _iota(q2.shape, 1)
    zero_q = jnp.zeros_like(q2)
    q_st = jnp.concatenate([jnp.where(lane < HEAD_DIM, q2, zero_q), jnp.where(lane >= HEAD_DIM, q2, zero_q)],
                           axis=0)
    acc_ref[...] = jnp.zeros_like(acc_ref)
    c_ref[...] = jnp.zeros_like(c_ref)

    def visit(k_blk, v_blk, masked):
        tk = k_blk.shape[0]
        later = (_iota((tk, tk), 0) > _iota((tk, tk), 1)).astype(BF16)
        z = _dot(q_st, k_blk, _NT)
        sp = jnp.maximum(z, 0.0) + jnp.log(1.0 + jnp.exp(-jnp.abs(z)))
        if masked:
            before = _iota((2 * tq, tk), 1) < _iota((2 * tq, tk), 0) % tq
            drop = jnp.where(before, sp, 0.0)
        else:
            drop = sp
        srev = _dot(drop, later)
        c = c_ref[...]
        att = jnp.exp((z - sp) - (c + srev))
        if masked:
            att = jnp.where(before, att, 0.0)
        acc_ref[...] += _dot(att, v_blk)
        c_ref[...] = c + (srev[:, 0:1] + drop[:, 0:1])

    start = pl.multiple_of(qi * tq, tq)
    visit(kn_ref[0, pl.ds(start, tq), :], vn_ref[0, pl.ds(start, tq), :], True)

    def earlier_new(i, carry):
        s = pl.multiple_of((qi - 1 - i) * tq, tq)
        visit(kn_ref[0, pl.ds(s, tq), :], vn_ref[0, pl.ds(s, tq), :], False)
        return carry

    lax.fori_loop(0, qi, earlier_new, 0)

    if n_past:
        def past(i, carry):
            s = pl.multiple_of((n_past - 1 - i) * SB_PAST_BLOCK, SB_PAST_BLOCK)
            visit(kp_ref[0, pl.ds(s, SB_PAST_BLOCK), :], vp_ref[0, pl.ds(s, SB_PAST_BLOCK), :], False)
            return carry

        lax.fori_loop(0, n_past, past, 0)

    o_ref[0] = jnp.where(lane < HEAD_DIM, acc_ref[:tq], acc_ref[tq:]).astype(BF16)


def stick_breaking(q, k_new, v_new, k_past, v_past, tq):
    b, t, _ = q.shape
    n_pairs = D_MIX // SB_PAIR
    qspec = pl.BlockSpec((1, tq, SB_PAIR), lambda i, p, j: (i, j, p))
    seq = lambda length: pl.BlockSpec((1, length, SB_PAIR), lambda i, p, j: (i, 0, p))
    args, specs, n_past = [q, k_new, v_new], [qspec, seq(t), seq(t)], 0
    if k_past is not None:
        past_len = k_past.shape[1]
        n_past = past_len // SB_PAST_BLOCK
        args += [k_past, v_past]
        specs += [seq(past_len), seq(past_len)]
    return pl.pallas_call(
        functools.partial(_sb_kernel, tq, n_past),
        out_shape=jax.ShapeDtypeStruct((b, t, D_MIX), BF16),
        grid=(b, n_pairs, t // tq),
        in_specs=specs,
        out_specs=qspec,
        scratch_shapes=[pltpu.VMEM((2 * tq, SB_PAIR), F32), pltpu.VMEM((2 * tq, 1), F32)],
        compiler_params=_cparams("parallel", "parallel", "arbitrary"),
        name="stick_breaking",
    )(*args)


def _merge_kernel(orw_ref, osb_ref, gate_ref, x_ref, wb0_ref, wb1_ref, wout_ref, gx_ref, wxq_ref,
                  x1_ref, qx_ref):
    g = jax.nn.sigmoid(gate_ref[...].astype(F32))
    mixed = (g[:, :D_MODEL] * jnp.dot(orw_ref[...], wb0_ref[...], preferred_element_type=F32)
             + g[:, D_MODEL:] * jnp.dot(osb_ref[...], wb1_ref[...], preferred_element_type=F32))
    x1 = x_ref[...] + _dot(mixed, wout_ref[...])
    x1_ref[...] = x1
    qx_ref[...] = _dot(_rms(x1, gx_ref[...]), wxq_ref[...]).astype(BF16)


def merge_out(o_rw, o_sb, gate, x, wb0, wb1, wout, gx, wxq):
    n = x.shape[0]
    tm = min(ROW_TILE, n)
    row = lambda width: pl.BlockSpec((tm, width), lambda i: (i, 0))
    return pl.pallas_call(
        _merge_kernel,
        out_shape=(jax.ShapeDtypeStruct((n, D_MODEL), F32), jax.ShapeDtypeStruct((n, D_MODEL), BF16)),
        grid=(n // tm,),
        in_specs=[row(D_MIX), row(D_MIX), row(GATE_COLS), row(D_MODEL),
                  _resident((D_MIX, D_MODEL)), _resident((D_MIX, D_MODEL)), _resident((D_MODEL, D_MODEL)),
                  _resident((1, D_MODEL)), _resident((D_MODEL, D_MODEL))],
        out_specs=(row(D_MODEL), row(D_MODEL)),
        compiler_params=_cparams("parallel"),
        name="merge_out",
    )(o_rw, o_sb, gate, x, wb0, wb1, wout, gx.reshape(1, D_MODEL), wxq)


def _mem_kv_kernel(m_ref, g_ref, wk_ref, wv_ref, k_ref, v_ref):
    h = _rms(m_ref[...], g_ref[...]).astype(BF16)
    k_ref[...] = jnp.dot(h, wk_ref[...], preferred_element_type=F32)
    v_ref[...] = jnp.dot(h, wv_ref[...], preferred_element_type=F32)


def mem_kv(mem, g, wk, wv):
    n = mem.shape[0]
    tm = min(ROW_TILE, n)
    row = pl.BlockSpec((tm, D_MODEL), lambda i: (i, 0))
    return pl.pallas_call(
        _mem_kv_kernel,
        out_shape=(jax.ShapeDtypeStruct((n, D_MODEL), F32),) * 2,
        grid=(n // tm,),
        in_specs=[row, _resident((1, D_MODEL)), _resident((D_MODEL, D_MODEL)), _resident((D_MODEL, D_MODEL))],
        out_specs=(row, row),
        compiler_params=_cparams("parallel"),
        name="mem_kv",
    )(mem, g.reshape(1, D_MODEL), wk, wv)


ROUTER_LANES = 128


def _xattn_kernel(q_ref, x1_ref, mk_ref, mv_ref, wxo_ref, gf_ref, wr_ref, br_ref,
                  x2_ref, hf_ref, idx_ref, gates_ref):
    q = q_ref[0]
    heads = []
    for h in range(N_X_HEADS):
        sl = slice(h * X_HEAD_DIM, (h + 1) * X_HEAD_DIM)
        s = _dot(q[:, sl], mk_ref[0, :, sl], _NT) * (X_HEAD_DIM ** -0.5)
        e = jnp.exp(s - jnp.max(s, axis=-1, keepdims=True))
        p = e / jnp.sum(e, axis=-1, keepdims=True)
        heads.append(_dot(p, mv_ref[0, :, sl]))
    x2 = x1_ref[0] + _dot(jnp.concatenate(heads, axis=1), wxo_ref[...])
    x2_ref[0] = x2
    hf = _rms(x2, gf_ref[...])
    hf_ref[0] = hf
    logits = _dot3(hf, wr_ref[...]) + br_ref[...]
    lane = _iota(logits.shape, 1)
    vals, idx_out = [], jnp.zeros(logits.shape, jnp.int32)
    for j in range(TOP_K):
        m = jnp.max(logits, axis=-1, keepdims=True)
        pick = jnp.min(jnp.where(logits == m, lane, ROUTER_LANES), axis=-1, keepdims=True)
        vals.append(m)
        idx_out = jnp.where(lane == j, pick, idx_out)
        logits = jnp.where(lane == pick, -jnp.inf, logits)
    exps = [jnp.exp(v - vals[0]) for v in vals]
    denom = exps[0] + exps[1] + exps[2] + exps[3]
    gates = jnp.zeros(logits.shape, F32)
    for j in range(TOP_K):
        gates = jnp.where(lane == j, exps[j] / denom, gates)
    idx_ref[0] = idx_out
    gates_ref[0] = gates


def xattn_router(qx, x1, mk, mv, wxo, g_ffn, w_router, b_router):
    b, t, _ = x1.shape
    tm = min(ROW_TILE, t)
    n_mem = mk.shape[1]
    wr = jnp.zeros((D_MODEL, ROUTER_LANES), F32).at[:, :N_EXPERTS].set(w_router)
    br = jnp.full((1, ROUTER_LANES), -jnp.inf, F32).at[0, :N_EXPERTS].set(b_router)
    tile = pl.BlockSpec((1, tm, D_MODEL), lambda i, j: (i, j, 0))
    mem = pl.BlockSpec((1, n_mem, D_MODEL), lambda i, j: (i, 0, 0))
    small = pl.BlockSpec((1, tm, ROUTER_LANES), lambda i, j: (i, j, 0))
    return pl.pallas_call(
        _xattn_kernel,
        out_shape=(jax.ShapeDtypeStruct((b, t, D_MODEL), F32), jax.ShapeDtypeStruct((b, t, D_MODEL), F32),
                   jax.ShapeDtypeStruct((b, t, ROUTER_LANES), jnp.int32),
                   jax.ShapeDtypeStruct((b, t, ROUTER_LANES), F32)),
        grid=(b, t // tm),
        in_specs=[tile, tile, mem, mem, _resident((D_MODEL, D_MODEL)), _resident((1, D_MODEL)),
                  _resident((D_MODEL, ROUTER_LANES)), _resident((1, ROUTER_LANES))],
        out_specs=(tile, tile, small, small),
        compiler_params=_cparams("parallel", "parallel"),
        name="xattn_router",
    )(qx, x1, mk, mv, wxo, g_ffn.reshape(1, D_MODEL), wr, br)


GATHER_UNROLL = 8
ROUTE_SUBLANES = 8


def _route_kernel(idx_ref, dest_ref, counts_ref, carry_ref, start_ref):
    phase, t = pl.program_id(0), pl.program_id(1)
    tm = idx_ref.shape[0]
    idx = idx_ref[...]
    lane = _iota((tm, ROUTER_LANES), 1)
    picks = [idx[:, j:j + 1] for j in range(TOP_K)]
    onehot = jnp.zeros((tm, ROUTER_LANES), F32)
    for pick in picks:
        onehot = onehot + (lane == pick).astype(F32)
    col_sum = jnp.broadcast_to(jnp.sum(onehot, axis=0, keepdims=True), carry_ref.shape)

    @pl.when(t == 0)
    def _():
        carry_ref[...] = jnp.zeros_like(carry_ref)

    @pl.when(phase == 0)
    def _():
        carry_ref[...] += col_sum

        @pl.when(t == pl.num_programs(1) - 1)
        def _():
            total = carry_ref[...]
            counts_ref[...] = total
            padded = jnp.floor((total + (MOE_BLOCK - 1.0)) * (1.0 / MOE_BLOCK)) * MOE_BLOCK
            earlier = (_iota((ROUTER_LANES, ROUTER_LANES), 0) < _iota((ROUTER_LANES, ROUTER_LANES), 1))
            start_ref[...] = _dot_exact_rhs(padded, earlier.astype(BF16))

    @pl.when(phase == 1)
    def _():
        below = (_iota((tm, tm), 1) < _iota((tm, tm), 0)).astype(BF16)
        row = _dot(below, onehot) + (carry_ref[0:1, :] + start_ref[0:1, :])
        dest = jnp.zeros((tm, ROUTER_LANES), F32)
        for j, pick in enumerate(picks):
            mine = jnp.sum(jnp.where(lane == pick, row, 0.0), axis=1, keepdims=True)
            dest = jnp.where(lane == j, mine, dest)
        dest_ref[0] = dest.T[:ROUTE_SUBLANES].astype(jnp.int32)
        carry_ref[...] += col_sum


def moe_route(idx, tm):
    n = idx.shape[0]
    n_tiles = n // tm
    return pl.pallas_call(
        _route_kernel,
        out_shape=(jax.ShapeDtypeStruct((n_tiles, ROUTE_SUBLANES, tm), jnp.int32),
                   jax.ShapeDtypeStruct((ROUTE_SUBLANES, ROUTER_LANES), F32)),
        grid=(2, n_tiles),
        in_specs=[pl.BlockSpec((tm, ROUTER_LANES), lambda p, t: (t, 0))],
        out_specs=(pl.BlockSpec((1, ROUTE_SUBLANES, tm), lambda p, t: (t * p, 0, 0)),
                   _resident((ROUTE_SUBLANES, ROUTER_LANES))),
        scratch_shapes=[pltpu.VMEM((ROUTE_SUBLANES, ROUTER_LANES), F32),
                        pltpu.VMEM((ROUTE_SUBLANES, ROUTER_LANES), F32)],
        compiler_params=_cparams("arbitrary", "arbitrary"),
        name="moe_route",
    )(idx)


def _block_tables(counts, n_blocks):
    cnt = counts[0, :N_EXPERTS].astype(jnp.int32)
    padded = (cnt + MOE_BLOCK - 1) // MOE_BLOCK * MOE_BLOCK
    pends = jnp.cumsum(padded).astype(jnp.int32)
    block_expert = jnp.minimum(
        jnp.searchsorted(pends, jnp.arange(n_blocks, dtype=jnp.int32) * MOE_BLOCK, side='right'),
        N_EXPERTS - 1).astype(jnp.int32)
    n_used = (pends[-1:] // MOE_BLOCK).astype(jnp.int32)
    return pends, block_expert, n_used


def _dispatch_kernel(bounds, pends_ref, dest_ref, *rest):
    hf_refs = rest[:len(bounds) - 1]
    xs_hbm, zbuf, sem = rest[len(bounds) - 1:]
    i = pl.program_id(0)

    @pl.when(i == 0)
    def _():
        zbuf[...] = jnp.zeros_like(zbuf)
        n_blocks = xs_hbm.shape[0] // MOE_BLOCK
        n_used = pends_ref[N_EXPERTS - 1] // MOE_BLOCK
        for wait in (False, True):
            def unused_block(blk, carry):
                cp = pltpu.make_async_copy(
                    zbuf, xs_hbm.at[pl.ds(pl.multiple_of(blk * MOE_BLOCK, MOE_BLOCK), MOE_BLOCK)], sem)
                cp.wait() if wait else cp.start()
                return carry

            lax.fori_loop(n_used, n_blocks, unused_block, 0)
            for e in range(N_EXPERTS):
                end = pends_ref[e]
                begin = pends_ref[e - 1] if e else 0

                @pl.when(end > begin)
                def _():
                    last = pl.multiple_of(end - MOE_BLOCK, MOE_BLOCK)
                    cp = pltpu.make_async_copy(zbuf, xs_hbm.at[pl.ds(last, MOE_BLOCK)], sem)
                    cp.wait() if wait else cp.start()

    for hf_ref, lo, hi in zip(hf_refs, bounds[:-1], bounds[1:]):
        @pl.when((i >= lo) & (i < hi))
        def _():
            tm = hf_ref.shape[0]
            for j in range(TOP_K):
                def issue(r, carry):
                    pltpu.make_async_copy(hf_ref.at[pl.ds(r, 1)], xs_hbm.at[pl.ds(dest_ref[0, j, r], 1)],
                                          sem).start()
                    return carry

                lax.fori_loop(0, tm, issue, 0, unroll=GATHER_UNROLL)
            for j in range(TOP_K):
                pltpu.make_async_copy(hf_ref, xs_hbm.at[pl.ds(0, tm)], sem).wait()


def moe_dispatch(hfs, bounds, dest, pends, n_rows):
    n_tiles, _, tm = dest.shape
    hf_spec = lambda lo, hi: pl.BlockSpec((tm, D_MODEL), lambda i, pe: (jnp.clip(i - lo, 0, hi - lo - 1), 0))
    grid_spec = pltpu.PrefetchScalarGridSpec(
        num_scalar_prefetch=1,
        grid=(n_tiles,),
        in_specs=[pl.BlockSpec((1, ROUTE_SUBLANES, tm), lambda i, pe: (i, 0, 0), memory_space=pltpu.SMEM)]
        + [hf_spec(lo, hi) for lo, hi in zip(bounds[:-1], bounds[1:])],
        out_specs=pl.BlockSpec(memory_space=pl.ANY),
        scratch_shapes=[pltpu.VMEM((MOE_BLOCK, D_MODEL), F32), pltpu.SemaphoreType.DMA(())],
    )
    return pl.pallas_call(
        functools.partial(_dispatch_kernel, tuple(bounds)),
        out_shape=jax.ShapeDtypeStruct((n_rows, D_MODEL), F32),
        grid_spec=grid_spec,
        compiler_params=_cparams_gather(),
        name="moe_dispatch",
    )(pends, dest, *hfs)


W1_CHUNKS = 4
W2_CHUNKS = 2


def _moe_kernel(be_ref, nused_ref, x_ref, *rest):
    w1 = rest[:W1_CHUNKS]
    b1_ref = rest[W1_CHUNKS]
    w2 = rest[W1_CHUNKS + 1:W1_CHUNKS + 1 + W2_CHUNKS]
    b2_ref, o_ref = rest[W1_CHUNKS + 1 + W2_CHUNKS:]
    i = pl.program_id(0)
    half = W1_CHUNKS // 2
    wide = 2 * D_FF // W1_CHUNKS

    @pl.when(i < nused_ref[0])
    def _():
        x = x_ref[...].astype(BF16)
        out = b2_ref[0]
        for c in range(half):
            glu = jnp.dot(x, w1[c][0], preferred_element_type=F32) + b1_ref[0, :, c * wide:(c + 1) * wide]
            lin = (jnp.dot(x, w1[half + c][0], preferred_element_type=F32)
                   + b1_ref[0, :, D_FF + c * wide:D_FF + (c + 1) * wide])
            glu = jnp.minimum(glu, SWIGLU_LIMIT)
            lin = jnp.clip(lin, -SWIGLU_LIMIT, SWIGLU_LIMIT)
            act = glu * jax.nn.sigmoid(SWIGLU_ALPHA * glu) * (lin + 1.0)
            out = out + _dot(act, w2[c][0])
        o_ref[...] = out

    @pl.when(i >= nused_ref[0])
    def _():
        o_ref[...] = jnp.zeros_like(o_ref)


def moe_experts(xs, block_expert, n_used, w1, b1, w2, b2):
    n_blocks = block_expert.shape[0]
    assert W1_CHUNKS // 2 == W2_CHUNKS
    c1 = 2 * D_FF // W1_CHUNKS
    c2 = D_FF // W2_CHUNKS
    w1_spec = lambda c: pl.BlockSpec((1, D_MODEL, c1), lambda i, be, nu: (be[i], 0, c))
    w2_spec = lambda c: pl.BlockSpec((1, c2, D_MODEL), lambda i, be, nu: (be[i], c, 0))
    grid_spec = pltpu.PrefetchScalarGridSpec(
        num_scalar_prefetch=2,
        grid=(n_blocks,),
        in_specs=[pl.BlockSpec((MOE_BLOCK, D_MODEL), lambda i, be, nu: (jnp.minimum(i, nu[0] - 1), 0))]
        + [w1_spec(c) for c in range(W1_CHUNKS)]
        + [pl.BlockSpec((1, 1, 2 * D_FF), lambda i, be, nu: (be[i], 0, 0))]
        + [w2_spec(c) for c in range(W2_CHUNKS)]
        + [pl.BlockSpec((1, 1, D_MODEL), lambda i, be, nu: (be[i], 0, 0))],
        out_specs=pl.BlockSpec((MOE_BLOCK, D_MODEL), lambda i, be, nu: (i, 0)),
    )
    return pl.pallas_call(
        _moe_kernel,
        out_shape=jax.ShapeDtypeStruct((n_blocks * MOE_BLOCK, D_MODEL), F32),
        grid_spec=grid_spec,
        compiler_params=_cparams("arbitrary"),
        name="moe_experts",
    )(block_expert, n_used, xs, *([w1] * W1_CHUNKS), b1.reshape(N_EXPERTS, 1, 2 * D_FF),
      *([w2] * W2_CHUNKS), b2.reshape(N_EXPERTS, 1, D_MODEL))


def _start_combine_gather(dest_ref, os_hbm, dst, sem):
    tm = dst.shape[0] // TOP_K
    for j in range(TOP_K):
        def issue(r, carry):
            pltpu.make_async_copy(os_hbm.at[pl.ds(dest_ref[0, j, r], 1)], dst.at[pl.ds(j * tm + r, 1)],
                                  sem).start()
            return carry

        lax.fori_loop(0, tm, issue, 0, unroll=GATHER_UNROLL)


def _combine_kernel(n_tiles, dest_ref, dest_next_ref, os_hbm, x2_ref, gates_ref, gfin_ref, y_ref, buf, sem):
    tm = x2_ref.shape[0]
    i = pl.program_id(0)
    slot = i % 2

    @pl.when(i == 0)
    def _():
        _start_combine_gather(dest_ref, os_hbm, buf.at[0], sem.at[0])

    if n_tiles > 1:
        @pl.when(i + 1 < n_tiles)
        def _():
            _start_combine_gather(dest_next_ref, os_hbm, buf.at[1 - slot], sem.at[1 - slot])

    pltpu.make_async_copy(os_hbm.at[pl.ds(0, TOP_K * tm)], buf.at[slot], sem.at[slot]).wait()
    y = x2_ref[...]
    gates = gates_ref[...]
    for j in range(TOP_K):
        y = y + gates[:, j:j + 1] * buf[slot, j * tm:(j + 1) * tm, :]
    y_ref[...] = _rms(y, gfin_ref[...])


def moe_combine(os_rows, dest, x2, gates, g_final):
    n = x2.shape[0]
    n_tiles, _, tm = dest.shape
    dest_spec = lambda step: pl.BlockSpec(
        (1, ROUTE_SUBLANES, tm), lambda i: (jnp.minimum(i + step, n_tiles - 1), 0, 0), memory_space=pltpu.SMEM)
    return pl.pallas_call(
        functools.partial(_combine_kernel, n_tiles),
        out_shape=jax.ShapeDtypeStruct((n, D_MODEL), F32),
        grid=(n_tiles,),
        in_specs=[dest_spec(0), dest_spec(1),
                  pl.BlockSpec(memory_space=pl.ANY),
                  pl.BlockSpec((tm, D_MODEL), lambda i: (i, 0)),
                  pl.BlockSpec((tm, ROUTER_LANES), lambda i: (i, 0)),
                  _resident((1, D_MODEL))],
        out_specs=pl.BlockSpec((tm, D_MODEL), lambda i: (i, 0)),
        scratch_shapes=[pltpu.VMEM((2, TOP_K * tm, D_MODEL), F32), pltpu.SemaphoreType.DMA((2,))],
        compiler_params=_cparams_gather(),
        name="moe_combine",
    )(dest, dest, os_rows, x2, gates, g_final.reshape(1, D_MODEL))


def moe_ffn(groups, w):
    sizes = [g[0].shape[0] for g in groups]
    n = sum(sizes)
    tm = next(c for c in (256, 128, 64, 32, 16, 8) if all(s % c == 0 for s in sizes))
    dest, counts = moe_route(jnp.concatenate([g[2] for g in groups], axis=0), tm)
    n_blocks = -(-(n * TOP_K + N_EXPERTS * (MOE_BLOCK - 1)) // MOE_BLOCK)
    pends, block_expert, n_used = _block_tables(counts, n_blocks)
    bounds = [0]
    for s in sizes:
        bounds.append(bounds[-1] + s // tm)
    xs = moe_dispatch([g[1] for g in groups], bounds, dest, pends, n_blocks * MOE_BLOCK)
    os_rows = moe_experts(xs, block_expert, n_used, w["w_e1"], w["b_e1"], w["w_e2"], w["b_e2"])
    return [moe_combine(os_rows, dest[lo:hi], g[0], g[3], w["g_final"])
            for g, lo, hi in zip(groups, bounds[:-1], bounds[1:])]


def _mixers(x, prev0, s0_t, k_past, v_past, mk, mv, w, chunk, prep_tile, tq):
    b, t, _ = x.shape
    n = b * t
    xf = x.reshape(n, D_MODEL)
    q, k_sb, v_sb, rw, gate = in_proj(xf, w["g_mix"], w["w_in"])
    shift = norm_rows(x[:, -1], w["g_mix"])
    prep = rw_prep(rw.reshape(b, t, D_RW_COLS), prev0, w["rw"], chunk, prep_tile)
    o_rw, s_t = rw_chunks(prep, s0_t, w["gn_w"], w["gn_b"], chunk)
    seq = lambda a: a.reshape(b, t, D_MIX)
    o_sb = stick_breaking(seq(q), seq(k_sb), seq(v_sb), k_past, v_past, tq)
    x1, qx = merge_out(o_rw.reshape(n, D_MIX), o_sb.reshape(n, D_MIX), gate, xf,
                       w["wb0"], w["wb1"], w["w_out"], w["g_xattn"], w["w_xq"])
    x2, hf, idx, gates = xattn_router(qx.reshape(b, t, D_MODEL), x1.reshape(b, t, D_MODEL), mk, mv,
                                      w["w_xo"], w["g_ffn"], w["w_router"], w["b_router"])
    tok = (x2.reshape(n, D_MODEL), hf.reshape(n, D_MODEL), idx.reshape(n, ROUTER_LANES),
           gates.reshape(n, ROUTER_LANES))
    return tok, shift, s_t, k_sb, v_sb


def _state_to_t(s):
    b = s.shape[0]
    return s.transpose(0, 3, 1, 2).reshape(b, HEAD_DIM, D_MIX)


def _state_from_t(s_t):
    b = s_t.shape[0]
    return s_t.reshape(b, HEAD_DIM, N_HEADS, HEAD_DIM).transpose(0, 2, 3, 1)


def kernel(x_prompt, x_sample, state_rw_shift, state_rw_wkv, cache_sb_k, cache_sb_v, cache_mem_k, cache_mem_v, mem_prompt, g_mix, w_in, rw_mu, rw_w0, rw_w_up, rw_a0, rw_a_up, rw_g_up, rw_k_k, rw_k_a, rw_r_k, rw_gn_w, rw_gn_b, w_branch, w_out, g_xattn, g_mem, w_xq, w_mk, w_mv, w_xo, g_ffn, w_router, b_router, w_e1, b_e1, w_e2, b_e2, g_final):
    assert g_mix.shape[0] == 1, "single-layer trunk"
    row = lambda a: a.reshape(1, -1)
    w = dict(
        g_mix=g_mix[0], w_in=w_in[0].astype(BF16),
        rw=dict(mu=row(rw_mu[0]), w0=row(rw_w0[0]), a0=row(rw_a0[0]), k_k=row(rw_k_k[0]), k_a=row(rw_k_a[0]),
                r_k=row(rw_r_k[0]), w_up=rw_w_up[0].astype(BF16), a_up=rw_a_up[0].astype(BF16),
                g_up=rw_g_up[0].astype(BF16)),
        gn_w=row(rw_gn_w[0]), gn_b=row(rw_gn_b[0]),
        wb0=w_branch[0, 0].astype(BF16), wb1=w_branch[0, 1].astype(BF16), w_out=w_out[0].astype(BF16),
        g_xattn=g_xattn[0], w_xq=w_xq[0].astype(BF16), w_xo=w_xo[0].astype(BF16),
        g_ffn=g_ffn[0], w_router=w_router[0], b_router=b_router[0],
        w_e1=w_e1[0].astype(BF16), b_e1=b_e1[0], w_e2=w_e2[0].astype(BF16), b_e2=b_e2[0],
        g_final=g_final,
    )
    bp, t, _ = x_prompt.shape
    bs, ts, _ = x_sample.shape
    n_mem = mem_prompt.shape[1]

    mk_p, mv_p = mem_kv(mem_prompt.reshape(bp * n_mem, D_MODEL), g_mem[0],
                        w_mk[0].astype(BF16), w_mv[0].astype(BF16))
    mk_p = mk_p.reshape(bp, n_mem, D_MODEL)
    mv_p = mv_p.reshape(bp, n_mem, D_MODEL)
    tok_p, sh_p, st_p, k_p, v_p = _mixers(
        x_prompt, jnp.zeros((bp, 1, D_RW_COLS), F32), jnp.zeros((bp, HEAD_DIM, D_MIX), F32),
        None, None, mk_p, mv_p, w, chunk=64, prep_tile=256, tq=256)

    prev_s = plain_proj(state_rw_shift[0], w["w_in"][:, SB_COLS:SB_COLS + D_RW_COLS])
    past = cache_sb_k.shape[2]
    tok_s, sh_s, st_s, k_s, v_s = _mixers(
        x_sample, prev_s.reshape(bs, 1, D_RW_COLS), _state_to_t(state_rw_wkv[0]),
        cache_sb_k[0].reshape(bs, past, D_MIX), cache_sb_v[0].reshape(bs, past, D_MIX),
        cache_mem_k[0].reshape(bs, n_mem, D_MODEL), cache_mem_v[0].reshape(bs, n_mem, D_MODEL),
        w, chunk=ts, prep_tile=ts, tq=ts)

    y_p, y_s = moe_ffn([tok_p, tok_s], w)

    heads = lambda a, b_, t_: a.reshape(1, b_, t_, N_HEADS, HEAD_DIM)
    xh = lambda a: a.reshape(1, bp, n_mem, N_X_HEADS, X_HEAD_DIM)
    return (y_p.reshape(bp, t, D_MODEL), y_s.reshape(bs, ts, D_MODEL),
            sh_p[None], _state_from_t(st_p)[None], heads(k_p, bp, t), heads(v_p, bp, t), xh(mk_p), xh(mv_p),
            sh_s[None], _state_from_t(st_s)[None], heads(k_s, bs, ts), heads(v_s, bs, ts))
```

```python
import functools

import jax
import jax.numpy as jnp
from jax import lax
from jax.experimental import pallas as pl
from jax.experimental.pallas import tpu as pltpu

F32 = jnp.float32
BF16 = jnp.bfloat16

D_MODEL = 1024
HEAD_DIM = 64
N_HEADS = 8
D_MIX = N_HEADS * HEAD_DIM
LORA_W, LORA_A, LORA_G = 64, 64, 128
D_RW_COLS = 3 * D_MIX + LORA_W + LORA_A + LORA_G
N_X_HEADS = 4
X_HEAD_DIM = 256
N_EXPERTS = 32
TOP_K = 4
D_FF = 1024
SWIGLU_LIMIT = 7.0
SWIGLU_ALPHA = 1.702
MOE_BLOCK = 256
RMS_EPS = 1e-5
GN_EPS = 64e-5

ROW_TILE = 256
RW_GROUP = 4
RW_LANES = RW_GROUP * HEAD_DIM
RW_SEQS = 2
VMEM_LIMIT = 56 * 1024 * 1024


def _cparams(*sem):
    return pltpu.CompilerParams(dimension_semantics=sem, vmem_limit_bytes=VMEM_LIMIT)


def _cparams_gather():
    return pltpu.CompilerParams(dimension_semantics=("arbitrary",), vmem_limit_bytes=VMEM_LIMIT,
                                disable_bounds_checks=True)


def _resident(shape):
    nd = len(shape)
    return pl.BlockSpec(shape, lambda *_: (0,) * nd)


_NN = ((1,), (0,))
_NT = ((1,), (1,))
_TN = ((0,), (0,))


def _dot(a, b, dims=_NN):
    return lax.dot_general(a.astype(BF16), b.astype(BF16), (dims, ((), ())),
                           preferred_element_type=F32)


def _split(x, n):
    parts, rem = [], x
    for i in range(n):
        p = rem.astype(BF16)
        parts.append(p)
        if i + 1 < n:
            rem = rem - p.astype(F32)
    return parts


def _dot_exact_rhs(a, b_bf16, dims=_NN, n=3):
    out = None
    for p in _split(a, n):
        t = lax.dot_general(p, b_bf16, (dims, ((), ())), preferred_element_type=F32)
        out = t if out is None else out + t
    return out


def _dot_exact_lhs(a_bf16, b, dims=_NN, n=3):
    out = None
    for p in _split(b, n):
        t = lax.dot_general(a_bf16, p, (dims, ((), ())), preferred_element_type=F32)
        out = t if out is None else out + t
    return out


def _dot3(a, b, dims=_NN):
    ah, al = _split(a, 2)
    bh, bl = _split(b, 2)
    dn = (dims, ((), ()))
    return (lax.dot_general(ah, bh, dn, preferred_element_type=F32)
            + (lax.dot_general(ah, bl, dn, preferred_element_type=F32)
               + lax.dot_general(al, bh, dn, preferred_element_type=F32)))


def _rms(x, g):
    return x * lax.rsqrt(jnp.mean(x * x, axis=-1, keepdims=True) + RMS_EPS) * g


def _iota(shape, dim):
    return lax.broadcasted_iota(jnp.int32, shape, dim)


def _block_ones(n, width):
    return (_iota((n, n), 0) // width == _iota((n, n), 1) // width).astype(BF16)


def _bd_rows(x, nblk, width):
    blk = _iota(x.shape, 1) // width
    zero = jnp.zeros_like(x)
    return jnp.concatenate([jnp.where(blk == h, x, zero) for h in range(nblk)], axis=0)


def _norm_rows_kernel(x_ref, g_ref, o_ref):
    o_ref[...] = _rms(x_ref[...], g_ref[...])


def norm_rows(x, g):
    r, d = x.shape
    return pl.pallas_call(
        _norm_rows_kernel,
        out_shape=jax.ShapeDtypeStruct((r, d), F32),
        grid=(1,),
        in_specs=[_resident((r, d)), _resident((1, d))],
        out_specs=_resident((r, d)),
        name="norm_rows",
    )(x, g.reshape(1, d))


SB_COLS = 3 * D_MIX
GATE_COLS = 2 * D_MODEL
D_IN = SB_COLS + D_RW_COLS + GATE_COLS


def _in_proj_kernel(x_ref, g_ref, w_ref, q_ref, k_ref, v_ref, rw_ref, gate_ref):
    h = _rms(x_ref[...], g_ref[...]).astype(BF16)

    def proj(lo, hi):
        return jnp.dot(h, w_ref[:, lo:hi], preferred_element_type=F32)

    q_ref[...] = proj(0, D_MIX).astype(BF16)
    k_ref[...] = proj(D_MIX, 2 * D_MIX)
    v_ref[...] = proj(2 * D_MIX, SB_COLS)
    rw_ref[...] = proj(SB_COLS, SB_COLS + D_RW_COLS)
    gate_ref[...] = proj(SB_COLS + D_RW_COLS, D_IN).astype(BF16)


def in_proj(x, g, w_bf16):
    n = x.shape[0]
    tm = min(ROW_TILE, n)
    row = lambda width: pl.BlockSpec((tm, width), lambda i: (i, 0))
    return pl.pallas_call(
        _in_proj_kernel,
        out_shape=(jax.ShapeDtypeStruct((n, D_MIX), BF16),
                   jax.ShapeDtypeStruct((n, D_MIX), F32),
                   jax.ShapeDtypeStruct((n, D_MIX), F32),
                   jax.ShapeDtypeStruct((n, D_RW_COLS), F32),
                   jax.ShapeDtypeStruct((n, GATE_COLS), BF16)),
        grid=(n // tm,),
        in_specs=[row(D_MODEL), _resident((1, D_MODEL)), _resident((D_MODEL, D_IN))],
        out_specs=(row(D_MIX), row(D_MIX), row(D_MIX), row(D_RW_COLS), row(GATE_COLS)),
        compiler_params=_cparams("parallel"),
        name="in_proj",
    )(x, g.reshape(1, D_MODEL), w_bf16)


def _plain_proj_kernel(x_ref, w_ref, o_ref):
    o_ref[...] = _dot(x_ref[...], w_ref[...])


def plain_proj(x, w_bf16):
    r, m = x.shape[0], w_bf16.shape[1]
    return pl.pallas_call(
        _plain_proj_kernel,
        out_shape=jax.ShapeDtypeStruct((r, m), F32),
        grid=(1,),
        in_specs=[_resident(x.shape), _resident(w_bf16.shape)],
        out_specs=_resident((r, m)),
        name="plain_proj",
    )(x, w_bf16)


def _rw_prep_kernel(chunk, rw_ref, prev0_ref, mu_ref, w0_ref, a0_ref, kk_ref, ka_ref, rk_ref,
                    wup_ref, aup_ref, gup_ref,
                    rt_ref, at_ref, bt_ref, kt_ref, bh_ref, kh_ref, v_ref, g_ref, bonus_ref, wc_ref,
                    carry_ref):
    tt = rw_ref.shape[1]

    @pl.when(pl.program_id(1) == 0)
    def _():
        carry_ref[...] = prev0_ref[0]

    p = rw_ref[0]
    prev = jnp.where(_iota(p.shape, 0) == 0, carry_ref[...], pltpu.roll(p, 1, 0))
    carry_ref[...] = p[tt - 1:tt]
    xs = p + mu_ref[...] * (prev - p)
    r = xs[:, 0:D_MIX]
    k = xs[:, D_MIX:2 * D_MIX]
    v = xs[:, 2 * D_MIX:3 * D_MIX]
    lo = 3 * D_MIX
    wd = xs[:, lo:lo + LORA_W]
    ad = xs[:, lo + LORA_W:lo + LORA_W + LORA_A]
    gd = xs[:, lo + LORA_W + LORA_A:D_RW_COLS]

    pre_w = w0_ref[...] + _dot(jnp.tanh(wd), wup_ref[...])
    w_log = -jax.nn.softplus(-pre_w) - 0.5
    logw = -jnp.exp(w_log)
    a = jax.nn.sigmoid(a0_ref[...] + _dot(ad, aup_ref[...]))
    g_ref[0] = _dot(jax.nn.sigmoid(gd), gup_ref[...])

    head_ones = _block_ones(D_MIX, HEAD_DIM)
    kk = k * kk_ref[...]
    ss = _dot_exact_rhs(kk * kk, head_ones)
    kk = kk / jnp.maximum(jnp.sqrt(ss), 1e-12)
    k2 = k * (1.0 + (a - 1.0) * ka_ref[...])
    nb = -(kk * a)
    bonus_ref[0] = _dot_exact_rhs(r * k2 * rk_ref[...], head_ones) * v
    v_ref[0] = v.astype(BF16)

    ti, tj = _iota((tt, tt), 0), _iota((tt, tt), 1)
    same = ti // chunk == tj // chunk
    cum = _dot_exact_lhs((same & (tj <= ti)).astype(BF16), logw)
    tot = _dot_exact_lhs(same.astype(BF16), logw)
    e_in = jnp.exp(cum)
    e_out = jnp.exp(-cum)
    e_end = jnp.exp(tot - cum)
    rt_ref[0] = (r * e_in).astype(BF16)
    at_ref[0] = (kk * jnp.exp(cum - logw)).astype(BF16)
    bt_ref[0] = (nb * e_out).astype(BF16)
    kt_ref[0] = (k2 * e_out).astype(BF16)
    bh_ref[0] = (nb * e_end).astype(BF16)
    kh_ref[0] = (k2 * e_end).astype(BF16)
    etot = jnp.exp(tot)
    for c in range(tt // chunk):
        wc_ref[0, c] = etot[c * chunk:c * chunk + 1]


def rw_prep(rw, prev0, p, chunk, tt):
    b, t, _ = rw.shape
    seq = lambda width, dt: jax.ShapeDtypeStruct((b, t, width), dt)
    tile = lambda width: pl.BlockSpec((1, tt, width), lambda i, j: (i, j, 0))
    vec = lambda width: _resident((1, width))
    n_c = tt // chunk
    return pl.pallas_call(
        functools.partial(_rw_prep_kernel, chunk),
        out_shape=tuple(seq(D_MIX, BF16) for _ in range(7))
        + (seq(D_MIX, F32), seq(D_MIX, F32), jax.ShapeDtypeStruct((b, t // chunk, 1, D_MIX), F32)),
        grid=(b, t // tt),
        in_specs=[tile(D_RW_COLS), pl.BlockSpec((1, 1, D_RW_COLS), lambda i, j: (i, 0, 0)),
                  vec(D_RW_COLS), vec(D_MIX), vec(D_MIX), vec(D_MIX), vec(D_MIX), vec(D_MIX),
                  _resident((LORA_W, D_MIX)), _resident((LORA_A, D_MIX)), _resident((LORA_G, D_MIX))],
        out_specs=tuple(tile(D_MIX) for _ in range(9))
        + (pl.BlockSpec((1, n_c, 1, D_MIX), lambda i, j: (i, j, 0, 0)),),
        scratch_shapes=[pltpu.VMEM((1, D_RW_COLS), F32)],
        compiler_params=_cparams("parallel", "arbitrary"),
        name="rw_prep",
    )(rw, prev0, p["mu"], p["w0"], p["a0"], p["k_k"], p["k_a"], p["r_k"],
      p["w_up"], p["a_up"], p["g_up"])


def _unit_lower_inverse(a, chunk):
    shape = a.shape
    eye = (_iota(shape, 1) % chunk == _iota(shape, 0)).astype(F32)
    res = eye + a
    power = a.astype(BF16)
    for _ in range(chunk.bit_length() - 2):
        power = _dot(power, _bd_rows(power, RW_GROUP, chunk)).astype(BF16)
        res = res + _dot(res, _bd_rows(power, RW_GROUP, chunk))
    return res


def _rw_chunk_kernel(chunk, rt_ref, at_ref, bt_ref, kt_ref, bh_ref, kh_ref, v_ref, g_ref, bonus_ref,
                     wc_ref, s0_ref, gnw_ref, gnb_ref, o_ref, s_out_ref, st_ref):
    c = pl.program_id(1)

    @pl.when(c == 0)
    def _():
        st_ref[...] = s0_ref[...]

    cat = (chunk, RW_GROUP * chunk)
    col_t = _iota(cat, 1) % chunk
    strict = col_t < _iota(cat, 0)
    incl = col_t <= _iota(cat, 0)
    head_ones = _block_ones(RW_LANES, HEAD_DIM)
    bd = lambda x: _bd_rows(x, RW_GROUP, HEAD_DIM)
    n_cat = RW_GROUP * chunk
    for bi, grp in [(bi, grp) for bi in range(st_ref.shape[0]) for grp in range(N_HEADS // RW_GROUP)]:
        sl = slice(grp * RW_LANES, (grp + 1) * RW_LANES)
        rt, at, bt, kt = rt_ref[bi, :, sl], at_ref[bi, :, sl], bt_ref[bi, :, sl], kt_ref[bi, :, sl]
        bh, kh, v = bh_ref[bi, :, sl], kh_ref[bi, :, sl], v_ref[bi, :, sl]
        st = st_ref[bi, :, sl]

        gram = _dot(jnp.concatenate([at, rt], axis=0),
                    jnp.concatenate([bd(bt), bd(kt)], axis=0), _NT)
        zero = jnp.zeros(cat, F32)
        a_ab = jnp.where(strict, gram[:chunk, :n_cat], zero)
        a_ak = jnp.where(strict, gram[:chunk, n_cat:], zero)
        a_r = jnp.concatenate([jnp.where(incl, gram[chunk:, :n_cat], zero),
                               jnp.where(incl, gram[chunk:, n_cat:], zero)], axis=1).astype(BF16)

        t_inv = _unit_lower_inverse(a_ab, chunk)
        v_bd = bd(v)
        x = _dot(a_ak, v_bd).astype(BF16)
        ua = _dot(t_inv, jnp.concatenate([bd(x), bd(at)], axis=1))
        st_bd = bd(st.astype(BF16))
        from_state = _dot(jnp.concatenate([ua[:, RW_LANES:].astype(BF16), rt], axis=0), st_bd)
        u = (ua[:, :RW_LANES] + from_state[:chunk]).astype(BF16)
        y = from_state[chunk:] + _dot(a_r, jnp.concatenate([bd(u), v_bd], axis=0))

        m = _dot(jnp.concatenate([bh, kh], axis=0), jnp.concatenate([u, v], axis=0), _TN)
        lane_head = _iota((HEAD_DIM, RW_LANES), 1) // HEAD_DIM
        fold = jnp.zeros((HEAD_DIM, RW_LANES), F32)
        for h in range(RW_GROUP):
            fold = fold + jnp.where(lane_head == h, m[h * HEAD_DIM:(h + 1) * HEAD_DIM], 0.0)
        diag = _iota((HEAD_DIM, RW_LANES), 1) % HEAD_DIM == _iota((HEAD_DIM, RW_LANES), 0)
        w_rows = jnp.where(diag, wc_ref[bi, 0, :, sl], 0.0)
        w_t = _dot_exact_rhs(w_rows, head_ones)
        st_ref[bi, :, sl] = st * w_t + fold

        mean = _dot_exact_rhs(y, head_ones, n=2) * (1.0 / HEAD_DIM)
        d = y - mean
        var = _dot_exact_rhs(d * d, head_ones, n=2) * (1.0 / HEAD_DIM)
        yn = d * lax.rsqrt(var + GN_EPS) * gnw_ref[:, sl] + gnb_ref[:, sl]
        o_ref[bi, :, sl] = ((yn + bonus_ref[bi, :, sl]) * g_ref[bi, :, sl]).astype(BF16)

    @pl.when(c == pl.num_programs(1) - 1)
    def _():
        s_out_ref[...] = st_ref[...]


def rw_chunks(prep, s0_t, gn_w, gn_b, chunk):
    rt, at, bt, kt, bh, kh, v, g, bonus, wc = prep
    b, t, _ = rt.shape
    nb = RW_SEQS if b % RW_SEQS == 0 else 1
    tile = pl.BlockSpec((nb, chunk, D_MIX), lambda i, j: (i, j, 0))
    state = pl.BlockSpec((nb, HEAD_DIM, D_MIX), lambda i, j: (i, 0, 0))
    return pl.pallas_call(
        functools.partial(_rw_chunk_kernel, chunk),
        out_shape=(jax.ShapeDtypeStruct((b, t, D_MIX), BF16),
                   jax.ShapeDtypeStruct((b, HEAD_DIM, D_MIX), F32)),
        grid=(b // nb, t // chunk),
        in_specs=[tile] * 9 + [pl.BlockSpec((nb, 1, 1, D_MIX), lambda i, j: (i, j, 0, 0)), state,
                               _resident((1, D_MIX)), _resident((1, D_MIX))],
        out_specs=(tile, state),
        scratch_shapes=[pltpu.VMEM((nb, HEAD_DIM, D_MIX), F32)],
        compiler_params=_cparams("parallel", "arbitrary"),
        name="rw_chunks",
    )(rt, at, bt, kt, bh, kh, v, g, bonus, wc, s0_t, gn_w, gn_b)


SB_PAIR = 2 * HEAD_DIM
SB_PAST_BLOCK = 256
SB_VISITS = 4


def _sb_kernel(tq, n_past, q_ref, kn_ref, vn_ref, *rest):
    if n_past:
        kp_ref, vp_ref, o_ref, acc_ref, c_ref = rest
    else:
        o_ref, acc_ref, c_ref = rest
    qi = pl.program_id(2)
    q2 = q_ref[0] * (HEAD_DIM ** -0.5)
    lane = _iota(q2.shape, 1)
    zero_q = jnp.zeros_like(q2)
    q_st = jnp.concatenate([jnp.where(lane < HEAD_DIM, q2, zero_q), jnp.where(lane >= HEAD_DIM, q2, zero_q)],
                           axis=0)
    acc_ref[...] = jnp.zeros_like(acc_ref)
    c_ref[...] = jnp.zeros_like(c_ref)

    def visit(k_blk, v_blk, masked):
        tk = k_blk.shape[0]
        later = (_iota((tk, tk), 0) > _iota((tk, tk), 1)).astype(BF16)
        z = _dot(q_st, k_blk, _NT)
        sp = jnp.maximum(z, 0.0) + jnp.log(1.0 + jnp.exp(-jnp.abs(z)))
        if masked:
            before = _iota((2 * tq, tk), 1) < _iota((2 * tq, tk), 0) % tq
            drop = jnp.where(before, sp, 0.0)
        else:
            drop = sp
        srev = _dot(drop, later)
        c = c_ref[...]
        att = jnp.exp((z - sp) - (c + srev))
        if masked:
            att = jnp.where(before, att, 0.0)
        acc_ref[...] += _dot(att, v_blk)
        c_ref[...] = c + (srev[:, 0:1] + drop[:, 0:1])

    start = pl.multiple_of(qi * tq, tq)
    visit(kn_ref[0, pl.ds(start, tq), :], vn_ref[0, pl.ds(start, tq), :], True)

    def earlier_new(blk):
        s = pl.multiple_of(blk * tq, tq)
        visit(kn_ref[0, pl.ds(s, tq), :], vn_ref[0, pl.ds(s, tq), :], False)

    def earlier_group(i, carry):
        for u in range(SB_VISITS):
            earlier_new(qi - 1 - u - SB_VISITS * i)
        return carry

    lax.fori_loop(0, qi // SB_VISITS, earlier_group, 0)
    rest = qi % SB_VISITS

    @pl.when(rest >= 2)
    def _():
        earlier_new(rest - 1)
        earlier_new(rest - 2)

    @pl.when(rest % 2 == 1)
    def _():
        earlier_new(0)

    if n_past:
        def past(blk):
            s = pl.multiple_of(blk * SB_PAST_BLOCK, SB_PAST_BLOCK)
            visit(kp_ref[0, pl.ds(s, SB_PAST_BLOCK), :], vp_ref[0, pl.ds(s, SB_PAST_BLOCK), :], False)

        def past_group(i, carry):
            for u in range(SB_VISITS):
                past(n_past - 1 - u - SB_VISITS * i)
            return carry

        lax.fori_loop(0, n_past // SB_VISITS, past_group, 0)
        for blk in reversed(range(n_past % SB_VISITS)):
            past(blk)

    o_ref[0] = jnp.where(lane < HEAD_DIM, acc_ref[:tq], acc_ref[tq:]).astype(BF16)


def stick_breaking(q, k_new, v_new, k_past, v_past, tq):
    b, t, _ = q.shape
    n_pairs = D_MIX // SB_PAIR
    qspec = pl.BlockSpec((1, tq, SB_PAIR), lambda i, p, j: (i, j, p))
    seq = lambda length: pl.BlockSpec((1, length, SB_PAIR), lambda i, p, j: (i, 0, p))
    args, specs, n_past = [q, k_new, v_new], [qspec, seq(t), seq(t)], 0
    if k_past is not None:
        past_len = k_past.shape[1]
        n_past = past_len // SB_PAST_BLOCK
        args += [k_past, v_past]
        specs += [seq(past_len), seq(past_len)]
    return pl.pallas_call(
        functools.partial(_sb_kernel, tq, n_past),
        out_shape=jax.ShapeDtypeStruct((b, t, D_MIX), BF16),
        grid=(b, n_pairs, t // tq),
        in_specs=specs,
        out_specs=qspec,
        scratch_shapes=[pltpu.VMEM((2 * tq, SB_PAIR), F32), pltpu.VMEM((2 * tq, 1), F32)],
        compiler_params=_cparams("parallel", "parallel", "arbitrary"),
        name="stick_breaking",
    )(*args)


def _merge_kernel(orw_ref, osb_ref, gate_ref, x_ref, wb0_ref, wb1_ref, wout_ref, gx_ref, wxq_ref,
                  x1_ref, qx_ref):
    g = jax.nn.sigmoid(gate_ref[...].astype(F32))
    mixed = (g[:, :D_MODEL] * jnp.dot(orw_ref[...], wb0_ref[...], preferred_element_type=F32)
             + g[:, D_MODEL:] * jnp.dot(osb_ref[...], wb1_ref[...], preferred_element_type=F32))
    x1 = x_ref[...] + _dot(mixed, wout_ref[...])
    x1_ref[...] = x1
    qx_ref[...] = _dot(_rms(x1, gx_ref[...]), wxq_ref[...]).astype(BF16)


def merge_out(o_rw, o_sb, gate, x, wb0, wb1, wout, gx, wxq):
    n = x.shape[0]
    tm = min(ROW_TILE, n)
    row = lambda width: pl.BlockSpec((tm, width), lambda i: (i, 0))
    return pl.pallas_call(
        _merge_kernel,
        out_shape=(jax.ShapeDtypeStruct((n, D_MODEL), F32), jax.ShapeDtypeStruct((n, D_MODEL), BF16)),
        grid=(n // tm,),
        in_specs=[row(D_MIX), row(D_MIX), row(GATE_COLS), row(D_MODEL),
                  _resident((D_MIX, D_MODEL)), _resident((D_MIX, D_MODEL)), _resident((D_MODEL, D_MODEL)),
                  _resident((1, D_MODEL)), _resident((D_MODEL, D_MODEL))],
        out_specs=(row(D_MODEL), row(D_MODEL)),
        compiler_params=_cparams("parallel"),
        name="merge_out",
    )(o_rw, o_sb, gate, x, wb0, wb1, wout, gx.reshape(1, D_MODEL), wxq)


def _mem_kv_kernel(m_ref, g_ref, wk_ref, wv_ref, k_ref, v_ref):
    h = _rms(m_ref[...], g_ref[...]).astype(BF16)
    k_ref[...] = jnp.dot(h, wk_ref[...], preferred_element_type=F32)
    v_ref[...] = jnp.dot(h, wv_ref[...], preferred_element_type=F32)


def mem_kv(mem, g, wk, wv):
    n = mem.shape[0]
    tm = min(ROW_TILE, n)
    row = pl.BlockSpec((tm, D_MODEL), lambda i: (i, 0))
    return pl.pallas_call(
        _mem_kv_kernel,
        out_shape=(jax.ShapeDtypeStruct((n, D_MODEL), F32),) * 2,
        grid=(n // tm,),
        in_specs=[row, _resident((1, D_MODEL)), _resident((D_MODEL, D_MODEL)), _resident((D_MODEL, D_MODEL))],
        out_specs=(row, row),
        compiler_params=_cparams("parallel"),
        name="mem_kv",
    )(mem, g.reshape(1, D_MODEL), wk, wv)


ROUTER_LANES = 128


def _xattn_kernel(q_ref, x1_ref, mk_ref, mv_ref, wxo_ref, gf_ref, wr_ref, br_ref,
                  x2_ref, hf_ref, idx_ref, gates_ref):
    q = q_ref[0]
    heads = []
    for h in range(N_X_HEADS):
        sl = slice(h * X_HEAD_DIM, (h + 1) * X_HEAD_DIM)
        s = _dot(q[:, sl], mk_ref[0, :, sl], _NT) * (X_HEAD_DIM ** -0.5)
        e = jnp.exp(s - jnp.max(s, axis=-1, keepdims=True))
        p = e / jnp.sum(e, axis=-1, keepdims=True)
        heads.append(_dot(p, mv_ref[0, :, sl]))
    x2 = x1_ref[0] + _dot(jnp.concatenate(heads, axis=1), wxo_ref[...])
    x2_ref[0] = x2
    hf = _rms(x2, gf_ref[...])
    hf_ref[0] = hf
    logits = _dot3(hf, wr_ref[...]) + br_ref[...]
    lane = _iota(logits.shape, 1)
    vals, idx_out = [], jnp.zeros(logits.shape, jnp.int32)
    for j in range(TOP_K):
        m = jnp.max(logits, axis=-1, keepdims=True)
        pick = jnp.min(jnp.where(logits == m, lane, ROUTER_LANES), axis=-1, keepdims=True)
        vals.append(m)
        idx_out = jnp.where(lane == j, pick, idx_out)
        logits = jnp.where(lane == pick, -jnp.inf, logits)
    exps = [jnp.exp(v - vals[0]) for v in vals]
    denom = exps[0] + exps[1] + exps[2] + exps[3]
    gates = jnp.zeros(logits.shape, F32)
    for j in range(TOP_K):
        gates = jnp.where(lane == j, exps[j] / denom, gates)
    idx_ref[0] = idx_out
    gates_ref[0] = gates


def xattn_router(qx, x1, mk, mv, wxo, g_ffn, w_router, b_router):
    b, t, _ = x1.shape
    tm = min(ROW_TILE, t)
    n_mem = mk.shape[1]
    wr = jnp.zeros((D_MODEL, ROUTER_LANES), F32).at[:, :N_EXPERTS].set(w_router)
    br = jnp.full((1, ROUTER_LANES), -jnp.inf, F32).at[0, :N_EXPERTS].set(b_router)
    tile = pl.BlockSpec((1, tm, D_MODEL), lambda i, j: (i, j, 0))
    mem = pl.BlockSpec((1, n_mem, D_MODEL), lambda i, j: (i, 0, 0))
    small = pl.BlockSpec((1, tm, ROUTER_LANES), lambda i, j: (i, j, 0))
    return pl.pallas_call(
        _xattn_kernel,
        out_shape=(jax.ShapeDtypeStruct((b, t, D_MODEL), F32), jax.ShapeDtypeStruct((b, t, D_MODEL), F32),
                   jax.ShapeDtypeStruct((b, t, ROUTER_LANES), jnp.int32),
                   jax.ShapeDtypeStruct((b, t, ROUTER_LANES), F32)),
        grid=(b, t // tm),
        in_specs=[tile, tile, mem, mem, _resident((D_MODEL, D_MODEL)), _resident((1, D_MODEL)),
                  _resident((D_MODEL, ROUTER_LANES)), _resident((1, ROUTER_LANES))],
        out_specs=(tile, tile, small, small),
        compiler_params=_cparams("parallel", "parallel"),
        name="xattn_router",
    )(qx, x1, mk, mv, wxo, g_ffn.reshape(1, D_MODEL), wr, br)


ROW_GROUP = 8
ROUTE_SUBLANES = 8


def _route_kernel(idx_ref, dest_ref, counts_ref, carry_ref, start_ref):
    phase, t = pl.program_id(0), pl.program_id(1)
    tm = idx_ref.shape[0]
    idx = idx_ref[...]
    lane = _iota((tm, ROUTER_LANES), 1)
    picks = [idx[:, j:j + 1] for j in range(TOP_K)]
    onehot = jnp.zeros((tm, ROUTER_LANES), F32)
    for pick in picks:
        onehot = onehot + (lane == pick).astype(F32)
    col_sum = jnp.broadcast_to(jnp.sum(onehot, axis=0, keepdims=True), carry_ref.shape)

    @pl.when(t == 0)
    def _():
        carry_ref[...] = jnp.zeros_like(carry_ref)

    @pl.when(phase == 0)
    def _():
        carry_ref[...] += col_sum

        @pl.when(t == pl.num_programs(1) - 1)
        def _():
            total = carry_ref[...]
            counts_ref[...] = total
            padded = jnp.floor((total + (MOE_BLOCK - 1.0)) * (1.0 / MOE_BLOCK)) * MOE_BLOCK
            earlier = (_iota((ROUTER_LANES, ROUTER_LANES), 0) < _iota((ROUTER_LANES, ROUTER_LANES), 1))
            start_ref[...] = _dot_exact_rhs(padded, earlier.astype(BF16))

    @pl.when(phase == 1)
    def _():
        below = (_iota((tm, tm), 1) < _iota((tm, tm), 0)).astype(BF16)
        row = _dot(below, onehot) + (carry_ref[0:1, :] + start_ref[0:1, :])
        dest = jnp.zeros((tm, ROUTER_LANES), F32)
        for j, pick in enumerate(picks):
            mine = jnp.sum(jnp.where(lane == pick, row, 0.0), axis=1, keepdims=True)
            dest = jnp.where(lane == j, mine, dest)
        dest_ref[0] = dest.T[:ROUTE_SUBLANES].astype(jnp.int32)
        carry_ref[...] += col_sum


def moe_route(idx, tm):
    n = idx.shape[0]
    n_tiles = n // tm
    return pl.pallas_call(
        _route_kernel,
        out_shape=(jax.ShapeDtypeStruct((n_tiles, ROUTE_SUBLANES, tm), jnp.int32),
                   jax.ShapeDtypeStruct((ROUTE_SUBLANES, ROUTER_LANES), F32)),
        grid=(2, n_tiles),
        in_specs=[pl.BlockSpec((tm, ROUTER_LANES), lambda p, t: (t, 0))],
        out_specs=(pl.BlockSpec((1, ROUTE_SUBLANES, tm), lambda p, t: (t * p, 0, 0)),
                   _resident((ROUTE_SUBLANES, ROUTER_LANES))),
        scratch_shapes=[pltpu.VMEM((ROUTE_SUBLANES, ROUTER_LANES), F32),
                        pltpu.VMEM((ROUTE_SUBLANES, ROUTER_LANES), F32)],
        compiler_params=_cparams("arbitrary", "arbitrary"),
        name="moe_route",
    )(idx)


def _block_tables(counts, n_blocks):
    cnt = counts[0, :N_EXPERTS].astype(jnp.int32)
    padded = (cnt + MOE_BLOCK - 1) // MOE_BLOCK * MOE_BLOCK
    pends = jnp.cumsum(padded).astype(jnp.int32)
    block_start = jnp.arange(n_blocks, dtype=jnp.int32) * MOE_BLOCK
    block_expert = jnp.minimum(jnp.sum((pends[None, :] <= block_start[:, None]).astype(jnp.int32), axis=1),
                               N_EXPERTS - 1)
    n_used = (pends[-1:] // MOE_BLOCK).astype(jnp.int32)
    return pends, block_expert, n_used


def _dispatch_kernel(bounds, pends_ref, dest_ref, *rest):
    hf_refs = rest[:len(bounds) - 1]
    xs_hbm, zbuf, sem = rest[len(bounds) - 1:]
    i = pl.program_id(0)

    @pl.when(i == 0)
    def _():
        zbuf[...] = jnp.zeros_like(zbuf)
        n_blocks = xs_hbm.shape[0] // MOE_BLOCK
        n_used = pends_ref[N_EXPERTS - 1] // MOE_BLOCK
        for wait in (False, True):
            def unused_block(blk, carry):
                cp = pltpu.make_async_copy(
                    zbuf, xs_hbm.at[pl.ds(pl.multiple_of(blk * MOE_BLOCK, MOE_BLOCK), MOE_BLOCK)], sem)
                cp.wait() if wait else cp.start()
                return carry

            lax.fori_loop(n_used, n_blocks, unused_block, 0)
            for e in range(N_EXPERTS):
                end = pends_ref[e]
                begin = pends_ref[e - 1] if e else 0

                @pl.when(end > begin)
                def _():
                    last = pl.multiple_of(end - MOE_BLOCK, MOE_BLOCK)
                    cp = pltpu.make_async_copy(zbuf, xs_hbm.at[pl.ds(last, MOE_BLOCK)], sem)
                    cp.wait() if wait else cp.start()

    for hf_ref, lo, hi in zip(hf_refs, bounds[:-1], bounds[1:]):
        @pl.when((i >= lo) & (i < hi))
        def _():
            tm = hf_ref.shape[0]
            for j in range(TOP_K):
                def issue(g, carry):
                    base = pl.multiple_of(g * ROW_GROUP, ROW_GROUP)
                    rows = hf_ref.at[pl.ds(base, ROW_GROUP)]
                    for u in range(ROW_GROUP):
                        pltpu.make_async_copy(rows.at[pl.ds(u, 1)], xs_hbm.at[pl.ds(dest_ref[j * tm + base + u], 1)],
                                              sem).start()
                    return carry

                lax.fori_loop(0, tm // ROW_GROUP, issue, 0)
            for j in range(TOP_K):
                pltpu.make_async_copy(hf_ref, xs_hbm.at[pl.ds(0, tm)], sem).wait()


def moe_dispatch(hfs, bounds, dest, pends, n_rows):
    n_tiles, _, tm = dest.shape
    hf_spec = lambda lo, hi: pl.BlockSpec((tm, D_MODEL), lambda i, pe: (jnp.clip(i - lo, 0, hi - lo - 1), 0))
    grid_spec = pltpu.PrefetchScalarGridSpec(
        num_scalar_prefetch=1,
        grid=(n_tiles,),
        in_specs=[pl.BlockSpec((ROUTE_SUBLANES * tm,), lambda i, pe: (i,), memory_space=pltpu.SMEM)]
        + [hf_spec(lo, hi) for lo, hi in zip(bounds[:-1], bounds[1:])],
        out_specs=pl.BlockSpec(memory_space=pl.ANY),
        scratch_shapes=[pltpu.VMEM((MOE_BLOCK, D_MODEL), F32), pltpu.SemaphoreType.DMA(())],
    )
    return pl.pallas_call(
        functools.partial(_dispatch_kernel, tuple(bounds)),
        out_shape=jax.ShapeDtypeStruct((n_rows, D_MODEL), F32),
        grid_spec=grid_spec,
        compiler_params=_cparams_gather(),
        name="moe_dispatch",
    )(pends, dest.reshape(-1), *hfs)


W1_CHUNKS = 4
W2_CHUNKS = 2


def _moe_kernel(be_ref, nused_ref, x_ref, *rest):
    w1 = rest[:W1_CHUNKS]
    b1_ref = rest[W1_CHUNKS]
    w2 = rest[W1_CHUNKS + 1:W1_CHUNKS + 1 + W2_CHUNKS]
    b2_ref, o_ref, w1_bf, w2_bf = rest[W1_CHUNKS + 1 + W2_CHUNKS:]
    i = pl.program_id(0)
    half = W1_CHUNKS // 2
    wide = 2 * D_FF // W1_CHUNKS

    @pl.when((i < nused_ref[0]) & ((i == 0) | (be_ref[i] != be_ref[jnp.maximum(i - 1, 0)])))
    def _():
        for c in range(W1_CHUNKS):
            w1_bf[c] = w1[c][0].astype(BF16)
        for c in range(W2_CHUNKS):
            w2_bf[c] = w2[c][0].astype(BF16)

    @pl.when(i < nused_ref[0])
    def _():
        x = x_ref[...].astype(BF16)
        out = b2_ref[0]
        for c in range(half):
            glu = jnp.dot(x, w1_bf[c], preferred_element_type=F32) + b1_ref[0, :, c * wide:(c + 1) * wide]
            lin = (jnp.dot(x, w1_bf[half + c], preferred_element_type=F32)
                   + b1_ref[0, :, D_FF + c * wide:D_FF + (c + 1) * wide])
            glu = jnp.minimum(glu, SWIGLU_LIMIT)
            lin = jnp.clip(lin, -SWIGLU_LIMIT, SWIGLU_LIMIT)
            act = glu * jax.nn.sigmoid(SWIGLU_ALPHA * glu) * (lin + 1.0)
            out = out + _dot(act, w2_bf[c])
        o_ref[...] = out

    @pl.when(i >= nused_ref[0])
    def _():
        o_ref[...] = jnp.zeros_like(o_ref)


def moe_experts(xs, block_expert, n_used, w1, b1, w2, b2):
    n_blocks = block_expert.shape[0]
    assert W1_CHUNKS // 2 == W2_CHUNKS
    c1 = 2 * D_FF // W1_CHUNKS
    c2 = D_FF // W2_CHUNKS
    w1_spec = lambda c: pl.BlockSpec((1, D_MODEL, c1), lambda i, be, nu: (be[i], 0, c))
    w2_spec = lambda c: pl.BlockSpec((1, c2, D_MODEL), lambda i, be, nu: (be[i], c, 0))
    grid_spec = pltpu.PrefetchScalarGridSpec(
        num_scalar_prefetch=2,
        grid=(n_blocks,),
        in_specs=[pl.BlockSpec((MOE_BLOCK, D_MODEL), lambda i, be, nu: (jnp.minimum(i, nu[0] - 1), 0))]
        + [w1_spec(c) for c in range(W1_CHUNKS)]
        + [pl.BlockSpec((1, 1, 2 * D_FF), lambda i, be, nu: (be[i], 0, 0))]
        + [w2_spec(c) for c in range(W2_CHUNKS)]
        + [pl.BlockSpec((1, 1, D_MODEL), lambda i, be, nu: (be[i], 0, 0))],
        out_specs=pl.BlockSpec((MOE_BLOCK, D_MODEL), lambda i, be, nu: (i, 0)),
        scratch_shapes=[pltpu.VMEM((W1_CHUNKS, D_MODEL, c1), BF16), pltpu.VMEM((W2_CHUNKS, c2, D_MODEL), BF16)],
    )
    return pl.pallas_call(
        _moe_kernel,
        out_shape=jax.ShapeDtypeStruct((n_blocks * MOE_BLOCK, D_MODEL), F32),
        grid_spec=grid_spec,
        compiler_params=_cparams("arbitrary"),
        name="moe_experts",
    )(block_expert, n_used, xs, *([w1] * W1_CHUNKS), b1.reshape(N_EXPERTS, 1, 2 * D_FF),
      *([w2] * W2_CHUNKS), b2.reshape(N_EXPERTS, 1, D_MODEL))


def _start_combine_gather(dest_ref, os_hbm, dst, sem):
    tm = dst.shape[0] // TOP_K
    for j in range(TOP_K):
        def issue(g, carry):
            base = pl.multiple_of(g * ROW_GROUP, ROW_GROUP)
            rows = dst.at[pl.ds(j * tm + base, ROW_GROUP)]
            for u in range(ROW_GROUP):
                pltpu.make_async_copy(os_hbm.at[pl.ds(dest_ref[j * tm + base + u], 1)], rows.at[pl.ds(u, 1)],
                                      sem).start()
            return carry

        lax.fori_loop(0, tm // ROW_GROUP, issue, 0)


def _combine_kernel(n_tiles, dest_ref, dest_next_ref, os_hbm, x2_ref, gates_ref, gfin_ref, y_ref, buf, sem):
    tm = x2_ref.shape[0]
    i = pl.program_id(0)
    slot = i % 2

    @pl.when(i == 0)
    def _():
        _start_combine_gather(dest_ref, os_hbm, buf.at[0], sem.at[0])

    if n_tiles > 1:
        @pl.when(i + 1 < n_tiles)
        def _():
            _start_combine_gather(dest_next_ref, os_hbm, buf.at[1 - slot], sem.at[1 - slot])

    pltpu.make_async_copy(os_hbm.at[pl.ds(0, TOP_K * tm)], buf.at[slot], sem.at[slot]).wait()
    y = x2_ref[...]
    gates = gates_ref[...]
    for j in range(TOP_K):
        y = y + gates[:, j:j + 1] * buf[slot, j * tm:(j + 1) * tm, :]
    y_ref[...] = _rms(y, gfin_ref[...])


def moe_combine(os_rows, dest, x2, gates, g_final):
    n = x2.shape[0]
    n_tiles, _, tm = dest.shape
    dest_spec = lambda step: pl.BlockSpec(
        (ROUTE_SUBLANES * tm,), lambda i: (jnp.minimum(i + step, n_tiles - 1),), memory_space=pltpu.SMEM)
    dest = dest.reshape(-1)
    return pl.pallas_call(
        functools.partial(_combine_kernel, n_tiles),
        out_shape=jax.ShapeDtypeStruct((n, D_MODEL), F32),
        grid=(n_tiles,),
        in_specs=[dest_spec(0), dest_spec(1),
                  pl.BlockSpec(memory_space=pl.ANY),
                  pl.BlockSpec((tm, D_MODEL), lambda i: (i, 0)),
                  pl.BlockSpec((tm, ROUTER_LANES), lambda i: (i, 0)),
                  _resident((1, D_MODEL))],
        out_specs=pl.BlockSpec((tm, D_MODEL), lambda i: (i, 0)),
        scratch_shapes=[pltpu.VMEM((2, TOP_K * tm, D_MODEL), F32), pltpu.SemaphoreType.DMA((2,))],
        compiler_params=_cparams_gather(),
        name="moe_combine",
    )(dest, dest, os_rows, x2, gates, g_final.reshape(1, D_MODEL))


def moe_ffn(groups, w):
    sizes = [g[0].shape[0] for g in groups]
    n = sum(sizes)
    tm = next(c for c in (256, 128, 64, 32, 16, 8) if all(s % c == 0 for s in sizes))
    dest, counts = moe_route(jnp.concatenate([g[2] for g in groups], axis=0), tm)
    n_blocks = -(-(n * TOP_K + N_EXPERTS * (MOE_BLOCK - 1)) // MOE_BLOCK)
    pends, block_expert, n_used = _block_tables(counts, n_blocks)
    bounds = [0]
    for s in sizes:
        bounds.append(bounds[-1] + s // tm)
    xs = moe_dispatch([g[1] for g in groups], bounds, dest, pends, n_blocks * MOE_BLOCK)
    os_rows = moe_experts(xs, block_expert, n_used, w["w_e1"], w["b_e1"], w["w_e2"], w["b_e2"])
    return [moe_combine(os_rows, dest[lo:hi], g[0], g[3], w["g_final"])
            for g, lo, hi in zip(groups, bounds[:-1], bounds[1:])]


def _mixers(x, prev0, s0_t, k_past, v_past, mk, mv, w, chunk, prep_tile, tq):
    b, t, _ = x.shape
    n = b * t
    xf = x.reshape(n, D_MODEL)
    q, k_sb, v_sb, rw, gate = in_proj(xf, w["g_mix"], w["w_in"])
    shift = norm_rows(x[:, -1], w["g_mix"])
    prep = rw_prep(rw.reshape(b, t, D_RW_COLS), prev0, w["rw"], chunk, prep_tile)
    o_rw, s_t = rw_chunks(prep, s0_t, w["gn_w"], w["gn_b"], chunk)
    seq = lambda a: a.reshape(b, t, D_MIX)
    o_sb = stick_breaking(seq(q), seq(k_sb), seq(v_sb), k_past, v_past, tq)
    x1, qx = merge_out(o_rw.reshape(n, D_MIX), o_sb.reshape(n, D_MIX), gate, xf,
                       w["wb0"], w["wb1"], w["w_out"], w["g_xattn"], w["w_xq"])
    x2, hf, idx, gates = xattn_router(qx.reshape(b, t, D_MODEL), x1.reshape(b, t, D_MODEL), mk, mv,
                                      w["w_xo"], w["g_ffn"], w["w_router"], w["b_router"])
    tok = (x2.reshape(n, D_MODEL), hf.reshape(n, D_MODEL), idx.reshape(n, ROUTER_LANES),
           gates.reshape(n, ROUTER_LANES))
    return tok, shift, s_t, k_sb, v_sb


def _state_to_t(s):
    b = s.shape[0]
    return s.transpose(0, 3, 1, 2).reshape(b, HEAD_DIM, D_MIX)


def _state_from_t(s_t):
    b = s_t.shape[0]
    return s_t.reshape(b, HEAD_DIM, N_HEADS, HEAD_DIM).transpose(0, 2, 3, 1)


def kernel(x_prompt, x_sample, state_rw_shift, state_rw_wkv, cache_sb_k, cache_sb_v, cache_mem_k, cache_mem_v, mem_prompt, g_mix, w_in, rw_mu, rw_w0, rw_w_up, rw_a0, rw_a_up, rw_g_up, rw_k_k, rw_k_a, rw_r_k, rw_gn_w, rw_gn_b, w_branch, w_out, g_xattn, g_mem, w_xq, w_mk, w_mv, w_xo, g_ffn, w_router, b_router, w_e1, b_e1, w_e2, b_e2, g_final):
    assert g_mix.shape[0] == 1, "single-layer trunk"
    row = lambda a: a.reshape(1, -1)
    w = dict(
        g_mix=g_mix[0], w_in=w_in[0].astype(BF16),
        rw=dict(mu=row(rw_mu[0]), w0=row(rw_w0[0]), a0=row(rw_a0[0]), k_k=row(rw_k_k[0]), k_a=row(rw_k_a[0]),
                r_k=row(rw_r_k[0]), w_up=rw_w_up[0].astype(BF16), a_up=rw_a_up[0].astype(BF16),
                g_up=rw_g_up[0].astype(BF16)),
        gn_w=row(rw_gn_w[0]), gn_b=row(rw_gn_b[0]),
        wb0=w_branch[0, 0].astype(BF16), wb1=w_branch[0, 1].astype(BF16), w_out=w_out[0].astype(BF16),
        g_xattn=g_xattn[0], w_xq=w_xq[0].astype(BF16), w_xo=w_xo[0].astype(BF16),
        g_ffn=g_ffn[0], w_router=w_router[0], b_router=b_router[0],
        w_e1=w_e1[0], b_e1=b_e1[0], w_e2=w_e2[0], b_e2=b_e2[0],
        g_final=g_final,
    )
    bp, t, _ = x_prompt.shape
    bs, ts, _ = x_sample.shape
    n_mem = mem_prompt.shape[1]

    mk_p, mv_p = mem_kv(mem_prompt.reshape(bp * n_mem, D_MODEL), g_mem[0],
                        w_mk[0].astype(BF16), w_mv[0].astype(BF16))
    mk_p = mk_p.reshape(bp, n_mem, D_MODEL)
    mv_p = mv_p.reshape(bp, n_mem, D_MODEL)
    tok_p, sh_p, st_p, k_p, v_p = _mixers(
        x_prompt, jnp.zeros((bp, 1, D_RW_COLS), F32), jnp.zeros((bp, HEAD_DIM, D_MIX), F32),
        None, None, mk_p, mv_p, w, chunk=64, prep_tile=256, tq=256)

    prev_s = plain_proj(state_rw_shift[0], w["w_in"][:, SB_COLS:SB_COLS + D_RW_COLS])
    past = cache_sb_k.shape[2]
    tok_s, sh_s, st_s, k_s, v_s = _mixers(
        x_sample, prev_s.reshape(bs, 1, D_RW_COLS), _state_to_t(state_rw_wkv[0]),
        cache_sb_k[0].reshape(bs, past, D_MIX), cache_sb_v[0].reshape(bs, past, D_MIX),
        cache_mem_k[0].reshape(bs, n_mem, D_MODEL), cache_mem_v[0].reshape(bs, n_mem, D_MODEL),
        w, chunk=ts, prep_tile=ts, tq=ts)

    y_p, y_s = moe_ffn([tok_p, tok_s], w)

    heads = lambda a, b_, t_: a.reshape(1, b_, t_, N_HEADS, HEAD_DIM)
    xh = lambda a: a.reshape(1, bp, n_mem, N_X_HEADS, X_HEAD_DIM)
    return (y_p.reshape(bp, t, D_MODEL), y_s.reshape(bs, ts, D_MODEL),
            sh_p[None], _state_from_t(st_p)[None], heads(k_p, bp, t), heads(v_p, bp, t), xh(mk_p), xh(mv_p),
            sh_s[None], _state_from_t(st_s)[None], heads(k_s, bs, ts), heads(v_s, bs, ts))
```

```python
import functools

import jax
import jax.numpy as jnp
from jax import lax
from jax.experimental import pallas as pl
from jax.experimental.pallas import tpu as pltpu

F32 = jnp.float32
BF16 = jnp.bfloat16

D_MODEL = 1024
HEAD_DIM = 64
N_HEADS = 8
D_MIX = N_HEADS * HEAD_DIM
LORA_W, LORA_A, LORA_G = 64, 64, 128
D_RW_COLS = 3 * D_MIX + LORA_W + LORA_A + LORA_G
N_X_HEADS = 4
X_HEAD_DIM = 256
N_EXPERTS = 32
TOP_K = 4
D_FF = 1024
SWIGLU_LIMIT = 7.0
SWIGLU_ALPHA = 1.702
MOE_BLOCK = 256
RMS_EPS = 1e-5
GN_EPS = 64e-5

ROW_TILE = 256
RW_GROUP = 4
RW_LANES = RW_GROUP * HEAD_DIM
RW_SEQS = 4
VMEM_LIMIT = 56 * 1024 * 1024


def _cparams(*sem):
    return pltpu.CompilerParams(dimension_semantics=sem, vmem_limit_bytes=VMEM_LIMIT)


def _cparams_gather():
    return pltpu.CompilerParams(dimension_semantics=("arbitrary",), vmem_limit_bytes=VMEM_LIMIT,
                                disable_bounds_checks=True)


def _resident(shape):
    nd = len(shape)
    return pl.BlockSpec(shape, lambda *_: (0,) * nd)


_NN = ((1,), (0,))
_NT = ((1,), (1,))
_TN = ((0,), (0,))


def _dot(a, b, dims=_NN):
    return lax.dot_general(a.astype(BF16), b.astype(BF16), (dims, ((), ())),
                           preferred_element_type=F32)


def _split(x, n):
    parts, rem = [], x
    for i in range(n):
        p = rem.astype(BF16)
        parts.append(p)
        if i + 1 < n:
            rem = rem - p.astype(F32)
    return parts


def _dot_exact_rhs(a, b_bf16, dims=_NN, n=3):
    out = None
    for p in _split(a, n):
        t = lax.dot_general(p, b_bf16, (dims, ((), ())), preferred_element_type=F32)
        out = t if out is None else out + t
    return out


def _dot_exact_lhs(a_bf16, b, dims=_NN, n=3):
    out = None
    for p in _split(b, n):
        t = lax.dot_general(a_bf16, p, (dims, ((), ())), preferred_element_type=F32)
        out = t if out is None else out + t
    return out


def _dot3(a, b, dims=_NN):
    ah, al = _split(a, 2)
    bh, bl = _split(b, 2)
    dn = (dims, ((), ()))
    return (lax.dot_general(ah, bh, dn, preferred_element_type=F32)
            + (lax.dot_general(ah, bl, dn, preferred_element_type=F32)
               + lax.dot_general(al, bh, dn, preferred_element_type=F32)))


def _rms(x, g):
    return x * lax.rsqrt(jnp.mean(x * x, axis=-1, keepdims=True) + RMS_EPS) * g


def _iota(shape, dim):
    return lax.broadcasted_iota(jnp.int32, shape, dim)


def _block_ones(n, width):
    return (_iota((n, n), 0) // width == _iota((n, n), 1) // width).astype(BF16)


def _bd_rows(x, nblk, width):
    blk = _iota(x.shape, 1) // width
    zero = jnp.zeros_like(x)
    return jnp.concatenate([jnp.where(blk == h, x, zero) for h in range(nblk)], axis=0)


def _norm_rows_kernel(x_ref, g_ref, o_ref):
    o_ref[...] = _rms(x_ref[...], g_ref[...])


def norm_rows(x, g):
    r, d = x.shape
    return pl.pallas_call(
        _norm_rows_kernel,
        out_shape=jax.ShapeDtypeStruct((r, d), F32),
        grid=(1,),
        in_specs=[_resident((r, d)), _resident((1, d))],
        out_specs=_resident((r, d)),
        name="norm_rows",
    )(x, g.reshape(1, d))


SB_COLS = 3 * D_MIX
GATE_COLS = 2 * D_MODEL
D_IN = SB_COLS + D_RW_COLS + GATE_COLS


def _in_proj_kernel(x_ref, g_ref, w_ref, q_ref, k_ref, v_ref, rw_ref, gate_ref):
    h = _rms(x_ref[...], g_ref[...]).astype(BF16)

    def proj(lo, hi):
        return jnp.dot(h, w_ref[:, lo:hi], preferred_element_type=F32)

    q_ref[...] = proj(0, D_MIX).astype(BF16)
    k_ref[...] = proj(D_MIX, 2 * D_MIX)
    v_ref[...] = proj(2 * D_MIX, SB_COLS)
    rw_ref[...] = proj(SB_COLS, SB_COLS + D_RW_COLS)
    gate_ref[...] = proj(SB_COLS + D_RW_COLS, D_IN).astype(BF16)


def in_proj(x, g, w_bf16):
    n = x.shape[0]
    tm = min(ROW_TILE, n)
    row = lambda width: pl.BlockSpec((tm, width), lambda i: (i, 0))
    return pl.pallas_call(
        _in_proj_kernel,
        out_shape=(jax.ShapeDtypeStruct((n, D_MIX), BF16),
                   jax.ShapeDtypeStruct((n, D_MIX), F32),
                   jax.ShapeDtypeStruct((n, D_MIX), F32),
                   jax.ShapeDtypeStruct((n, D_RW_COLS), F32),
                   jax.ShapeDtypeStruct((n, GATE_COLS), BF16)),
        grid=(n // tm,),
        in_specs=[row(D_MODEL), _resident((1, D_MODEL)), _resident((D_MODEL, D_IN))],
        out_specs=(row(D_MIX), row(D_MIX), row(D_MIX), row(D_RW_COLS), row(GATE_COLS)),
        compiler_params=_cparams("parallel"),
        name="in_proj",
    )(x, g.reshape(1, D_MODEL), w_bf16)


def _plain_proj_kernel(x_ref, w_ref, o_ref):
    o_ref[...] = _dot(x_ref[...], w_ref[...])


def plain_proj(x, w_bf16):
    r, m = x.shape[0], w_bf16.shape[1]
    return pl.pallas_call(
        _plain_proj_kernel,
        out_shape=jax.ShapeDtypeStruct((r, m), F32),
        grid=(1,),
        in_specs=[_resident(x.shape), _resident(w_bf16.shape)],
        out_specs=_resident((r, m)),
        name="plain_proj",
    )(x, w_bf16)


def _rw_prep_kernel(chunk, rw_ref, prev0_ref, mu_ref, w0_ref, a0_ref, kk_ref, ka_ref, rk_ref,
                    wup_ref, aup_ref, gup_ref,
                    rt_ref, at_ref, bt_ref, kt_ref, bh_ref, kh_ref, v_ref, g_ref, bonus_ref, wc_ref,
                    carry_ref):
    tt = rw_ref.shape[1]

    @pl.when(pl.program_id(1) == 0)
    def _():
        carry_ref[...] = prev0_ref[0]

    p = rw_ref[0]
    prev = jnp.where(_iota(p.shape, 0) == 0, carry_ref[...], pltpu.roll(p, 1, 0))
    carry_ref[...] = p[tt - 1:tt]
    xs = p + mu_ref[...] * (prev - p)
    r = xs[:, 0:D_MIX]
    k = xs[:, D_MIX:2 * D_MIX]
    v = xs[:, 2 * D_MIX:3 * D_MIX]
    lo = 3 * D_MIX
    wd = xs[:, lo:lo + LORA_W]
    ad = xs[:, lo + LORA_W:lo + LORA_W + LORA_A]
    gd = xs[:, lo + LORA_W + LORA_A:D_RW_COLS]

    pre_w = w0_ref[...] + _dot(jnp.tanh(wd), wup_ref[...])
    w_log = -jax.nn.softplus(-pre_w) - 0.5
    logw = -jnp.exp(w_log)
    a = jax.nn.sigmoid(a0_ref[...] + _dot(ad, aup_ref[...]))
    g_ref[0] = _dot(jax.nn.sigmoid(gd), gup_ref[...])

    head_ones = _block_ones(D_MIX, HEAD_DIM)
    kk = k * kk_ref[...]
    ss = _dot_exact_rhs(kk * kk, head_ones)
    kk = kk / jnp.maximum(jnp.sqrt(ss), 1e-12)
    k2 = k * (1.0 + (a - 1.0) * ka_ref[...])
    nb = -(kk * a)
    bonus_ref[0] = _dot_exact_rhs(r * k2 * rk_ref[...], head_ones) * v
    v_ref[0] = v.astype(BF16)

    ti, tj = _iota((tt, tt), 0), _iota((tt, tt), 1)
    same = ti // chunk == tj // chunk
    cum = _dot_exact_lhs((same & (tj <= ti)).astype(BF16), logw)
    tot = _dot_exact_lhs(same.astype(BF16), logw)
    e_in = jnp.exp(cum)
    e_out = jnp.exp(-cum)
    e_end = jnp.exp(tot - cum)
    rt_ref[0] = (r * e_in).astype(BF16)
    at_ref[0] = (kk * jnp.exp(cum - logw)).astype(BF16)
    bt_ref[0] = (nb * e_out).astype(BF16)
    kt_ref[0] = (k2 * e_out).astype(BF16)
    bh_ref[0] = (nb * e_end).astype(BF16)
    kh_ref[0] = (k2 * e_end).astype(BF16)
    etot = jnp.exp(tot)
    for c in range(tt // chunk):
        wc_ref[0, c] = etot[c * chunk:c * chunk + 1]


def rw_prep(rw, prev0, p, chunk, tt):
    b, t, _ = rw.shape
    seq = lambda width, dt: jax.ShapeDtypeStruct((b, t, width), dt)
    tile = lambda width: pl.BlockSpec((1, tt, width), lambda i, j: (i, j, 0))
    vec = lambda width: _resident((1, width))
    n_c = tt // chunk
    return pl.pallas_call(
        functools.partial(_rw_prep_kernel, chunk),
        out_shape=tuple(seq(D_MIX, BF16) for _ in range(7))
        + (seq(D_MIX, F32), seq(D_MIX, F32), jax.ShapeDtypeStruct((b, t // chunk, 1, D_MIX), F32)),
        grid=(b, t // tt),
        in_specs=[tile(D_RW_COLS), pl.BlockSpec((1, 1, D_RW_COLS), lambda i, j: (i, 0, 0)),
                  vec(D_RW_COLS), vec(D_MIX), vec(D_MIX), vec(D_MIX), vec(D_MIX), vec(D_MIX),
                  _resident((LORA_W, D_MIX)), _resident((LORA_A, D_MIX)), _resident((LORA_G, D_MIX))],
        out_specs=tuple(tile(D_MIX) for _ in range(9))
        + (pl.BlockSpec((1, n_c, 1, D_MIX), lambda i, j: (i, j, 0, 0)),),
        scratch_shapes=[pltpu.VMEM((1, D_RW_COLS), F32)],
        compiler_params=_cparams("parallel", "arbitrary"),
        name="rw_prep",
    )(rw, prev0, p["mu"], p["w0"], p["a0"], p["k_k"], p["k_a"], p["r_k"],
      p["w_up"], p["a_up"], p["g_up"])


def _unit_lower_inverse(a, chunk):
    shape = a.shape
    eye = (_iota(shape, 1) % chunk == _iota(shape, 0)).astype(F32)
    res = eye + a
    power = a.astype(BF16)
    for _ in range(chunk.bit_length() - 2):
        power = _dot(power, _bd_rows(power, RW_GROUP, chunk)).astype(BF16)
        yield
        res = res + _dot(res, _bd_rows(power, RW_GROUP, chunk))
    return res


def _interleave(chains):
    live = list(chains)
    while live:
        live = [c for c in live if next(c, StopIteration) is not StopIteration]


def _rw_chunk_kernel(chunk, rt_ref, at_ref, bt_ref, kt_ref, bh_ref, kh_ref, v_ref, g_ref, bonus_ref,
                     wc_ref, s0_ref, gnw_ref, gnb_ref, o_ref, s_out_ref, st_ref):
    c = pl.program_id(1)

    @pl.when(c == 0)
    def _():
        st_ref[...] = s0_ref[...]

    cat = (chunk, RW_GROUP * chunk)
    col_t = _iota(cat, 1) % chunk
    strict = col_t < _iota(cat, 0)
    incl = col_t <= _iota(cat, 0)
    head_ones = _block_ones(RW_LANES, HEAD_DIM)
    bd = lambda x: _bd_rows(x, RW_GROUP, HEAD_DIM)
    n_cat = RW_GROUP * chunk
    def chain(bi, grp):
        sl = slice(grp * RW_LANES, (grp + 1) * RW_LANES)
        rt, at, bt, kt = rt_ref[bi, :, sl], at_ref[bi, :, sl], bt_ref[bi, :, sl], kt_ref[bi, :, sl]
        bh, kh, v = bh_ref[bi, :, sl], kh_ref[bi, :, sl], v_ref[bi, :, sl]
        st = st_ref[bi, :, sl]

        gram = _dot(jnp.concatenate([at, rt], axis=0),
                    jnp.concatenate([bd(bt), bd(kt)], axis=0), _NT)
        yield
        zero = jnp.zeros(cat, F32)
        a_ab = jnp.where(strict, gram[:chunk, :n_cat], zero)
        a_ak = jnp.where(strict, gram[:chunk, n_cat:], zero)
        a_r = jnp.concatenate([jnp.where(incl, gram[chunk:, :n_cat], zero),
                               jnp.where(incl, gram[chunk:, n_cat:], zero)], axis=1).astype(BF16)
        v_bd = bd(v)
        x = _dot(a_ak, v_bd).astype(BF16)
        t_inv = yield from _unit_lower_inverse(a_ab, chunk)
        yield
        ua = _dot(t_inv, jnp.concatenate([bd(x), bd(at)], axis=1))
        st_bd = bd(st.astype(BF16))
        yield
        from_state = _dot(jnp.concatenate([ua[:, RW_LANES:].astype(BF16), rt], axis=0), st_bd)
        yield
        u = (ua[:, :RW_LANES] + from_state[:chunk]).astype(BF16)
        y = from_state[chunk:] + _dot(a_r, jnp.concatenate([bd(u), v_bd], axis=0))
        m = _dot(jnp.concatenate([bh, kh], axis=0), jnp.concatenate([u, v], axis=0), _TN)
        yield
        lane_head = _iota((HEAD_DIM, RW_LANES), 1) // HEAD_DIM
        fold = jnp.zeros((HEAD_DIM, RW_LANES), F32)
        for h in range(RW_GROUP):
            fold = fold + jnp.where(lane_head == h, m[h * HEAD_DIM:(h + 1) * HEAD_DIM], 0.0)
        diag = _iota((HEAD_DIM, RW_LANES), 1) % HEAD_DIM == _iota((HEAD_DIM, RW_LANES), 0)
        w_rows = jnp.where(diag, wc_ref[bi, 0, :, sl], 0.0)
        w_t = _dot_exact_rhs(w_rows, head_ones)
        st_ref[bi, :, sl] = st * w_t + fold

        mean = _dot_exact_rhs(y, head_ones, n=2) * (1.0 / HEAD_DIM)
        yield
        d = y - mean
        var = _dot_exact_rhs(d * d, head_ones, n=2) * (1.0 / HEAD_DIM)
        yield
        yn = d * lax.rsqrt(var + GN_EPS) * gnw_ref[:, sl] + gnb_ref[:, sl]
        o_ref[bi, :, sl] = ((yn + bonus_ref[bi, :, sl]) * g_ref[bi, :, sl]).astype(BF16)

    _interleave(chain(bi, grp) for bi in range(st_ref.shape[0]) for grp in range(N_HEADS // RW_GROUP))

    @pl.when(c == pl.num_programs(1) - 1)
    def _():
        s_out_ref[...] = st_ref[...]


def rw_chunks(prep, s0_t, gn_w, gn_b, chunk):
    rt, at, bt, kt, bh, kh, v, g, bonus, wc = prep
    b, t, _ = rt.shape
    nb = RW_SEQS if b % RW_SEQS == 0 else 1
    tile = pl.BlockSpec((nb, chunk, D_MIX), lambda i, j: (i, j, 0))
    state = pl.BlockSpec((nb, HEAD_DIM, D_MIX), lambda i, j: (i, 0, 0))
    return pl.pallas_call(
        functools.partial(_rw_chunk_kernel, chunk),
        out_shape=(jax.ShapeDtypeStruct((b, t, D_MIX), BF16),
                   jax.ShapeDtypeStruct((b, HEAD_DIM, D_MIX), F32)),
        grid=(b // nb, t // chunk),
        in_specs=[tile] * 9 + [pl.BlockSpec((nb, 1, 1, D_MIX), lambda i, j: (i, j, 0, 0)), state,
                               _resident((1, D_MIX)), _resident((1, D_MIX))],
        out_specs=(tile, state),
        scratch_shapes=[pltpu.VMEM((nb, HEAD_DIM, D_MIX), F32)],
        compiler_params=_cparams("parallel", "arbitrary"),
        name="rw_chunks",
    )(rt, at, bt, kt, bh, kh, v, g, bonus, wc, s0_t, gn_w, gn_b)


SB_PAIR = 2 * HEAD_DIM
SB_PAST_BLOCK = 256
SB_VISITS = 4


def _sb_kernel(tq, n_past, q_ref, kn_ref, vn_ref, *rest):
    if n_past:
        kp_ref, vp_ref, o_ref, acc_ref, c_ref = rest
    else:
        o_ref, acc_ref, c_ref = rest
    qi = pl.program_id(2)
    q2 = q_ref[0] * (HEAD_DIM ** -0.5)
    lane = _iota(q2.shape, 1)
    zero_q = jnp.zeros_like(q2)
    q_st = jnp.concatenate([jnp.where(lane < HEAD_DIM, q2, zero_q), jnp.where(lane >= HEAD_DIM, q2, zero_q)],
                           axis=0)
    acc_ref[...] = jnp.zeros_like(acc_ref)
    c_ref[...] = jnp.zeros_like(c_ref)

    def visit(k_blk, v_blk, masked):
        tk = k_blk.shape[0]
        later = (_iota((tk, tk), 0) > _iota((tk, tk), 1)).astype(BF16)
        z = _dot(q_st, k_blk, _NT)
        sp = jnp.maximum(z, 0.0) + jnp.log(1.0 + jnp.exp(-jnp.abs(z)))
        if masked:
            before = _iota((2 * tq, tk), 1) < _iota((2 * tq, tk), 0) % tq
            drop = jnp.where(before, sp, 0.0)
        else:
            drop = sp
        srev = _dot(drop, later)
        yield
        c = c_ref[...]
        att = jnp.exp((z - sp) - (c + srev))
        if masked:
            att = jnp.where(before, att, 0.0)
        acc_ref[...] += _dot(att, v_blk)
        c_ref[...] = c + (srev[:, 0:1] + drop[:, 0:1])

    def new_block(blk, masked=False):
        s = pl.multiple_of(blk * tq, tq)
        return visit(kn_ref[0, pl.ds(s, tq), :], vn_ref[0, pl.ds(s, tq), :], masked)

    _interleave([new_block(qi, masked=True)])

    def earlier_group(i, carry):
        _interleave(new_block(qi - 1 - u - SB_VISITS * i) for u in range(SB_VISITS))
        return carry

    lax.fori_loop(0, qi // SB_VISITS, earlier_group, 0)
    rest = qi % SB_VISITS

    @pl.when(rest >= 2)
    def _():
        _interleave([new_block(rest - 1), new_block(rest - 2)])

    @pl.when(rest % 2 == 1)
    def _():
        _interleave([new_block(0)])

    if n_past:
        def past_block(blk):
            s = pl.multiple_of(blk * SB_PAST_BLOCK, SB_PAST_BLOCK)
            return visit(kp_ref[0, pl.ds(s, SB_PAST_BLOCK), :], vp_ref[0, pl.ds(s, SB_PAST_BLOCK), :], False)

        def past_group(i, carry):
            _interleave(past_block(n_past - 1 - u - SB_VISITS * i) for u in range(SB_VISITS))
            return carry

        lax.fori_loop(0, n_past // SB_VISITS, past_group, 0)
        _interleave(past_block(blk) for blk in reversed(range(n_past % SB_VISITS)))

    o_ref[0] = jnp.where(lane < HEAD_DIM, acc_ref[:tq], acc_ref[tq:]).astype(BF16)


def stick_breaking(q, k_new, v_new, k_past, v_past, tq):
    b, t, _ = q.shape
    n_pairs = D_MIX // SB_PAIR
    qspec = pl.BlockSpec((1, tq, SB_PAIR), lambda i, p, j: (i, j, p))
    seq = lambda length: pl.BlockSpec((1, length, SB_PAIR), lambda i, p, j: (i, 0, p))
    args, specs, n_past = [q, k_new, v_new], [qspec, seq(t), seq(t)], 0
    if k_past is not None:
        past_len = k_past.shape[1]
        n_past = past_len // SB_PAST_BLOCK
        args += [k_past, v_past]
        specs += [seq(past_len), seq(past_len)]
    return pl.pallas_call(
        functools.partial(_sb_kernel, tq, n_past),
        out_shape=jax.ShapeDtypeStruct((b, t, D_MIX), BF16),
        grid=(b, n_pairs, t // tq),
        in_specs=specs,
        out_specs=qspec,
        scratch_shapes=[pltpu.VMEM((2 * tq, SB_PAIR), F32), pltpu.VMEM((2 * tq, 1), F32)],
        compiler_params=_cparams("parallel", "parallel", "arbitrary"),
        name="stick_breaking",
    )(*args)


def _merge_kernel(orw_ref, osb_ref, gate_ref, x_ref, wb0_ref, wb1_ref, wout_ref, gx_ref, wxq_ref,
                  x1_ref, qx_ref):
    g = jax.nn.sigmoid(gate_ref[...].astype(F32))
    mixed = (g[:, :D_MODEL] * jnp.dot(orw_ref[...], wb0_ref[...], preferred_element_type=F32)
             + g[:, D_MODEL:] * jnp.dot(osb_ref[...], wb1_ref[...], preferred_element_type=F32))
    x1 = x_ref[...] + _dot(mixed, wout_ref[...])
    x1_ref[...] = x1
    qx_ref[...] = _dot(_rms(x1, gx_ref[...]), wxq_ref[...]).astype(BF16)


def merge_out(o_rw, o_sb, gate, x, wb0, wb1, wout, gx, wxq):
    n = x.shape[0]
    tm = min(ROW_TILE, n)
    row = lambda width: pl.BlockSpec((tm, width), lambda i: (i, 0))
    return pl.pallas_call(
        _merge_kernel,
        out_shape=(jax.ShapeDtypeStruct((n, D_MODEL), F32), jax.ShapeDtypeStruct((n, D_MODEL), BF16)),
        grid=(n // tm,),
        in_specs=[row(D_MIX), row(D_MIX), row(GATE_COLS), row(D_MODEL),
                  _resident((D_MIX, D_MODEL)), _resident((D_MIX, D_MODEL)), _resident((D_MODEL, D_MODEL)),
                  _resident((1, D_MODEL)), _resident((D_MODEL, D_MODEL))],
        out_specs=(row(D_MODEL), row(D_MODEL)),
        compiler_params=_cparams("parallel"),
        name="merge_out",
    )(o_rw, o_sb, gate, x, wb0, wb1, wout, gx.reshape(1, D_MODEL), wxq)


def _mem_kv_kernel(m_ref, g_ref, wk_ref, wv_ref, k_ref, v_ref):
    h = _rms(m_ref[...], g_ref[...]).astype(BF16)
    k_ref[...] = jnp.dot(h, wk_ref[...], preferred_element_type=F32)
    v_ref[...] = jnp.dot(h, wv_ref[...], preferred_element_type=F32)


def mem_kv(mem, g, wk, wv):
    n = mem.shape[0]
    tm = min(ROW_TILE, n)
    row = pl.BlockSpec((tm, D_MODEL), lambda i: (i, 0))
    return pl.pallas_call(
        _mem_kv_kernel,
        out_shape=(jax.ShapeDtypeStruct((n, D_MODEL), F32),) * 2,
        grid=(n // tm,),
        in_specs=[row, _resident((1, D_MODEL)), _resident((D_MODEL, D_MODEL)), _resident((D_MODEL, D_MODEL))],
        out_specs=(row, row),
        compiler_params=_cparams("parallel"),
        name="mem_kv",
    )(mem, g.reshape(1, D_MODEL), wk, wv)


ROUTER_LANES = 128


def _xattn_kernel(q_ref, x1_ref, mk_ref, mv_ref, wxo_ref, gf_ref, wr_ref, br_ref,
                  x2_ref, hf_ref, idx_ref, gates_ref):
    q = q_ref[0]
    heads = [None] * N_X_HEADS

    def head(h):
        sl = slice(h * X_HEAD_DIM, (h + 1) * X_HEAD_DIM)
        s = _dot(q[:, sl], mk_ref[0, :, sl], _NT) * (X_HEAD_DIM ** -0.5)
        yield
        e = jnp.exp(s - jnp.max(s, axis=-1, keepdims=True))
        p = e * (1.0 / jnp.sum(e, axis=-1, keepdims=True))
        heads[h] = _dot(p, mv_ref[0, :, sl])

    _interleave(head(h) for h in range(N_X_HEADS))
    x2 = x1_ref[0] + _dot(jnp.concatenate(heads, axis=1), wxo_ref[...])
    x2_ref[0] = x2
    hf = _rms(x2, gf_ref[...])
    hf_ref[0] = hf
    logits = _dot3(hf, wr_ref[...]) + br_ref[...]
    lane = _iota(logits.shape, 1)
    vals, idx_out = [], jnp.zeros(logits.shape, jnp.int32)
    for j in range(TOP_K):
        m = jnp.max(logits, axis=-1, keepdims=True)
        pick = jnp.min(jnp.where(logits == m, lane, ROUTER_LANES), axis=-1, keepdims=True)
        vals.append(m)
        idx_out = jnp.where(lane == j, pick, idx_out)
        logits = jnp.where(lane == pick, -jnp.inf, logits)
    exps = [jnp.exp(v - vals[0]) for v in vals]
    denom = exps[0] + exps[1] + exps[2] + exps[3]
    gates = jnp.zeros(logits.shape, F32)
    for j in range(TOP_K):
        gates = jnp.where(lane == j, exps[j] / denom, gates)
    idx_ref[0] = idx_out
    gates_ref[0] = gates


def xattn_router(qx, x1, mk, mv, wxo, g_ffn, w_router, b_router):
    b, t, _ = x1.shape
    tm = min(ROW_TILE, t)
    n_mem = mk.shape[1]
    wr = jnp.zeros((D_MODEL, ROUTER_LANES), F32).at[:, :N_EXPERTS].set(w_router)
    br = jnp.full((1, ROUTER_LANES), -jnp.inf, F32).at[0, :N_EXPERTS].set(b_router)
    tile = pl.BlockSpec((1, tm, D_MODEL), lambda i, j: (i, j, 0))
    mem = pl.BlockSpec((1, n_mem, D_MODEL), lambda i, j: (i, 0, 0))
    small = pl.BlockSpec((1, tm, ROUTER_LANES), lambda i, j: (i, j, 0))
    return pl.pallas_call(
        _xattn_kernel,
        out_shape=(jax.ShapeDtypeStruct((b, t, D_MODEL), F32), jax.ShapeDtypeStruct((b, t, D_MODEL), F32),
                   jax.ShapeDtypeStruct((b, t, ROUTER_LANES), jnp.int32),
                   jax.ShapeDtypeStruct((b, t, ROUTER_LANES), F32)),
        grid=(b, t // tm),
        in_specs=[tile, tile, mem, mem, _resident((D_MODEL, D_MODEL)), _resident((1, D_MODEL)),
                  _resident((D_MODEL, ROUTER_LANES)), _resident((1, ROUTER_LANES))],
        out_specs=(tile, tile, small, small),
        compiler_params=_cparams("parallel", "parallel"),
        name="xattn_router",
    )(qx, x1, mk, mv, wxo, g_ffn.reshape(1, D_MODEL), wr, br)


ROW_GROUP = 8
ROUTE_SUBLANES = 8


def _route_kernel(idx_ref, dest_ref, counts_ref, carry_ref, start_ref):
    phase, t = pl.program_id(0), pl.program_id(1)
    tm = idx_ref.shape[0]
    idx = idx_ref[...]
    lane = _iota((tm, ROUTER_LANES), 1)
    picks = [idx[:, j:j + 1] for j in range(TOP_K)]
    onehot = jnp.zeros((tm, ROUTER_LANES), F32)
    for pick in picks:
        onehot = onehot + (lane == pick).astype(F32)
    col_sum = jnp.broadcast_to(jnp.sum(onehot, axis=0, keepdims=True), carry_ref.shape)

    @pl.when(t == 0)
    def _():
        carry_ref[...] = jnp.zeros_like(carry_ref)

    @pl.when(phase == 0)
    def _():
        carry_ref[...] += col_sum

        @pl.when(t == pl.num_programs(1) - 1)
        def _():
            total = carry_ref[...]
            counts_ref[...] = total
            padded = jnp.floor((total + (MOE_BLOCK - 1.0)) * (1.0 / MOE_BLOCK)) * MOE_BLOCK
            earlier = (_iota((ROUTER_LANES, ROUTER_LANES), 0) < _iota((ROUTER_LANES, ROUTER_LANES), 1))
            start_ref[...] = _dot_exact_rhs(padded, earlier.astype(BF16))

    @pl.when(phase == 1)
    def _():
        below = (_iota((tm, tm), 1) < _iota((tm, tm), 0)).astype(BF16)
        row = _dot(below, onehot) + (carry_ref[0:1, :] + start_ref[0:1, :])
        dest = jnp.zeros((tm, ROUTER_LANES), F32)
        for j, pick in enumerate(picks):
            mine = jnp.sum(jnp.where(lane == pick, row, 0.0), axis=1, keepdims=True)
            dest = jnp.where(lane == j, mine, dest)
        dest_ref[0] = dest.T[:ROUTE_SUBLANES].astype(jnp.int32)
        carry_ref[...] += col_sum


def moe_route(idx, tm):
    n = idx.shape[0]
    n_tiles = n // tm
    return pl.pallas_call(
        _route_kernel,
        out_shape=(jax.ShapeDtypeStruct((n_tiles, ROUTE_SUBLANES, tm), jnp.int32),
                   jax.ShapeDtypeStruct((ROUTE_SUBLANES, ROUTER_LANES), F32)),
        grid=(2, n_tiles),
        in_specs=[pl.BlockSpec((tm, ROUTER_LANES), lambda p, t: (t, 0))],
        out_specs=(pl.BlockSpec((1, ROUTE_SUBLANES, tm), lambda p, t: (t * p, 0, 0)),
                   _resident((ROUTE_SUBLANES, ROUTER_LANES))),
        scratch_shapes=[pltpu.VMEM((ROUTE_SUBLANES, ROUTER_LANES), F32),
                        pltpu.VMEM((ROUTE_SUBLANES, ROUTER_LANES), F32)],
        compiler_params=_cparams("arbitrary", "arbitrary"),
        name="moe_route",
    )(idx)


def _block_tables(counts, n_blocks):
    cnt = counts[0, :N_EXPERTS].astype(jnp.int32)
    padded = (cnt + MOE_BLOCK - 1) // MOE_BLOCK * MOE_BLOCK
    pends = jnp.cumsum(padded).astype(jnp.int32)
    block_start = jnp.arange(n_blocks, dtype=jnp.int32) * MOE_BLOCK
    block_expert = jnp.minimum(jnp.sum((pends[None, :] <= block_start[:, None]).astype(jnp.int32), axis=1),
                               N_EXPERTS - 1)
    n_used = (pends[-1:] // MOE_BLOCK).astype(jnp.int32)
    return pends, block_expert, n_used


def _dispatch_kernel(bounds, pends_ref, dest_ref, *rest):
    hf_refs = rest[:len(bounds) - 1]
    xs_hbm, zbuf, sem = rest[len(bounds) - 1:]
    i = pl.program_id(0)

    @pl.when(i == 0)
    def _():
        zbuf[...] = jnp.zeros_like(zbuf)
        n_blocks = xs_hbm.shape[0] // MOE_BLOCK
        n_used = pends_ref[N_EXPERTS - 1] // MOE_BLOCK
        for wait in (False, True):
            def unused_block(blk, carry):
                cp = pltpu.make_async_copy(
                    zbuf, xs_hbm.at[pl.ds(pl.multiple_of(blk * MOE_BLOCK, MOE_BLOCK), MOE_BLOCK)], sem)
                cp.wait() if wait else cp.start()
                return carry

            lax.fori_loop(n_used, n_blocks, unused_block, 0)
            for e in range(N_EXPERTS):
                end = pends_ref[e]
                begin = pends_ref[e - 1] if e else 0

                @pl.when(end > begin)
                def _():
                    last = pl.multiple_of(end - MOE_BLOCK, MOE_BLOCK)
                    cp = pltpu.make_async_copy(zbuf, xs_hbm.at[pl.ds(last, MOE_BLOCK)], sem)
                    cp.wait() if wait else cp.start()

    for hf_ref, lo, hi in zip(hf_refs, bounds[:-1], bounds[1:]):
        @pl.when((i >= lo) & (i < hi))
        def _():
            tm = hf_ref.shape[0]
            for j in range(TOP_K):
                def issue(g, carry):
                    base = pl.multiple_of(g * ROW_GROUP, ROW_GROUP)
                    rows = hf_ref.at[pl.ds(base, ROW_GROUP)]
                    for u in range(ROW_GROUP):
                        pltpu.make_async_copy(rows.at[pl.ds(u, 1)], xs_hbm.at[pl.ds(dest_ref[j * tm + base + u], 1)],
                                              sem).start()
                    return carry

                lax.fori_loop(0, tm // ROW_GROUP, issue, 0)
            for j in range(TOP_K):
                pltpu.make_async_copy(hf_ref, xs_hbm.at[pl.ds(0, tm)], sem).wait()


def moe_dispatch(hfs, bounds, dest, pends, n_rows):
    n_tiles, _, tm = dest.shape
    hf_spec = lambda lo, hi: pl.BlockSpec((tm, D_MODEL), lambda i, pe: (jnp.clip(i - lo, 0, hi - lo - 1), 0))
    grid_spec = pltpu.PrefetchScalarGridSpec(
        num_scalar_prefetch=1,
        grid=(n_tiles,),
        in_specs=[pl.BlockSpec((ROUTE_SUBLANES * tm,), lambda i, pe: (i,), memory_space=pltpu.SMEM)]
        + [hf_spec(lo, hi) for lo, hi in zip(bounds[:-1], bounds[1:])],
        out_specs=pl.BlockSpec(memory_space=pl.ANY),
        scratch_shapes=[pltpu.VMEM((MOE_BLOCK, D_MODEL), F32), pltpu.SemaphoreType.DMA(())],
    )
    return pl.pallas_call(
        functools.partial(_dispatch_kernel, tuple(bounds)),
        out_shape=jax.ShapeDtypeStruct((n_rows, D_MODEL), F32),
        grid_spec=grid_spec,
        compiler_params=_cparams_gather(),
        name="moe_dispatch",
    )(pends, dest.reshape(-1), *hfs)


W1_CHUNKS = 4
W2_CHUNKS = 2


def _moe_kernel(be_ref, nused_ref, x_ref, *rest):
    w1 = rest[:W1_CHUNKS]
    b1_ref = rest[W1_CHUNKS]
    w2 = rest[W1_CHUNKS + 1:W1_CHUNKS + 1 + W2_CHUNKS]
    b2_ref, o_ref, w1_bf, w2_bf = rest[W1_CHUNKS + 1 + W2_CHUNKS:]
    i = pl.program_id(0)
    half = W1_CHUNKS // 2
    wide = 2 * D_FF // W1_CHUNKS

    @pl.when((i < nused_ref[0]) & ((i == 0) | (be_ref[i] != be_ref[jnp.maximum(i - 1, 0)])))
    def _():
        for c in range(W1_CHUNKS):
            w1_bf[c] = w1[c][0].astype(BF16)
        for c in range(W2_CHUNKS):
            w2_bf[c] = w2[c][0].astype(BF16)

    @pl.when(i < nused_ref[0])
    def _():
        x = x_ref[...].astype(BF16)
        out = b2_ref[0]
        for c in range(half):
            glu = jnp.dot(x, w1_bf[c], preferred_element_type=F32) + b1_ref[0, :, c * wide:(c + 1) * wide]
            lin = (jnp.dot(x, w1_bf[half + c], preferred_element_type=F32)
                   + b1_ref[0, :, D_FF + c * wide:D_FF + (c + 1) * wide])
            glu = jnp.minimum(glu, SWIGLU_LIMIT)
            lin = jnp.clip(lin, -SWIGLU_LIMIT, SWIGLU_LIMIT)
            act = glu * jax.nn.sigmoid(SWIGLU_ALPHA * glu) * (lin + 1.0)
            out = out + _dot(act, w2_bf[c])
        o_ref[...] = out

    @pl.when(i >= nused_ref[0])
    def _():
        o_ref[...] = jnp.zeros_like(o_ref)


def moe_experts(xs, block_expert, n_used, w1, b1, w2, b2):
    n_blocks = block_expert.shape[0]
    assert W1_CHUNKS // 2 == W2_CHUNKS
    c1 = 2 * D_FF // W1_CHUNKS
    c2 = D_FF // W2_CHUNKS
    w1_spec = lambda c: pl.BlockSpec((1, D_MODEL, c1), lambda i, be, nu: (be[i], 0, c))
    w2_spec = lambda c: pl.BlockSpec((1, c2, D_MODEL), lambda i, be, nu: (be[i], c, 0))
    grid_spec = pltpu.PrefetchScalarGridSpec(
        num_scalar_prefetch=2,
        grid=(n_blocks,),
        in_specs=[pl.BlockSpec((MOE_BLOCK, D_MODEL), lambda i, be, nu: (jnp.minimum(i, nu[0] - 1), 0))]
        + [w1_spec(c) for c in range(W1_CHUNKS)]
        + [pl.BlockSpec((1, 1, 2 * D_FF), lambda i, be, nu: (be[i], 0, 0))]
        + [w2_spec(c) for c in range(W2_CHUNKS)]
        + [pl.BlockSpec((1, 1, D_MODEL), lambda i, be, nu: (be[i], 0, 0))],
        out_specs=pl.BlockSpec((MOE_BLOCK, D_MODEL), lambda i, be, nu: (i, 0)),
        scratch_shapes=[pltpu.VMEM((W1_CHUNKS, D_MODEL, c1), BF16), pltpu.VMEM((W2_CHUNKS, c2, D_MODEL), BF16)],
    )
    return pl.pallas_call(
        _moe_kernel,
        out_shape=jax.ShapeDtypeStruct((n_blocks * MOE_BLOCK, D_MODEL), F32),
        grid_spec=grid_spec,
        compiler_params=_cparams("arbitrary"),
        name="moe_experts",
    )(block_expert, n_used, xs, *([w1] * W1_CHUNKS), b1.reshape(N_EXPERTS, 1, 2 * D_FF),
      *([w2] * W2_CHUNKS), b2.reshape(N_EXPERTS, 1, D_MODEL))


def _start_combine_gather(dest_ref, os_hbm, dst, sem):
    tm = dst.shape[0] // TOP_K
    for j in range(TOP_K):
        def issue(g, carry):
            base = pl.multiple_of(g * ROW_GROUP, ROW_GROUP)
            rows = dst.at[pl.ds(j * tm + base, ROW_GROUP)]
            for u in range(ROW_GROUP):
                pltpu.make_async_copy(os_hbm.at[pl.ds(dest_ref[j * tm + base + u], 1)], rows.at[pl.ds(u, 1)],
                                      sem).start()
            return carry

        lax.fori_loop(0, tm // ROW_GROUP, issue, 0)


def _combine_kernel(n_tiles, dest_ref, dest_next_ref, os_hbm, x2_ref, gates_ref, gfin_ref, y_ref, buf, sem):
    tm = x2_ref.shape[0]
    i = pl.program_id(0)
    slot = i % 2

    @pl.when(i == 0)
    def _():
        _start_combine_gather(dest_ref, os_hbm, buf.at[0], sem.at[0])

    if n_tiles > 1:
        @pl.when(i + 1 < n_tiles)
        def _():
            _start_combine_gather(dest_next_ref, os_hbm, buf.at[1 - slot], sem.at[1 - slot])

    pltpu.make_async_copy(os_hbm.at[pl.ds(0, TOP_K * tm)], buf.at[slot], sem.at[slot]).wait()
    y = x2_ref[...]
    gates = gates_ref[...]
    for j in range(TOP_K):
        y = y + gates[:, j:j + 1] * buf[slot, j * tm:(j + 1) * tm, :]
    y_ref[...] = _rms(y, gfin_ref[...])


def moe_combine(os_rows, dest, x2, gates, g_final):
    n = x2.shape[0]
    n_tiles, _, tm = dest.shape
    dest_spec = lambda step: pl.BlockSpec(
        (ROUTE_SUBLANES * tm,), lambda i: (jnp.minimum(i + step, n_tiles - 1),), memory_space=pltpu.SMEM)
    dest = dest.reshape(-1)
    return pl.pallas_call(
        functools.partial(_combine_kernel, n_tiles),
        out_shape=jax.ShapeDtypeStruct((n, D_MODEL), F32),
        grid=(n_tiles,),
        in_specs=[dest_spec(0), dest_spec(1),
                  pl.BlockSpec(memory_space=pl.ANY),
                  pl.BlockSpec((tm, D_MODEL), lambda i: (i, 0)),
                  pl.BlockSpec((tm, ROUTER_LANES), lambda i: (i, 0)),
                  _resident((1, D_MODEL))],
        out_specs=pl.BlockSpec((tm, D_MODEL), lambda i: (i, 0)),
        scratch_shapes=[pltpu.VMEM((2, TOP_K * tm, D_MODEL), F32), pltpu.SemaphoreType.DMA((2,))],
        compiler_params=_cparams_gather(),
        name="moe_combine",
    )(dest, dest, os_rows, x2, gates, g_final.reshape(1, D_MODEL))


def moe_ffn(groups, w):
    sizes = [g[0].shape[0] for g in groups]
    n = sum(sizes)
    tm = next(c for c in (256, 128, 64, 32, 16, 8) if all(s % c == 0 for s in sizes))
    dest, counts = moe_route(jnp.concatenate([g[2] for g in groups], axis=0), tm)
    n_blocks = -(-(n * TOP_K + N_EXPERTS * (MOE_BLOCK - 1)) // MOE_BLOCK)
    pends, block_expert, n_used = _block_tables(counts, n_blocks)
    bounds = [0]
    for s in sizes:
        bounds.append(bounds[-1] + s // tm)
    xs = moe_dispatch([g[1] for g in groups], bounds, dest, pends, n_blocks * MOE_BLOCK)
    os_rows = moe_experts(xs, block_expert, n_used, w["w_e1"], w["b_e1"], w["w_e2"], w["b_e2"])
    return [moe_combine(os_rows, dest[lo:hi], g[0], g[3], w["g_final"])
            for g, lo, hi in zip(groups, bounds[:-1], bounds[1:])]


def _mixers(x, prev0, s0_t, k_past, v_past, mk, mv, w, chunk, prep_tile, tq):
    b, t, _ = x.shape
    n = b * t
    xf = x.reshape(n, D_MODEL)
    q, k_sb, v_sb, rw, gate = in_proj(xf, w["g_mix"], w["w_in"])
    shift = norm_rows(x[:, -1], w["g_mix"])
    prep = rw_prep(rw.reshape(b, t, D_RW_COLS), prev0, w["rw"], chunk, prep_tile)
    o_rw, s_t = rw_chunks(prep, s0_t, w["gn_w"], w["gn_b"], chunk)
    seq = lambda a: a.reshape(b, t, D_MIX)
    o_sb = stick_breaking(seq(q), seq(k_sb), seq(v_sb), k_past, v_past, tq)
    x1, qx = merge_out(o_rw.reshape(n, D_MIX), o_sb.reshape(n, D_MIX), gate, xf,
                       w["wb0"], w["wb1"], w["w_out"], w["g_xattn"], w["w_xq"])
    x2, hf, idx, gates = xattn_router(qx.reshape(b, t, D_MODEL), x1.reshape(b, t, D_MODEL), mk, mv,
                                      w["w_xo"], w["g_ffn"], w["w_router"], w["b_router"])
    tok = (x2.reshape(n, D_MODEL), hf.reshape(n, D_MODEL), idx.reshape(n, ROUTER_LANES),
           gates.reshape(n, ROUTER_LANES))
    return tok, shift, s_t, k_sb, v_sb


def _state_to_t(s):
    b = s.shape[0]
    return s.transpose(0, 3, 1, 2).reshape(b, HEAD_DIM, D_MIX)


def _state_from_t(s_t):
    b = s_t.shape[0]
    return s_t.reshape(b, HEAD_DIM, N_HEADS, HEAD_DIM).transpose(0, 2, 3, 1)


def kernel(x_prompt, x_sample, state_rw_shift, state_rw_wkv, cache_sb_k, cache_sb_v, cache_mem_k, cache_mem_v, mem_prompt, g_mix, w_in, rw_mu, rw_w0, rw_w_up, rw_a0, rw_a_up, rw_g_up, rw_k_k, rw_k_a, rw_r_k, rw_gn_w, rw_gn_b, w_branch, w_out, g_xattn, g_mem, w_xq, w_mk, w_mv, w_xo, g_ffn, w_router, b_router, w_e1, b_e1, w_e2, b_e2, g_final):
    assert g_mix.shape[0] == 1, "single-layer trunk"
    row = lambda a: a.reshape(1, -1)
    w = dict(
        g_mix=g_mix[0], w_in=w_in[0].astype(BF16),
        rw=dict(mu=row(rw_mu[0]), w0=row(rw_w0[0]), a0=row(rw_a0[0]), k_k=row(rw_k_k[0]), k_a=row(rw_k_a[0]),
                r_k=row(rw_r_k[0]), w_up=rw_w_up[0].astype(BF16), a_up=rw_a_up[0].astype(BF16),
                g_up=rw_g_up[0].astype(BF16)),
        gn_w=row(rw_gn_w[0]), gn_b=row(rw_gn_b[0]),
        wb0=w_branch[0, 0].astype(BF16), wb1=w_branch[0, 1].astype(BF16), w_out=w_out[0].astype(BF16),
        g_xattn=g_xattn[0], w_xq=w_xq[0].astype(BF16), w_xo=w_xo[0].astype(BF16),
        g_ffn=g_ffn[0], w_router=w_router[0], b_router=b_router[0],
        w_e1=w_e1[0], b_e1=b_e1[0], w_e2=w_e2[0], b_e2=b_e2[0],
        g_final=g_final,
    )
    bp, t, _ = x_prompt.shape
    bs, ts, _ = x_sample.shape
    n_mem = mem_prompt.shape[1]

    mk_p, mv_p = mem_kv(mem_prompt.reshape(bp * n_mem, D_MODEL), g_mem[0],
                        w_mk[0].astype(BF16), w_mv[0].astype(BF16))
    mk_p = mk_p.reshape(bp, n_mem, D_MODEL)
    mv_p = mv_p.reshape(bp, n_mem, D_MODEL)
    tok_p, sh_p, st_p, k_p, v_p = _mixers(
        x_prompt, jnp.zeros((bp, 1, D_RW_COLS), F32), jnp.zeros((bp, HEAD_DIM, D_MIX), F32),
        None, None, mk_p, mv_p, w, chunk=64, prep_tile=256, tq=256)

    prev_s = plain_proj(state_rw_shift[0], w["w_in"][:, SB_COLS:SB_COLS + D_RW_COLS])
    past = cache_sb_k.shape[2]
    tok_s, sh_s, st_s, k_s, v_s = _mixers(
        x_sample, prev_s.reshape(bs, 1, D_RW_COLS), _state_to_t(state_rw_wkv[0]),
        cache_sb_k[0].reshape(bs, past, D_MIX), cache_sb_v[0].reshape(bs, past, D_MIX),
        cache_mem_k[0].reshape(bs, n_mem, D_MODEL), cache_mem_v[0].reshape(bs, n_mem, D_MODEL),
        w, chunk=ts, prep_tile=ts, tq=ts)

    y_p, y_s = moe_ffn([tok_p, tok_s], w)

    heads = lambda a, b_, t_: a.reshape(1, b_, t_, N_HEADS, HEAD_DIM)
    xh = lambda a: a.reshape(1, bp, n_mem, N_X_HEADS, X_HEAD_DIM)
    return (y_p.reshape(bp, t, D_MODEL), y_s.reshape(bs, ts, D_MODEL),
            sh_p[None], _state_from_t(st_p)[None], heads(k_p, bp, t), heads(v_p, bp, t), xh(mk_p), xh(mv_p),
            sh_s[None], _state_from_t(st_s)[None], heads(k_s, bs, ts), heads(v_s, bs, ts))
```

```python
import functools

import jax
import jax.numpy as jnp
from jax import lax
from jax.experimental import pallas as pl
from jax.experimental.pallas import tpu as pltpu

F32 = jnp.float32
BF16 = jnp.bfloat16

D_MODEL = 1024
HEAD_DIM = 64
N_HEADS = 8
D_MIX = N_HEADS * HEAD_DIM
LORA_W, LORA_A, LORA_G = 64, 64, 128
D_RW_COLS = 3 * D_MIX + LORA_W + LORA_A + LORA_G
N_X_HEADS = 4
X_HEAD_DIM = 256
N_EXPERTS = 32
TOP_K = 4
D_FF = 1024
SWIGLU_LIMIT = 7.0
SWIGLU_ALPHA = 1.702
MOE_BLOCK = 256
RMS_EPS = 1e-5
GN_EPS = 64e-5

ROW_TILE = 256
RW_GROUP = 4
RW_LANES = RW_GROUP * HEAD_DIM
RW_SEQS = 4
VMEM_LIMIT = 56 * 1024 * 1024


def _cparams(*sem):
    return pltpu.CompilerParams(dimension_semantics=sem, vmem_limit_bytes=VMEM_LIMIT)


def _cparams_gather():
    return pltpu.CompilerParams(dimension_semantics=("arbitrary",), vmem_limit_bytes=VMEM_LIMIT,
                                disable_bounds_checks=True)


def _resident(shape):
    nd = len(shape)
    return pl.BlockSpec(shape, lambda *_: (0,) * nd)


_NN = ((1,), (0,))
_NT = ((1,), (1,))
_TN = ((0,), (0,))


def _dot(a, b, dims=_NN):
    return lax.dot_general(a.astype(BF16), b.astype(BF16), (dims, ((), ())),
                           preferred_element_type=F32)


def _split(x, n):
    parts, rem = [], x
    for i in range(n):
        p = rem.astype(BF16)
        parts.append(p)
        if i + 1 < n:
            rem = rem - p.astype(F32)
    return parts


def _dot_exact_rhs(a, b_bf16, dims=_NN, n=3):
    out = None
    for p in _split(a, n):
        t = lax.dot_general(p, b_bf16, (dims, ((), ())), preferred_element_type=F32)
        out = t if out is None else out + t
    return out


def _dot_exact_lhs(a_bf16, b, dims=_NN, n=3):
    out = None
    for p in _split(b, n):
        t = lax.dot_general(a_bf16, p, (dims, ((), ())), preferred_element_type=F32)
        out = t if out is None else out + t
    return out


def _dot3(a, b, dims=_NN):
    ah, al = _split(a, 2)
    bh, bl = _split(b, 2)
    dn = (dims, ((), ()))
    return (lax.dot_general(ah, bh, dn, preferred_element_type=F32)
            + (lax.dot_general(ah, bl, dn, preferred_element_type=F32)
               + lax.dot_general(al, bh, dn, preferred_element_type=F32)))


def _rms(x, g):
    return x * lax.rsqrt(jnp.mean(x * x, axis=-1, keepdims=True) + RMS_EPS) * g


def _iota(shape, dim):
    return lax.broadcasted_iota(jnp.int32, shape, dim)


def _block_ones(n, width):
    return (_iota((n, n), 0) // width == _iota((n, n), 1) // width).astype(BF16)


def _bd_rows(x, nblk, width):
    blk = _iota(x.shape, 1) // width
    zero = jnp.zeros_like(x)
    return jnp.concatenate([jnp.where(blk == h, x, zero) for h in range(nblk)], axis=0)


def _norm_rows_kernel(x_ref, g_ref, o_ref):
    o_ref[...] = _rms(x_ref[...], g_ref[...])


def norm_rows(x, g):
    r, d = x.shape
    return pl.pallas_call(
        _norm_rows_kernel,
        out_shape=jax.ShapeDtypeStruct((r, d), F32),
        grid=(1,),
        in_specs=[_resident((r, d)), _resident((1, d))],
        out_specs=_resident((r, d)),
        name="norm_rows",
    )(x, g.reshape(1, d))


SB_COLS = 3 * D_MIX
GATE_COLS = 2 * D_MODEL
D_IN = SB_COLS + D_RW_COLS + GATE_COLS


def _in_proj_kernel(x_ref, g_ref, w_ref, q_ref, k_ref, v_ref, rw_ref, gate_ref, kh_ref, vh_ref):
    h = _rms(x_ref[...], g_ref[...]).astype(BF16)

    def proj(lo, hi):
        return jnp.dot(h, w_ref[:, lo:hi], preferred_element_type=F32)

    q_ref[...] = proj(0, D_MIX).astype(BF16)
    k = proj(D_MIX, 2 * D_MIX)
    v = proj(2 * D_MIX, SB_COLS)
    k_ref[...] = k
    v_ref[...] = v
    kh_ref[...] = k.reshape(kh_ref.shape)
    vh_ref[...] = v.reshape(vh_ref.shape)
    rw_ref[...] = proj(SB_COLS, SB_COLS + D_RW_COLS)
    gate_ref[...] = proj(SB_COLS + D_RW_COLS, D_IN).astype(BF16)


def in_proj(x, g, w_bf16):
    n = x.shape[0]
    tm = min(ROW_TILE, n)
    row = lambda width: pl.BlockSpec((tm, width), lambda i: (i, 0))
    heads = pl.BlockSpec((tm, N_HEADS, HEAD_DIM), lambda i: (i, 0, 0))
    return pl.pallas_call(
        _in_proj_kernel,
        out_shape=(jax.ShapeDtypeStruct((n, D_MIX), BF16),
                   jax.ShapeDtypeStruct((n, D_MIX), F32),
                   jax.ShapeDtypeStruct((n, D_MIX), F32),
                   jax.ShapeDtypeStruct((n, D_RW_COLS), F32),
                   jax.ShapeDtypeStruct((n, GATE_COLS), BF16),
                   jax.ShapeDtypeStruct((n, N_HEADS, HEAD_DIM), F32),
                   jax.ShapeDtypeStruct((n, N_HEADS, HEAD_DIM), F32)),
        grid=(n // tm,),
        in_specs=[row(D_MODEL), _resident((1, D_MODEL)), _resident((D_MODEL, D_IN))],
        out_specs=(row(D_MIX), row(D_MIX), row(D_MIX), row(D_RW_COLS), row(GATE_COLS), heads, heads),
        compiler_params=_cparams("parallel"),
        name="in_proj",
    )(x, g.reshape(1, D_MODEL), w_bf16)


def _plain_proj_kernel(x_ref, w_ref, o_ref):
    o_ref[...] = _dot(x_ref[...], w_ref[...])


def plain_proj(x, w_bf16):
    r, m = x.shape[0], w_bf16.shape[1]
    return pl.pallas_call(
        _plain_proj_kernel,
        out_shape=jax.ShapeDtypeStruct((r, m), F32),
        grid=(1,),
        in_specs=[_resident(x.shape), _resident(w_bf16.shape)],
        out_specs=_resident((r, m)),
        name="plain_proj",
    )(x, w_bf16)


def _rw_prep_kernel(chunk, rw_ref, prev0_ref, mu_ref, w0_ref, a0_ref, kk_ref, ka_ref, rk_ref,
                    wup_ref, aup_ref, gup_ref,
                    rt_ref, at_ref, bt_ref, kt_ref, bh_ref, kh_ref, v_ref, g_ref, bonus_ref, wc_ref,
                    carry_ref):
    tt = rw_ref.shape[1]

    @pl.when(pl.program_id(1) == 0)
    def _():
        carry_ref[...] = prev0_ref[0]

    p = rw_ref[0]
    prev = jnp.where(_iota(p.shape, 0) == 0, carry_ref[...], pltpu.roll(p, 1, 0))
    carry_ref[...] = p[tt - 1:tt]
    xs = p + mu_ref[...] * (prev - p)
    r = xs[:, 0:D_MIX]
    k = xs[:, D_MIX:2 * D_MIX]
    v = xs[:, 2 * D_MIX:3 * D_MIX]
    lo = 3 * D_MIX
    wd = xs[:, lo:lo + LORA_W]
    ad = xs[:, lo + LORA_W:lo + LORA_W + LORA_A]
    gd = xs[:, lo + LORA_W + LORA_A:D_RW_COLS]

    pre_w = w0_ref[...] + _dot(jnp.tanh(wd), wup_ref[...])
    w_log = -jax.nn.softplus(-pre_w) - 0.5
    logw = -jnp.exp(w_log)
    a = jax.nn.sigmoid(a0_ref[...] + _dot(ad, aup_ref[...]))
    g_ref[0] = _dot(jax.nn.sigmoid(gd), gup_ref[...])

    head_ones = _block_ones(D_MIX, HEAD_DIM)
    kk = k * kk_ref[...]
    ss = _dot_exact_rhs(kk * kk, head_ones, n=2)
    kk = kk * jnp.minimum(lax.rsqrt(ss), 1e12)
    k2 = k * (1.0 + (a - 1.0) * ka_ref[...])
    nb = -(kk * a)
    bonus_ref[0] = _dot_exact_rhs(r * k2 * rk_ref[...], head_ones, n=2) * v
    v_ref[0] = v.astype(BF16)

    ti, tj = _iota((tt, tt), 0), _iota((tt, tt), 1)
    same = ti // chunk == tj // chunk
    cum = _dot_exact_lhs((same & (tj <= ti)).astype(BF16), logw)
    tot = jnp.concatenate(
        [jnp.broadcast_to(cum[(c + 1) * chunk - 1:(c + 1) * chunk], (chunk, D_MIX)) for c in range(tt // chunk)],
        axis=0)
    e_in = jnp.exp(cum)
    e_out = jnp.exp(-cum)
    e_end = jnp.exp(tot - cum)
    rt_ref[0] = (r * e_in).astype(BF16)
    at_ref[0] = (kk * jnp.exp(cum - logw)).astype(BF16)
    bt_ref[0] = (nb * e_out).astype(BF16)
    kt_ref[0] = (k2 * e_out).astype(BF16)
    bh_ref[0] = (nb * e_end).astype(BF16)
    kh_ref[0] = (k2 * e_end).astype(BF16)
    etot = jnp.exp(tot)
    for c in range(tt // chunk):
        wc_ref[0, c] = etot[c * chunk:c * chunk + 1]


def rw_prep(rw, prev0, p, chunk, tt):
    b, t, _ = rw.shape
    seq = lambda width, dt: jax.ShapeDtypeStruct((b, t, width), dt)
    tile = lambda width: pl.BlockSpec((1, tt, width), lambda i, j: (i, j, 0))
    vec = lambda width: _resident((1, width))
    n_c = tt // chunk
    return pl.pallas_call(
        functools.partial(_rw_prep_kernel, chunk),
        out_shape=tuple(seq(D_MIX, BF16) for _ in range(7))
        + (seq(D_MIX, F32), seq(D_MIX, F32), jax.ShapeDtypeStruct((b, t // chunk, 1, D_MIX), F32)),
        grid=(b, t // tt),
        in_specs=[tile(D_RW_COLS), pl.BlockSpec((1, 1, D_RW_COLS), lambda i, j: (i, 0, 0)),
                  vec(D_RW_COLS), vec(D_MIX), vec(D_MIX), vec(D_MIX), vec(D_MIX), vec(D_MIX),
                  _resident((LORA_W, D_MIX)), _resident((LORA_A, D_MIX)), _resident((LORA_G, D_MIX))],
        out_specs=tuple(tile(D_MIX) for _ in range(9))
        + (pl.BlockSpec((1, n_c, 1, D_MIX), lambda i, j: (i, j, 0, 0)),),
        scratch_shapes=[pltpu.VMEM((1, D_RW_COLS), F32)],
        compiler_params=_cparams("parallel", "arbitrary"),
        name="rw_prep",
    )(rw, prev0, p["mu"], p["w0"], p["a0"], p["k_k"], p["k_a"], p["r_k"],
      p["w_up"], p["a_up"], p["g_up"])


def _unit_lower_inverse(a, chunk):
    shape = a.shape
    eye = (_iota(shape, 1) % chunk == _iota(shape, 0)).astype(F32)
    res = eye + a
    power = a.astype(BF16)
    for _ in range(chunk.bit_length() - 2):
        power = _dot(power, _bd_rows(power, RW_GROUP, chunk)).astype(BF16)
        yield
        res = res + _dot(res, _bd_rows(power, RW_GROUP, chunk))
    return res


def _interleave(chains):
    live = list(chains)
    while live:
        live = [c for c in live if next(c, StopIteration) is not StopIteration]


def _rw_chunk_kernel(chunk, rt_ref, at_ref, bt_ref, kt_ref, bh_ref, kh_ref, v_ref, g_ref, bonus_ref,
                     wc_ref, s0_ref, gnw_ref, gnb_ref, o_ref, s_out_ref, st_ref):
    c = pl.program_id(1)

    @pl.when(c == 0)
    def _():
        st_ref[...] = s0_ref[...]

    cat = (chunk, RW_GROUP * chunk)
    col_t = _iota(cat, 1) % chunk
    strict = col_t < _iota(cat, 0)
    incl = col_t <= _iota(cat, 0)
    head_ones = _block_ones(RW_LANES, HEAD_DIM)
    bd = lambda x: _bd_rows(x, RW_GROUP, HEAD_DIM)
    n_cat = RW_GROUP * chunk
    def chain(bi, grp):
        sl = slice(grp * RW_LANES, (grp + 1) * RW_LANES)
        rt, at, bt, kt = rt_ref[bi, :, sl], at_ref[bi, :, sl], bt_ref[bi, :, sl], kt_ref[bi, :, sl]
        bh, kh, v = bh_ref[bi, :, sl], kh_ref[bi, :, sl], v_ref[bi, :, sl]
        st = st_ref[bi, :, sl]

        gram = _dot(jnp.concatenate([at, rt], axis=0),
                    jnp.concatenate([bd(bt), bd(kt)], axis=0), _NT)
        yield
        zero = jnp.zeros(cat, F32)
        a_ab = jnp.where(strict, gram[:chunk, :n_cat], zero)
        a_ak = jnp.where(strict, gram[:chunk, n_cat:], zero)
        a_r = jnp.concatenate([jnp.where(incl, gram[chunk:, :n_cat], zero),
                               jnp.where(incl, gram[chunk:, n_cat:], zero)], axis=1).astype(BF16)
        v_bd = bd(v)
        x = _dot(a_ak, v_bd).astype(BF16)
        t_inv = yield from _unit_lower_inverse(a_ab, chunk)
        yield
        ua = _dot(t_inv, jnp.concatenate([bd(x), bd(at)], axis=1))
        st_bd = bd(st.astype(BF16))
        yield
        from_state = _dot(jnp.concatenate([ua[:, RW_LANES:].astype(BF16), rt], axis=0), st_bd)
        yield
        u = (ua[:, :RW_LANES] + from_state[:chunk]).astype(BF16)
        y = from_state[chunk:] + _dot(a_r, jnp.concatenate([bd(u), v_bd], axis=0))
        m = _dot(jnp.concatenate([bh, kh], axis=0), jnp.concatenate([u, v], axis=0), _TN)
        yield
        lane_head = _iota((HEAD_DIM, RW_LANES), 1) // HEAD_DIM
        fold = jnp.zeros((HEAD_DIM, RW_LANES), F32)
        for h in range(RW_GROUP):
            fold = fold + jnp.where(lane_head == h, m[h * HEAD_DIM:(h + 1) * HEAD_DIM], 0.0)
        diag = _iota((HEAD_DIM, RW_LANES), 1) % HEAD_DIM == _iota((HEAD_DIM, RW_LANES), 0)
        w_rows = jnp.where(diag, wc_ref[bi, 0, :, sl], 0.0)
        w_t = _dot_exact_rhs(w_rows, head_ones)
        st_ref[bi, :, sl] = st * w_t + fold

        mean = _dot_exact_rhs(y, head_ones, n=2) * (1.0 / HEAD_DIM)
        yield
        d = y - mean
        var = _dot_exact_rhs(d * d, head_ones, n=2) * (1.0 / HEAD_DIM)
        yield
        yn = d * lax.rsqrt(var + GN_EPS) * gnw_ref[:, sl] + gnb_ref[:, sl]
        o_ref[bi, :, sl] = ((yn + bonus_ref[bi, :, sl]) * g_ref[bi, :, sl]).astype(BF16)

    _interleave(chain(bi, grp) for bi in range(st_ref.shape[0]) for grp in range(N_HEADS // RW_GROUP))

    @pl.when(c == pl.num_programs(1) - 1)
    def _():
        s_out_ref[...] = st_ref[...]


def rw_chunks(prep, s0_t, gn_w, gn_b, chunk):
    rt, at, bt, kt, bh, kh, v, g, bonus, wc = prep
    b, t, _ = rt.shape
    nb = RW_SEQS if b % RW_SEQS == 0 else 1
    tile = pl.BlockSpec((nb, chunk, D_MIX), lambda i, j: (i, j, 0))
    state = pl.BlockSpec((nb, HEAD_DIM, D_MIX), lambda i, j: (i, 0, 0))
    return pl.pallas_call(
        functools.partial(_rw_chunk_kernel, chunk),
        out_shape=(jax.ShapeDtypeStruct((b, t, D_MIX), BF16),
                   jax.ShapeDtypeStruct((b, HEAD_DIM, D_MIX), F32)),
        grid=(b // nb, t // chunk),
        in_specs=[tile] * 9 + [pl.BlockSpec((nb, 1, 1, D_MIX), lambda i, j: (i, j, 0, 0)), state,
                               _resident((1, D_MIX)), _resident((1, D_MIX))],
        out_specs=(tile, state),
        scratch_shapes=[pltpu.VMEM((nb, HEAD_DIM, D_MIX), F32)],
        compiler_params=_cparams("parallel", "arbitrary"),
        name="rw_chunks",
    )(rt, at, bt, kt, bh, kh, v, g, bonus, wc, s0_t, gn_w, gn_b)


SB_PAIR = 2 * HEAD_DIM
SB_PAST_BLOCK = 256
SB_VISITS = 4


def _sb_kernel(tq, n_past, q_ref, kn_ref, vn_ref, *rest):
    if n_past:
        kp_ref, vp_ref, o_ref, acc_ref, c_ref = rest
    else:
        o_ref, acc_ref, c_ref = rest
    qi = pl.program_id(2)
    q2 = q_ref[0] * (HEAD_DIM ** -0.5)
    lane = _iota(q2.shape, 1)
    zero_q = jnp.zeros_like(q2)
    q_st = jnp.concatenate([jnp.where(lane < HEAD_DIM, q2, zero_q), jnp.where(lane >= HEAD_DIM, q2, zero_q)],
                           axis=0)
    acc_ref[...] = jnp.zeros_like(acc_ref)
    c_ref[...] = jnp.zeros_like(c_ref)

    def visit(k_blk, v_blk, masked):
        tk = k_blk.shape[0]
        later = (_iota((tk, tk), 0) > _iota((tk, tk), 1)).astype(BF16)
        z = _dot(q_st, k_blk, _NT)
        sp = jnp.maximum(z, 0.0) + jnp.log(1.0 + jnp.exp(-jnp.abs(z)))
        if masked:
            before = _iota((2 * tq, tk), 1) < _iota((2 * tq, tk), 0) % tq
            drop = jnp.where(before, sp, 0.0)
        else:
            drop = sp
        srev = _dot(drop, later)
        yield
        c = c_ref[...]
        att = jnp.exp((z - sp) - (c + srev))
        if masked:
            att = jnp.where(before, att, 0.0)
        acc_ref[...] += _dot(att, v_blk)
        c_ref[...] = c + (srev[:, 0:1] + drop[:, 0:1])

    def new_block(blk, masked=False):
        s = pl.multiple_of(blk * tq, tq)
        return visit(kn_ref[0, pl.ds(s, tq), :], vn_ref[0, pl.ds(s, tq), :], masked)

    _interleave([new_block(qi, masked=True)])

    def earlier_group(i, carry):
        _interleave(new_block(qi - 1 - u - SB_VISITS * i) for u in range(SB_VISITS))
        return carry

    lax.fori_loop(0, qi // SB_VISITS, earlier_group, 0)
    rest = qi % SB_VISITS

    @pl.when(rest >= 2)
    def _():
        _interleave([new_block(rest - 1), new_block(rest - 2)])

    @pl.when(rest % 2 == 1)
    def _():
        _interleave([new_block(0)])

    if n_past:
        def past_block(blk):
            s = pl.multiple_of(blk * SB_PAST_BLOCK, SB_PAST_BLOCK)
            return visit(kp_ref[0, pl.ds(s, SB_PAST_BLOCK), :], vp_ref[0, pl.ds(s, SB_PAST_BLOCK), :], False)

        def past_group(i, carry):
            _interleave(past_block(n_past - 1 - u - SB_VISITS * i) for u in range(SB_VISITS))
            return carry

        lax.fori_loop(0, n_past // SB_VISITS, past_group, 0)
        _interleave(past_block(blk) for blk in reversed(range(n_past % SB_VISITS)))

    o_ref[0] = jnp.where(lane < HEAD_DIM, acc_ref[:tq], acc_ref[tq:]).astype(BF16)


def stick_breaking(q, k_new, v_new, k_past, v_past, tq):
    b, t, _ = q.shape
    n_pairs = D_MIX // SB_PAIR
    qspec = pl.BlockSpec((1, tq, SB_PAIR), lambda i, p, j: (i, j, p))
    seq = lambda length: pl.BlockSpec((1, length, SB_PAIR), lambda i, p, j: (i, 0, p))
    args, specs, n_past = [q, k_new, v_new], [qspec, seq(t), seq(t)], 0
    if k_past is not None:
        past_len = k_past.shape[1]
        n_past = past_len // SB_PAST_BLOCK
        args += [k_past, v_past]
        specs += [seq(past_len), seq(past_len)]
    return pl.pallas_call(
        functools.partial(_sb_kernel, tq, n_past),
        out_shape=jax.ShapeDtypeStruct((b, t, D_MIX), BF16),
        grid=(b, n_pairs, t // tq),
        in_specs=specs,
        out_specs=qspec,
        scratch_shapes=[pltpu.VMEM((2 * tq, SB_PAIR), F32), pltpu.VMEM((2 * tq, 1), F32)],
        compiler_params=_cparams("parallel", "parallel", "arbitrary"),
        name="stick_breaking",
    )(*args)


def _merge_kernel(orw_ref, osb_ref, gate_ref, x_ref, wb0_ref, wb1_ref, wout_ref, gx_ref, wxq_ref,
                  x1_ref, qx_ref):
    g = jax.nn.sigmoid(gate_ref[...].astype(F32))
    mixed = (g[:, :D_MODEL] * jnp.dot(orw_ref[...], wb0_ref[...], preferred_element_type=F32)
             + g[:, D_MODEL:] * jnp.dot(osb_ref[...], wb1_ref[...], preferred_element_type=F32))
    x1 = x_ref[...] + _dot(mixed, wout_ref[...])
    x1_ref[...] = x1
    qx_ref[...] = _dot(_rms(x1, gx_ref[...]), wxq_ref[...]).astype(BF16)


def merge_out(o_rw, o_sb, gate, x, wb0, wb1, wout, gx, wxq):
    n = x.shape[0]
    tm = min(ROW_TILE, n)
    row = lambda width: pl.BlockSpec((tm, width), lambda i: (i, 0))
    return pl.pallas_call(
        _merge_kernel,
        out_shape=(jax.ShapeDtypeStruct((n, D_MODEL), F32), jax.ShapeDtypeStruct((n, D_MODEL), BF16)),
        grid=(n // tm,),
        in_specs=[row(D_MIX), row(D_MIX), row(GATE_COLS), row(D_MODEL),
                  _resident((D_MIX, D_MODEL)), _resident((D_MIX, D_MODEL)), _resident((D_MODEL, D_MODEL)),
                  _resident((1, D_MODEL)), _resident((D_MODEL, D_MODEL))],
        out_specs=(row(D_MODEL), row(D_MODEL)),
        compiler_params=_cparams("parallel"),
        name="merge_out",
    )(o_rw, o_sb, gate, x, wb0, wb1, wout, gx.reshape(1, D_MODEL), wxq)


def _mem_kv_kernel(m_ref, g_ref, wk_ref, wv_ref, k_ref, v_ref):
    h = _rms(m_ref[...], g_ref[...]).astype(BF16)
    k_ref[...] = jnp.dot(h, wk_ref[...], preferred_element_type=F32)
    v_ref[...] = jnp.dot(h, wv_ref[...], preferred_element_type=F32)


def mem_kv(mem, g, wk, wv):
    n = mem.shape[0]
    tm = min(ROW_TILE, n)
    row = pl.BlockSpec((tm, D_MODEL), lambda i: (i, 0))
    return pl.pallas_call(
        _mem_kv_kernel,
        out_shape=(jax.ShapeDtypeStruct((n, D_MODEL), F32),) * 2,
        grid=(n // tm,),
        in_specs=[row, _resident((1, D_MODEL)), _resident((D_MODEL, D_MODEL)), _resident((D_MODEL, D_MODEL))],
        out_specs=(row, row),
        compiler_params=_cparams("parallel"),
        name="mem_kv",
    )(mem, g.reshape(1, D_MODEL), wk, wv)


ROUTER_LANES = 128


def _xattn_kernel(q_ref, x1_ref, mk_ref, mv_ref, wxo_ref, gf_ref, wr_ref, br_ref,
                  x2_ref, hf_ref, idx_ref, gates_ref, counts_ref):
    q = q_ref[0]
    heads = [None] * N_X_HEADS

    def head(h):
        sl = slice(h * X_HEAD_DIM, (h + 1) * X_HEAD_DIM)
        s = _dot(q[:, sl], mk_ref[0, :, sl], _NT) * (X_HEAD_DIM ** -0.5)
        yield
        e = jnp.exp(s - jnp.max(s, axis=-1, keepdims=True))
        p = e * (1.0 / jnp.sum(e, axis=-1, keepdims=True))
        heads[h] = _dot(p, mv_ref[0, :, sl])

    _interleave(head(h) for h in range(N_X_HEADS))
    x2 = x1_ref[0] + _dot(jnp.concatenate(heads, axis=1), wxo_ref[...])
    x2_ref[0] = x2
    hf = _rms(x2, gf_ref[...])
    hf_ref[0] = hf
    logits = _dot3(hf, wr_ref[...]) + br_ref[...]
    lane = _iota(logits.shape, 1)
    vals, idx_out = [], jnp.zeros(logits.shape, jnp.int32)
    chosen = jnp.zeros(logits.shape, F32)
    for j in range(TOP_K):
        m = jnp.max(logits, axis=-1, keepdims=True)
        pick = jnp.min(jnp.where(logits == m, lane, ROUTER_LANES), axis=-1, keepdims=True)
        vals.append(m)
        idx_out = jnp.where(lane == j, pick, idx_out)
        chosen = chosen + (lane == pick).astype(F32)
        logits = jnp.where(lane == pick, -jnp.inf, logits)

    @pl.when((pl.program_id(0) == 0) & (pl.program_id(1) == 0))
    def _():
        counts_ref[...] = jnp.zeros_like(counts_ref)

    counts_ref[...] += jnp.broadcast_to(jnp.sum(chosen, axis=0, keepdims=True), counts_ref.shape)
    exps = [jnp.exp(v - vals[0]) for v in vals]
    denom = exps[0] + exps[1] + exps[2] + exps[3]
    gates = jnp.zeros(logits.shape, F32)
    for j in range(TOP_K):
        gates = jnp.where(lane == j, exps[j] / denom, gates)
    idx_ref[0] = idx_out
    gates_ref[0] = gates


def xattn_router(qx, x1, mk, mv, wxo, g_ffn, w_router, b_router):
    b, t, _ = x1.shape
    tm = min(ROW_TILE, t)
    n_mem = mk.shape[1]
    wr = jnp.zeros((D_MODEL, ROUTER_LANES), F32).at[:, :N_EXPERTS].set(w_router)
    br = jnp.full((1, ROUTER_LANES), -jnp.inf, F32).at[0, :N_EXPERTS].set(b_router)
    tile = pl.BlockSpec((1, tm, D_MODEL), lambda i, j: (i, j, 0))
    mem = pl.BlockSpec((1, n_mem, D_MODEL), lambda i, j: (i, 0, 0))
    small = pl.BlockSpec((1, tm, ROUTER_LANES), lambda i, j: (i, j, 0))
    return pl.pallas_call(
        _xattn_kernel,
        out_shape=(jax.ShapeDtypeStruct((b, t, D_MODEL), F32), jax.ShapeDtypeStruct((b, t, D_MODEL), F32),
                   jax.ShapeDtypeStruct((b, t, ROUTER_LANES), jnp.int32),
                   jax.ShapeDtypeStruct((b, t, ROUTER_LANES), F32),
                   jax.ShapeDtypeStruct((ROUTE_SUBLANES, ROUTER_LANES), F32)),
        grid=(b, t // tm),
        in_specs=[tile, tile, mem, mem, _resident((D_MODEL, D_MODEL)), _resident((1, D_MODEL)),
                  _resident((D_MODEL, ROUTER_LANES)), _resident((1, ROUTER_LANES))],
        out_specs=(tile, tile, small, small, _resident((ROUTE_SUBLANES, ROUTER_LANES))),
        compiler_params=_cparams("arbitrary", "arbitrary"),
        name="xattn_router",
    )(qx, x1, mk, mv, wxo, g_ffn.reshape(1, D_MODEL), wr, br)


ROW_GROUP = 8
ROUTE_SUBLANES = 8


def _route_kernel(idx_ref, total_ref, dest_ref, carry_ref, start_ref):
    tm = idx_ref.shape[0]
    idx = idx_ref[...]
    lane = _iota((tm, ROUTER_LANES), 1)
    picks = [idx[:, j:j + 1] for j in range(TOP_K)]
    onehot = jnp.zeros((tm, ROUTER_LANES), F32)
    for pick in picks:
        onehot = onehot + (lane == pick).astype(F32)

    @pl.when(pl.program_id(0) == 0)
    def _():
        carry_ref[...] = jnp.zeros_like(carry_ref)
        padded = jnp.floor((total_ref[...] + (MOE_BLOCK - 1.0)) * (1.0 / MOE_BLOCK)) * MOE_BLOCK
        earlier = (_iota((ROUTER_LANES, ROUTER_LANES), 0) < _iota((ROUTER_LANES, ROUTER_LANES), 1))
        start_ref[...] = _dot_exact_rhs(padded, earlier.astype(BF16))

    below = (_iota((tm, tm), 1) < _iota((tm, tm), 0)).astype(BF16)
    row = _dot(below, onehot) + (carry_ref[0:1, :] + start_ref[0:1, :])
    dest = jnp.zeros((tm, ROUTER_LANES), F32)
    for j, pick in enumerate(picks):
        mine = jnp.sum(jnp.where(lane == pick, row, 0.0), axis=1, keepdims=True)
        dest = jnp.where(lane == j, mine, dest)
    dest_ref[0] = dest.T[:ROUTE_SUBLANES].astype(jnp.int32)
    carry_ref[...] += jnp.broadcast_to(jnp.sum(onehot, axis=0, keepdims=True), carry_ref.shape)


def moe_route(idx, total, tm):
    n = idx.shape[0]
    n_tiles = n // tm
    return pl.pallas_call(
        _route_kernel,
        out_shape=jax.ShapeDtypeStruct((n_tiles, ROUTE_SUBLANES, tm), jnp.int32),
        grid=(n_tiles,),
        in_specs=[pl.BlockSpec((tm, ROUTER_LANES), lambda t: (t, 0)), _resident((ROUTE_SUBLANES, ROUTER_LANES))],
        out_specs=pl.BlockSpec((1, ROUTE_SUBLANES, tm), lambda t: (t, 0, 0)),
        scratch_shapes=[pltpu.VMEM((ROUTE_SUBLANES, ROUTER_LANES), F32),
                        pltpu.VMEM((ROUTE_SUBLANES, ROUTER_LANES), F32)],
        compiler_params=_cparams("arbitrary"),
        name="moe_route",
    )(idx, total)


def _block_tables(counts, n_blocks):
    cnt = counts[0, :N_EXPERTS].astype(jnp.int32)
    padded = (cnt + MOE_BLOCK - 1) // MOE_BLOCK * MOE_BLOCK
    pends = jnp.cumsum(padded).astype(jnp.int32)
    block_start = jnp.arange(n_blocks, dtype=jnp.int32) * MOE_BLOCK
    block_expert = jnp.minimum(jnp.sum((pends[None, :] <= block_start[:, None]).astype(jnp.int32), axis=1),
                               N_EXPERTS - 1)
    n_used = (pends[-1:] // MOE_BLOCK).astype(jnp.int32)
    return pends, block_expert, n_used


def _dispatch_kernel(bounds, pends_ref, dest_ref, *rest):
    hf_refs = rest[:len(bounds) - 1]
    xs_hbm, zbuf, sem = rest[len(bounds) - 1:]
    i = pl.program_id(0)

    @pl.when(i == 0)
    def _():
        zbuf[...] = jnp.zeros_like(zbuf)
        n_blocks = xs_hbm.shape[0] // MOE_BLOCK
        n_used = pends_ref[N_EXPERTS - 1] // MOE_BLOCK
        for wait in (False, True):
            def unused_block(blk, carry):
                cp = pltpu.make_async_copy(
                    zbuf, xs_hbm.at[pl.ds(pl.multiple_of(blk * MOE_BLOCK, MOE_BLOCK), MOE_BLOCK)], sem)
                cp.wait() if wait else cp.start()
                return carry

            lax.fori_loop(n_used, n_blocks, unused_block, 0)
            for e in range(N_EXPERTS):
                end = pends_ref[e]
                begin = pends_ref[e - 1] if e else 0

                @pl.when(end > begin)
                def _():
                    last = pl.multiple_of(end - MOE_BLOCK, MOE_BLOCK)
                    cp = pltpu.make_async_copy(zbuf, xs_hbm.at[pl.ds(last, MOE_BLOCK)], sem)
                    cp.wait() if wait else cp.start()

    for hf_ref, lo, hi in zip(hf_refs, bounds[:-1], bounds[1:]):
        @pl.when((i >= lo) & (i < hi))
        def _():
            tm = hf_ref.shape[0]
            for j in range(TOP_K):
                def issue(g, carry):
                    base = pl.multiple_of(g * ROW_GROUP, ROW_GROUP)
                    rows = hf_ref.at[pl.ds(base, ROW_GROUP)]
                    for u in range(ROW_GROUP):
                        pltpu.make_async_copy(rows.at[pl.ds(u, 1)], xs_hbm.at[pl.ds(dest_ref[j * tm + base + u], 1)],
                                              sem).start()
                    return carry

                lax.fori_loop(0, tm // ROW_GROUP, issue, 0)
            for j in range(TOP_K):
                pltpu.make_async_copy(hf_ref, xs_hbm.at[pl.ds(0, tm)], sem).wait()


def moe_dispatch(hfs, bounds, dest, pends, n_rows):
    n_tiles, _, tm = dest.shape
    hf_spec = lambda lo, hi: pl.BlockSpec((tm, D_MODEL), lambda i, pe: (jnp.clip(i - lo, 0, hi - lo - 1), 0))
    grid_spec = pltpu.PrefetchScalarGridSpec(
        num_scalar_prefetch=1,
        grid=(n_tiles,),
        in_specs=[pl.BlockSpec((ROUTE_SUBLANES * tm,), lambda i, pe: (i,), memory_space=pltpu.SMEM)]
        + [hf_spec(lo, hi) for lo, hi in zip(bounds[:-1], bounds[1:])],
        out_specs=pl.BlockSpec(memory_space=pl.ANY),
        scratch_shapes=[pltpu.VMEM((MOE_BLOCK, D_MODEL), F32), pltpu.SemaphoreType.DMA(())],
    )
    return pl.pallas_call(
        functools.partial(_dispatch_kernel, tuple(bounds)),
        out_shape=jax.ShapeDtypeStruct((n_rows, D_MODEL), F32),
        grid_spec=grid_spec,
        compiler_params=_cparams_gather(),
        name="moe_dispatch",
    )(pends, dest.reshape(-1), *hfs)


W1_CHUNKS = 4
W2_CHUNKS = 2


def _moe_kernel(be_ref, nused_ref, x_ref, *rest):
    w1 = rest[:W1_CHUNKS]
    b1_ref = rest[W1_CHUNKS]
    w2 = rest[W1_CHUNKS + 1:W1_CHUNKS + 1 + W2_CHUNKS]
    b2_ref, o_ref, w1_bf, w2_bf = rest[W1_CHUNKS + 1 + W2_CHUNKS:]
    i = pl.program_id(0)
    half = W1_CHUNKS // 2
    wide = 2 * D_FF // W1_CHUNKS

    @pl.when((i < nused_ref[0]) & ((i == 0) | (be_ref[i] != be_ref[jnp.maximum(i - 1, 0)])))
    def _():
        for c in range(W1_CHUNKS):
            w1_bf[c] = w1[c][0].astype(BF16)
        for c in range(W2_CHUNKS):
            w2_bf[c] = w2[c][0].astype(BF16)

    @pl.when(i < nused_ref[0])
    def _():
        x = x_ref[...].astype(BF16)
        parts = [None] * half

        def ff_slice(c):
            glu = jnp.dot(x, w1_bf[c], preferred_element_type=F32) + b1_ref[0, :, c * wide:(c + 1) * wide]
            lin = (jnp.dot(x, w1_bf[half + c], preferred_element_type=F32)
                   + b1_ref[0, :, D_FF + c * wide:D_FF + (c + 1) * wide])
            yield
            glu = jnp.minimum(glu, SWIGLU_LIMIT)
            lin = jnp.clip(lin, -SWIGLU_LIMIT, SWIGLU_LIMIT)
            act = glu * jax.nn.sigmoid(SWIGLU_ALPHA * glu) * (lin + 1.0)
            parts[c] = _dot(act, w2_bf[c])

        _interleave(ff_slice(c) for c in range(half))
        o_ref[...] = b2_ref[0] + sum(parts[1:], parts[0])

    @pl.when(i >= nused_ref[0])
    def _():
        o_ref[...] = jnp.zeros_like(o_ref)


def moe_experts(xs, block_expert, n_used, w1, b1, w2, b2):
    n_blocks = block_expert.shape[0]
    assert W1_CHUNKS // 2 == W2_CHUNKS
    c1 = 2 * D_FF // W1_CHUNKS
    c2 = D_FF // W2_CHUNKS
    w1_spec = lambda c: pl.BlockSpec((1, D_MODEL, c1), lambda i, be, nu: (be[i], 0, c))
    w2_spec = lambda c: pl.BlockSpec((1, c2, D_MODEL), lambda i, be, nu: (be[i], c, 0))
    grid_spec = pltpu.PrefetchScalarGridSpec(
        num_scalar_prefetch=2,
        grid=(n_blocks,),
        in_specs=[pl.BlockSpec((MOE_BLOCK, D_MODEL), lambda i, be, nu: (jnp.minimum(i, nu[0] - 1), 0))]
        + [w1_spec(c) for c in range(W1_CHUNKS)]
        + [pl.BlockSpec((1, 1, 2 * D_FF), lambda i, be, nu: (be[i], 0, 0))]
        + [w2_spec(c) for c in range(W2_CHUNKS)]
        + [pl.BlockSpec((1, 1, D_MODEL), lambda i, be, nu: (be[i], 0, 0))],
        out_specs=pl.BlockSpec((MOE_BLOCK, D_MODEL), lambda i, be, nu: (i, 0)),
        scratch_shapes=[pltpu.VMEM((W1_CHUNKS, D_MODEL, c1), BF16), pltpu.VMEM((W2_CHUNKS, c2, D_MODEL), BF16)],
    )
    return pl.pallas_call(
        _moe_kernel,
        out_shape=jax.ShapeDtypeStruct((n_blocks * MOE_BLOCK, D_MODEL), F32),
        grid_spec=grid_spec,
        compiler_params=_cparams("arbitrary"),
        name="moe_experts",
    )(block_expert, n_used, xs, *([w1] * W1_CHUNKS), b1.reshape(N_EXPERTS, 1, 2 * D_FF),
      *([w2] * W2_CHUNKS), b2.reshape(N_EXPERTS, 1, D_MODEL))


def _start_combine_gather(dest_ref, os_hbm, dst, sem):
    tm = dst.shape[0] // TOP_K
    for j in range(TOP_K):
        def issue(g, carry):
            base = pl.multiple_of(g * ROW_GROUP, ROW_GROUP)
            rows = dst.at[pl.ds(j * tm + base, ROW_GROUP)]
            for u in range(ROW_GROUP):
                pltpu.make_async_copy(os_hbm.at[pl.ds(dest_ref[j * tm + base + u], 1)], rows.at[pl.ds(u, 1)],
                                      sem).start()
            return carry

        lax.fori_loop(0, tm // ROW_GROUP, issue, 0)


def _combine_kernel(n_tiles, dest_ref, dest_next_ref, os_hbm, x2_ref, gates_ref, gfin_ref, y_ref, buf, sem):
    tm = x2_ref.shape[0]
    i = pl.program_id(0)
    slot = i % 2

    @pl.when(i == 0)
    def _():
        _start_combine_gather(dest_ref, os_hbm, buf.at[0], sem.at[0])

    if n_tiles > 1:
        @pl.when(i + 1 < n_tiles)
        def _():
            _start_combine_gather(dest_next_ref, os_hbm, buf.at[1 - slot], sem.at[1 - slot])

    pltpu.make_async_copy(os_hbm.at[pl.ds(0, TOP_K * tm)], buf.at[slot], sem.at[slot]).wait()
    y = x2_ref[...]
    gates = gates_ref[...]
    for j in range(TOP_K):
        y = y + gates[:, j:j + 1] * buf[slot, j * tm:(j + 1) * tm, :]
    y_ref[...] = _rms(y, gfin_ref[...])


def moe_combine(os_rows, dest, x2, gates, g_final):
    n = x2.shape[0]
    n_tiles, _, tm = dest.shape
    dest_spec = lambda step: pl.BlockSpec(
        (ROUTE_SUBLANES * tm,), lambda i: (jnp.minimum(i + step, n_tiles - 1),), memory_space=pltpu.SMEM)
    dest = dest.reshape(-1)
    return pl.pallas_call(
        functools.partial(_combine_kernel, n_tiles),
        out_shape=jax.ShapeDtypeStruct((n, D_MODEL), F32),
        grid=(n_tiles,),
        in_specs=[dest_spec(0), dest_spec(1),
                  pl.BlockSpec(memory_space=pl.ANY),
                  pl.BlockSpec((tm, D_MODEL), lambda i: (i, 0)),
                  pl.BlockSpec((tm, ROUTER_LANES), lambda i: (i, 0)),
                  _resident((1, D_MODEL))],
        out_specs=pl.BlockSpec((tm, D_MODEL), lambda i: (i, 0)),
        scratch_shapes=[pltpu.VMEM((2, TOP_K * tm, D_MODEL), F32), pltpu.SemaphoreType.DMA((2,))],
        compiler_params=_cparams_gather(),
        name="moe_combine",
    )(dest, dest, os_rows, x2, gates, g_final.reshape(1, D_MODEL))


def moe_ffn(groups, w):
    sizes = [g[0].shape[0] for g in groups]
    n = sum(sizes)
    tm = next(c for c in (256, 128, 64, 32, 16, 8) if all(s % c == 0 for s in sizes))
    counts = sum(g[4] for g in groups[1:]) + groups[0][4]
    dest = moe_route(jnp.concatenate([g[2] for g in groups], axis=0), counts, tm)
    n_blocks = -(-(n * TOP_K + N_EXPERTS * (MOE_BLOCK - 1)) // MOE_BLOCK)
    pends, block_expert, n_used = _block_tables(counts, n_blocks)
    bounds = [0]
    for s in sizes:
        bounds.append(bounds[-1] + s // tm)
    xs = moe_dispatch([g[1] for g in groups], bounds, dest, pends, n_blocks * MOE_BLOCK)
    os_rows = moe_experts(xs, block_expert, n_used, w["w_e1"], w["b_e1"], w["w_e2"], w["b_e2"])
    return [moe_combine(os_rows, dest[lo:hi], g[0], g[3], w["g_final"])
            for g, lo, hi in zip(groups, bounds[:-1], bounds[1:])]


def _mixers(x, prev0, s0_t, k_past, v_past, mk, mv, w, chunk, prep_tile, tq):
    b, t, _ = x.shape
    n = b * t
    xf = x.reshape(n, D_MODEL)
    q, k_sb, v_sb, rw, gate, k_heads, v_heads = in_proj(xf, w["g_mix"], w["w_in"])
    shift = norm_rows(x[:, -1], w["g_mix"])
    prep = rw_prep(rw.reshape(b, t, D_RW_COLS), prev0, w["rw"], chunk, prep_tile)
    o_rw, s_t = rw_chunks(prep, s0_t, w["gn_w"], w["gn_b"], chunk)
    seq = lambda a: a.reshape(b, t, D_MIX)
    o_sb = stick_breaking(seq(q), seq(k_sb), seq(v_sb), k_past, v_past, tq)
    x1, qx = merge_out(o_rw.reshape(n, D_MIX), o_sb.reshape(n, D_MIX), gate, xf,
                       w["wb0"], w["wb1"], w["w_out"], w["g_xattn"], w["w_xq"])
    x2, hf, idx, gates, counts = xattn_router(qx.reshape(b, t, D_MODEL), x1.reshape(b, t, D_MODEL), mk, mv,
                                              w["w_xo"], w["g_ffn"], w["w_router"], w["b_router"])
    tok = (x2.reshape(n, D_MODEL), hf.reshape(n, D_MODEL), idx.reshape(n, ROUTER_LANES),
           gates.reshape(n, ROUTER_LANES), counts)
    return tok, shift, s_t, k_heads, v_heads


def _state_to_t(s):
    b = s.shape[0]
    return s.transpose(0, 3, 1, 2).reshape(b, HEAD_DIM, D_MIX)


def _state_from_t(s_t):
    b = s_t.shape[0]
    return s_t.reshape(b, HEAD_DIM, N_HEADS, HEAD_DIM).transpose(0, 2, 3, 1)


def kernel(x_prompt, x_sample, state_rw_shift, state_rw_wkv, cache_sb_k, cache_sb_v, cache_mem_k, cache_mem_v, mem_prompt, g_mix, w_in, rw_mu, rw_w0, rw_w_up, rw_a0, rw_a_up, rw_g_up, rw_k_k, rw_k_a, rw_r_k, rw_gn_w, rw_gn_b, w_branch, w_out, g_xattn, g_mem, w_xq, w_mk, w_mv, w_xo, g_ffn, w_router, b_router, w_e1, b_e1, w_e2, b_e2, g_final):
    assert g_mix.shape[0] == 1, "single-layer trunk"
    row = lambda a: a.reshape(1, -1)
    w = dict(
        g_mix=g_mix[0], w_in=w_in[0].astype(BF16),
        rw=dict(mu=row(rw_mu[0]), w0=row(rw_w0[0]), a0=row(rw_a0[0]), k_k=row(rw_k_k[0]), k_a=row(rw_k_a[0]),
                r_k=row(rw_r_k[0]), w_up=rw_w_up[0].astype(BF16), a_up=rw_a_up[0].astype(BF16),
                g_up=rw_g_up[0].astype(BF16)),
        gn_w=row(rw_gn_w[0]), gn_b=row(rw_gn_b[0]),
        wb0=w_branch[0, 0].astype(BF16), wb1=w_branch[0, 1].astype(BF16), w_out=w_out[0].astype(BF16),
        g_xattn=g_xattn[0], w_xq=w_xq[0].astype(BF16), w_xo=w_xo[0].astype(BF16),
        g_ffn=g_ffn[0], w_router=w_router[0], b_router=b_router[0],
        w_e1=w_e1[0], b_e1=b_e1[0], w_e2=w_e2[0], b_e2=b_e2[0],
        g_final=g_final,
    )
    bp, t, _ = x_prompt.shape
    bs, ts, _ = x_sample.shape
    n_mem = mem_prompt.shape[1]

    mk_p, mv_p = mem_kv(mem_prompt.reshape(bp * n_mem, D_MODEL), g_mem[0],
                        w_mk[0].astype(BF16), w_mv[0].astype(BF16))
    mk_p = mk_p.reshape(bp, n_mem, D_MODEL)
    mv_p = mv_p.reshape(bp, n_mem, D_MODEL)
    tok_p, sh_p, st_p, k_p, v_p = _mixers(
        x_prompt, jnp.zeros((bp, 1, D_RW_COLS), F32), jnp.zeros((bp, HEAD_DIM, D_MIX), F32),
        None, None, mk_p, mv_p, w, chunk=64, prep_tile=256, tq=256)

    prev_s = plain_proj(state_rw_shift[0], w["w_in"][:, SB_COLS:SB_COLS + D_RW_COLS])
    past = cache_sb_k.shape[2]
    tok_s, sh_s, st_s, k_s, v_s = _mixers(
        x_sample, prev_s.reshape(bs, 1, D_RW_COLS), _state_to_t(state_rw_wkv[0]),
        cache_sb_k[0].reshape(bs, past, D_MIX), cache_sb_v[0].reshape(bs, past, D_MIX),
        cache_mem_k[0].reshape(bs, n_mem, D_MODEL), cache_mem_v[0].reshape(bs, n_mem, D_MODEL),
        w, chunk=ts, prep_tile=ts, tq=ts)

    y_p, y_s = moe_ffn([tok_p, tok_s], w)

    heads = lambda a, b_, t_: a.reshape(1, b_, t_, N_HEADS, HEAD_DIM)
    xh = lambda a: a.reshape(1, bp, n_mem, N_X_HEADS, X_HEAD_DIM)
    return (y_p.reshape(bp, t, D_MODEL), y_s.reshape(bs, ts, D_MODEL),
            sh_p[None], _state_from_t(st_p)[None], heads(k_p, bp, t), heads(v_p, bp, t), xh(mk_p), xh(mv_p),
            sh_s[None], _state_from_t(st_s)[None], heads(k_s, bs, ts), heads(v_s, bs, ts))
```

```python
import functools

import jax
import jax.numpy as jnp
from jax import lax
from jax.experimental import pallas as pl
from jax.experimental.pallas import tpu as pltpu

F32 = jnp.float32
BF16 = jnp.bfloat16

D_MODEL = 1024
HEAD_DIM = 64
N_HEADS = 8
D_MIX = N_HEADS * HEAD_DIM
LORA_W, LORA_A, LORA_G = 64, 64, 128
D_RW_COLS = 3 * D_MIX + LORA_W + LORA_A + LORA_G
N_X_HEADS = 4
X_HEAD_DIM = 256
N_EXPERTS = 32
TOP_K = 4
D_FF = 1024
SWIGLU_LIMIT = 7.0
SWIGLU_ALPHA = 1.702
MOE_BLOCK = 512
RMS_EPS = 1e-5
GN_EPS = 64e-5

ROW_TILE = 256
RW_GROUP = 4
RW_LANES = RW_GROUP * HEAD_DIM
RW_SEQS = 4
VMEM_LIMIT = 56 * 1024 * 1024


def _cparams(*sem):
    return pltpu.CompilerParams(dimension_semantics=sem, vmem_limit_bytes=VMEM_LIMIT)


def _cparams_gather():
    return pltpu.CompilerParams(dimension_semantics=("arbitrary",), vmem_limit_bytes=VMEM_LIMIT,
                                disable_bounds_checks=True)


def _resident(shape):
    nd = len(shape)
    return pl.BlockSpec(shape, lambda *_: (0,) * nd)


_NN = ((1,), (0,))
_NT = ((1,), (1,))
_TN = ((0,), (0,))


def _dot(a, b, dims=_NN):
    return lax.dot_general(a.astype(BF16), b.astype(BF16), (dims, ((), ())),
                           preferred_element_type=F32)


def _split(x, n):
    parts, rem = [], x
    for i in range(n):
        p = rem.astype(BF16)
        parts.append(p)
        if i + 1 < n:
            rem = rem - p.astype(F32)
    return parts


def _dot_exact_rhs(a, b_bf16, dims=_NN, n=3):
    out = None
    for p in _split(a, n):
        t = lax.dot_general(p, b_bf16, (dims, ((), ())), preferred_element_type=F32)
        out = t if out is None else out + t
    return out


def _dot_exact_lhs(a_bf16, b, dims=_NN, n=3):
    out = None
    for p in _split(b, n):
        t = lax.dot_general(a_bf16, p, (dims, ((), ())), preferred_element_type=F32)
        out = t if out is None else out + t
    return out


def _dot3(a, b, dims=_NN):
    ah, al = _split(a, 2)
    bh, bl = _split(b, 2)
    dn = (dims, ((), ()))
    return (lax.dot_general(ah, bh, dn, preferred_element_type=F32)
            + (lax.dot_general(ah, bl, dn, preferred_element_type=F32)
               + lax.dot_general(al, bh, dn, preferred_element_type=F32)))


def _rms(x, g):
    return x * lax.rsqrt(jnp.mean(x * x, axis=-1, keepdims=True) + RMS_EPS) * g


def _iota(shape, dim):
    return lax.broadcasted_iota(jnp.int32, shape, dim)


def _block_ones(n, width):
    return (_iota((n, n), 0) // width == _iota((n, n), 1) // width).astype(BF16)


def _bd_rows(x, nblk, width):
    blk = _iota(x.shape, 1) // width
    zero = jnp.zeros_like(x)
    return jnp.concatenate([jnp.where(blk == h, x, zero) for h in range(nblk)], axis=0)


def _norm_rows_kernel(x_ref, g_ref, o_ref):
    o_ref[...] = _rms(x_ref[...], g_ref[...])


def norm_rows(x, g):
    r, d = x.shape
    return pl.pallas_call(
        _norm_rows_kernel,
        out_shape=jax.ShapeDtypeStruct((r, d), F32),
        grid=(1,),
        in_specs=[_resident((r, d)), _resident((1, d))],
        out_specs=_resident((r, d)),
        name="norm_rows",
    )(x, g.reshape(1, d))


SB_COLS = 3 * D_MIX
GATE_COLS = 2 * D_MODEL
D_IN = SB_COLS + D_RW_COLS + GATE_COLS


def _in_proj_kernel(x_ref, g_ref, w_ref, q_ref, k_ref, v_ref, rw_ref, gate_ref, kh_ref, vh_ref):
    h = _rms(x_ref[...], g_ref[...]).astype(BF16)

    def proj(lo, hi):
        return jnp.dot(h, w_ref[:, lo:hi], preferred_element_type=F32)

    q_ref[...] = proj(0, D_MIX).astype(BF16)
    k = proj(D_MIX, 2 * D_MIX)
    v = proj(2 * D_MIX, SB_COLS)
    k_ref[...] = k
    v_ref[...] = v
    kh_ref[...] = k.reshape(kh_ref.shape)
    vh_ref[...] = v.reshape(vh_ref.shape)
    rw_ref[...] = proj(SB_COLS, SB_COLS + D_RW_COLS)
    gate_ref[...] = proj(SB_COLS + D_RW_COLS, D_IN).astype(BF16)


def in_proj(x, g, w_bf16):
    n = x.shape[0]
    tm = min(ROW_TILE, n)
    row = lambda width: pl.BlockSpec((tm, width), lambda i: (i, 0))
    heads = pl.BlockSpec((tm, N_HEADS, HEAD_DIM), lambda i: (i, 0, 0))
    return pl.pallas_call(
        _in_proj_kernel,
        out_shape=(jax.ShapeDtypeStruct((n, D_MIX), BF16),
                   jax.ShapeDtypeStruct((n, D_MIX), F32),
                   jax.ShapeDtypeStruct((n, D_MIX), F32),
                   jax.ShapeDtypeStruct((n, D_RW_COLS), F32),
                   jax.ShapeDtypeStruct((n, GATE_COLS), BF16),
                   jax.ShapeDtypeStruct((n, N_HEADS, HEAD_DIM), F32),
                   jax.ShapeDtypeStruct((n, N_HEADS, HEAD_DIM), F32)),
        grid=(n // tm,),
        in_specs=[row(D_MODEL), _resident((1, D_MODEL)), _resident((D_MODEL, D_IN))],
        out_specs=(row(D_MIX), row(D_MIX), row(D_MIX), row(D_RW_COLS), row(GATE_COLS), heads, heads),
        compiler_params=_cparams("parallel"),
        name="in_proj",
    )(x, g.reshape(1, D_MODEL), w_bf16)


def _plain_proj_kernel(x_ref, w_ref, o_ref):
    o_ref[...] = _dot(x_ref[...], w_ref[...])


def plain_proj(x, w_bf16):
    r, m = x.shape[0], w_bf16.shape[1]
    return pl.pallas_call(
        _plain_proj_kernel,
        out_shape=jax.ShapeDtypeStruct((r, m), F32),
        grid=(1,),
        in_specs=[_resident(x.shape), _resident(w_bf16.shape)],
        out_specs=_resident((r, m)),
        name="plain_proj",
    )(x, w_bf16)


def _rw_prep_kernel(chunk, rw_ref, prev0_ref, mu_ref, w0_ref, a0_ref, kk_ref, ka_ref, rk_ref,
                    wup_ref, aup_ref, gup_ref,
                    rt_ref, at_ref, bt_ref, kt_ref, bh_ref, kh_ref, v_ref, g_ref, bonus_ref, wc_ref,
                    carry_ref):
    tt = rw_ref.shape[1]

    @pl.when(pl.program_id(1) == 0)
    def _():
        carry_ref[...] = prev0_ref[0]

    p = rw_ref[0]
    prev = jnp.where(_iota(p.shape, 0) == 0, carry_ref[...], pltpu.roll(p, 1, 0))
    carry_ref[...] = p[tt - 1:tt]
    xs = p + mu_ref[...] * (prev - p)
    r = xs[:, 0:D_MIX]
    k = xs[:, D_MIX:2 * D_MIX]
    v = xs[:, 2 * D_MIX:3 * D_MIX]
    lo = 3 * D_MIX
    wd = xs[:, lo:lo + LORA_W]
    ad = xs[:, lo + LORA_W:lo + LORA_W + LORA_A]
    gd = xs[:, lo + LORA_W + LORA_A:D_RW_COLS]

    pre_w = w0_ref[...] + _dot(jnp.tanh(wd), wup_ref[...])
    w_log = -jax.nn.softplus(-pre_w) - 0.5
    logw = -jnp.exp(w_log)
    a = jax.nn.sigmoid(a0_ref[...] + _dot(ad, aup_ref[...]))
    g_ref[0] = _dot(jax.nn.sigmoid(gd), gup_ref[...])

    head_ones = _block_ones(D_MIX, HEAD_DIM)
    kk = k * kk_ref[...]
    ss = _dot_exact_rhs(kk * kk, head_ones, n=2)
    kk = kk * jnp.minimum(lax.rsqrt(ss), 1e12)
    k2 = k * (1.0 + (a - 1.0) * ka_ref[...])
    nb = -(kk * a)
    bonus_ref[0] = _dot_exact_rhs(r * k2 * rk_ref[...], head_ones, n=2) * v
    v_ref[0] = v.astype(BF16)

    ti, tj = _iota((tt, tt), 0), _iota((tt, tt), 1)
    same = ti // chunk == tj // chunk
    cum = _dot_exact_lhs((same & (tj <= ti)).astype(BF16), logw)
    tot = jnp.concatenate(
        [jnp.broadcast_to(cum[(c + 1) * chunk - 1:(c + 1) * chunk], (chunk, D_MIX)) for c in range(tt // chunk)],
        axis=0)
    e_in = jnp.exp(cum)
    e_out = jnp.exp(-cum)
    e_end = jnp.exp(tot - cum)
    rt_ref[0] = (r * e_in).astype(BF16)
    at_ref[0] = (kk * jnp.exp(cum - logw)).astype(BF16)
    bt_ref[0] = (nb * e_out).astype(BF16)
    kt_ref[0] = (k2 * e_out).astype(BF16)
    bh_ref[0] = (nb * e_end).astype(BF16)
    kh_ref[0] = (k2 * e_end).astype(BF16)
    etot = jnp.exp(tot)
    for c in range(tt // chunk):
        wc_ref[0, c] = etot[c * chunk:c * chunk + 1]


def rw_prep(rw, prev0, p, chunk, tt):
    b, t, _ = rw.shape
    seq = lambda width, dt: jax.ShapeDtypeStruct((b, t, width), dt)
    tile = lambda width: pl.BlockSpec((1, tt, width), lambda i, j: (i, j, 0))
    vec = lambda width: _resident((1, width))
    n_c = tt // chunk
    return pl.pallas_call(
        functools.partial(_rw_prep_kernel, chunk),
        out_shape=tuple(seq(D_MIX, BF16) for _ in range(7))
        + (seq(D_MIX, F32), seq(D_MIX, F32), jax.ShapeDtypeStruct((b, t // chunk, 1, D_MIX), F32)),
        grid=(b, t // tt),
        in_specs=[tile(D_RW_COLS), pl.BlockSpec((1, 1, D_RW_COLS), lambda i, j: (i, 0, 0)),
                  vec(D_RW_COLS), vec(D_MIX), vec(D_MIX), vec(D_MIX), vec(D_MIX), vec(D_MIX),
                  _resident((LORA_W, D_MIX)), _resident((LORA_A, D_MIX)), _resident((LORA_G, D_MIX))],
        out_specs=tuple(tile(D_MIX) for _ in range(9))
        + (pl.BlockSpec((1, n_c, 1, D_MIX), lambda i, j: (i, j, 0, 0)),),
        scratch_shapes=[pltpu.VMEM((1, D_RW_COLS), F32)],
        compiler_params=_cparams("parallel", "arbitrary"),
        name="rw_prep",
    )(rw, prev0, p["mu"], p["w0"], p["a0"], p["k_k"], p["k_a"], p["r_k"],
      p["w_up"], p["a_up"], p["g_up"])


def _unit_lower_inverse(a, chunk):
    shape = a.shape
    eye = (_iota(shape, 1) % chunk == _iota(shape, 0)).astype(F32)
    res = eye + a
    power = a.astype(BF16)
    for _ in range(chunk.bit_length() - 2):
        power = _dot(power, _bd_rows(power, RW_GROUP, chunk)).astype(BF16)
        yield
        res = res + _dot(res, _bd_rows(power, RW_GROUP, chunk))
    return res


def _interleave(chains):
    live = list(chains)
    while live:
        live = [c for c in live if next(c, StopIteration) is not StopIteration]


def _rw_chunk_kernel(chunk, rt_ref, at_ref, bt_ref, kt_ref, bh_ref, kh_ref, v_ref, g_ref, bonus_ref,
                     wc_ref, s0_ref, gnw_ref, gnb_ref, o_ref, s_out_ref, st_ref):
    c = pl.program_id(1)

    @pl.when(c == 0)
    def _():
        st_ref[...] = s0_ref[...]

    cat = (chunk, RW_GROUP * chunk)
    col_t = _iota(cat, 1) % chunk
    strict = col_t < _iota(cat, 0)
    incl = col_t <= _iota(cat, 0)
    head_ones = _block_ones(RW_LANES, HEAD_DIM)
    bd = lambda x: _bd_rows(x, RW_GROUP, HEAD_DIM)
    n_cat = RW_GROUP * chunk
    def chain(bi, grp):
        sl = slice(grp * RW_LANES, (grp + 1) * RW_LANES)
        rt, at, bt, kt = rt_ref[bi, :, sl], at_ref[bi, :, sl], bt_ref[bi, :, sl], kt_ref[bi, :, sl]
        bh, kh, v = bh_ref[bi, :, sl], kh_ref[bi, :, sl], v_ref[bi, :, sl]
        st = st_ref[bi, :, sl]

        gram = _dot(jnp.concatenate([at, rt], axis=0),
                    jnp.concatenate([bd(bt), bd(kt)], axis=0), _NT)
        yield
        zero = jnp.zeros(cat, F32)
        a_ab = jnp.where(strict, gram[:chunk, :n_cat], zero)
        a_ak = jnp.where(strict, gram[:chunk, n_cat:], zero)
        a_r = jnp.concatenate([jnp.where(incl, gram[chunk:, :n_cat], zero),
                               jnp.where(incl, gram[chunk:, n_cat:], zero)], axis=1).astype(BF16)
        v_bd = bd(v)
        x = _dot(a_ak, v_bd).astype(BF16)
        t_inv = yield from _unit_lower_inverse(a_ab, chunk)
        yield
        ua = _dot(t_inv, jnp.concatenate([bd(x), bd(at)], axis=1))
        st_bd = bd(st.astype(BF16))
        yield
        from_state = _dot(jnp.concatenate([ua[:, RW_LANES:].astype(BF16), rt], axis=0), st_bd)
        yield
        u = (ua[:, :RW_LANES] + from_state[:chunk]).astype(BF16)
        y = from_state[chunk:] + _dot(a_r, jnp.concatenate([bd(u), v_bd], axis=0))
        m = _dot(jnp.concatenate([bh, kh], axis=0), jnp.concatenate([u, v], axis=0), _TN)
        yield
        lane_head = _iota((HEAD_DIM, RW_LANES), 1) // HEAD_DIM
        fold = jnp.zeros((HEAD_DIM, RW_LANES), F32)
        for h in range(RW_GROUP):
            fold = fold + jnp.where(lane_head == h, m[h * HEAD_DIM:(h + 1) * HEAD_DIM], 0.0)
        diag = _iota((HEAD_DIM, RW_LANES), 1) % HEAD_DIM == _iota((HEAD_DIM, RW_LANES), 0)
        w_rows = jnp.where(diag, wc_ref[bi, 0, :, sl], 0.0)
        w_t = _dot_exact_rhs(w_rows, head_ones)
        st_ref[bi, :, sl] = st * w_t + fold

        mean = _dot_exact_rhs(y, head_ones, n=2) * (1.0 / HEAD_DIM)
        yield
        d = y - mean
        var = _dot_exact_rhs(d * d, head_ones, n=2) * (1.0 / HEAD_DIM)
        yield
        yn = d * lax.rsqrt(var + GN_EPS) * gnw_ref[:, sl] + gnb_ref[:, sl]
        o_ref[bi, :, sl] = ((yn + bonus_ref[bi, :, sl]) * g_ref[bi, :, sl]).astype(BF16)

    _interleave(chain(bi, grp) for bi in range(st_ref.shape[0]) for grp in range(N_HEADS // RW_GROUP))

    @pl.when(c == pl.num_programs(1) - 1)
    def _():
        s_out_ref[...] = st_ref[...]


def rw_chunks(prep, s0_t, gn_w, gn_b, chunk):
    rt, at, bt, kt, bh, kh, v, g, bonus, wc = prep
    b, t, _ = rt.shape
    nb = RW_SEQS if b % RW_SEQS == 0 else 1
    tile = pl.BlockSpec((nb, chunk, D_MIX), lambda i, j: (i, j, 0))
    state = pl.BlockSpec((nb, HEAD_DIM, D_MIX), lambda i, j: (i, 0, 0))
    return pl.pallas_call(
        functools.partial(_rw_chunk_kernel, chunk),
        out_shape=(jax.ShapeDtypeStruct((b, t, D_MIX), BF16),
                   jax.ShapeDtypeStruct((b, HEAD_DIM, D_MIX), F32)),
        grid=(b // nb, t // chunk),
        in_specs=[tile] * 9 + [pl.BlockSpec((nb, 1, 1, D_MIX), lambda i, j: (i, j, 0, 0)), state,
                               _resident((1, D_MIX)), _resident((1, D_MIX))],
        out_specs=(tile, state),
        scratch_shapes=[pltpu.VMEM((nb, HEAD_DIM, D_MIX), F32)],
        compiler_params=_cparams("parallel", "arbitrary"),
        name="rw_chunks",
    )(rt, at, bt, kt, bh, kh, v, g, bonus, wc, s0_t, gn_w, gn_b)


SB_PAIR = 2 * HEAD_DIM
SB_PAST_BLOCK = 256
SB_VISITS = 4


def _sb_kernel(tq, n_past, n_q, q_ref, kn_ref, vn_ref, *rest):
    if n_past:
        kp_ref, vp_ref, o_ref, acc_ref, c_ref = rest
    else:
        o_ref, acc_ref, c_ref = rest
    qi = pl.program_id(2)
    q2 = q_ref[0] * (HEAD_DIM ** -0.5)
    lane = _iota(q2.shape, 1)
    zero_q = jnp.zeros_like(q2)
    q_st = jnp.concatenate([jnp.where(lane < HEAD_DIM, q2, zero_q), jnp.where(lane >= HEAD_DIM, q2, zero_q)],
                           axis=0)
    acc_ref[...] = jnp.zeros_like(acc_ref)
    c_ref[...] = jnp.zeros_like(c_ref)

    def visit(k_blk, v_blk, masked):
        tk = k_blk.shape[0]
        later = (_iota((tk, tk), 0) > _iota((tk, tk), 1)).astype(BF16)
        z = _dot(q_st, k_blk, _NT)
        sp = jnp.maximum(z, 0.0) + jnp.log(1.0 + jnp.exp(-jnp.abs(z)))
        if masked:
            before = _iota((2 * tq, tk), 1) < _iota((2 * tq, tk), 0) % tq
            drop = jnp.where(before, sp, 0.0)
        else:
            drop = sp
        srev = _dot(drop, later)
        yield
        c = c_ref[...]
        att = jnp.exp((z - sp) - (c + srev))
        if masked:
            att = jnp.where(before, att, 0.0)
        acc_ref[...] += _dot(att, v_blk)
        c_ref[...] = c + (srev[:, 0:1] + drop[:, 0:1])

    def new_block(blk, masked=False):
        s = pl.multiple_of(blk * tq, tq)
        return visit(kn_ref[0, pl.ds(s, tq), :], vn_ref[0, pl.ds(s, tq), :], masked)

    lead = SB_VISITS - 1
    for n_with in range(min(lead, n_q - 1) + 1):
        @pl.when((qi == n_with) if n_with < lead else (qi >= lead))
        def _():
            _interleave([new_block(qi, masked=True)] + [new_block(qi - 1 - u) for u in range(n_with)])

    left = jnp.maximum(qi - lead, 0)

    def earlier_group(i, carry):
        _interleave(new_block(left - 1 - u - SB_VISITS * i) for u in range(SB_VISITS))
        return carry

    lax.fori_loop(0, left // SB_VISITS, earlier_group, 0)
    rest = left % SB_VISITS

    @pl.when(rest >= 2)
    def _():
        _interleave([new_block(rest - 1), new_block(rest - 2)])

    @pl.when(rest % 2 == 1)
    def _():
        _interleave([new_block(0)])

    if n_past:
        def past_block(blk):
            s = pl.multiple_of(blk * SB_PAST_BLOCK, SB_PAST_BLOCK)
            return visit(kp_ref[0, pl.ds(s, SB_PAST_BLOCK), :], vp_ref[0, pl.ds(s, SB_PAST_BLOCK), :], False)

        def past_group(i, carry):
            _interleave(past_block(n_past - 1 - u - SB_VISITS * i) for u in range(SB_VISITS))
            return carry

        lax.fori_loop(0, n_past // SB_VISITS, past_group, 0)
        _interleave(past_block(blk) for blk in reversed(range(n_past % SB_VISITS)))

    o_ref[0] = jnp.where(lane < HEAD_DIM, acc_ref[:tq], acc_ref[tq:]).astype(BF16)


def stick_breaking(q, k_new, v_new, k_past, v_past, tq):
    b, t, _ = q.shape
    n_pairs = D_MIX // SB_PAIR
    qspec = pl.BlockSpec((1, tq, SB_PAIR), lambda i, p, j: (i, j, p))
    seq = lambda length: pl.BlockSpec((1, length, SB_PAIR), lambda i, p, j: (i, 0, p))
    args, specs, n_past = [q, k_new, v_new], [qspec, seq(t), seq(t)], 0
    if k_past is not None:
        past_len = k_past.shape[1]
        n_past = past_len // SB_PAST_BLOCK
        args += [k_past, v_past]
        specs += [seq(past_len), seq(past_len)]
    return pl.pallas_call(
        functools.partial(_sb_kernel, tq, n_past, t // tq),
        out_shape=jax.ShapeDtypeStruct((b, t, D_MIX), BF16),
        grid=(b, n_pairs, t // tq),
        in_specs=specs,
        out_specs=qspec,
        scratch_shapes=[pltpu.VMEM((2 * tq, SB_PAIR), F32), pltpu.VMEM((2 * tq, 1), F32)],
        compiler_params=_cparams("parallel", "parallel", "arbitrary"),
        name="stick_breaking",
    )(*args)


def _merge_kernel(orw_ref, osb_ref, gate_ref, x_ref, wb0_ref, wb1_ref, wout_ref, gx_ref, wxq_ref,
                  x1_ref, qx_ref):
    g = jax.nn.sigmoid(gate_ref[...].astype(F32))
    mixed = (g[:, :D_MODEL] * jnp.dot(orw_ref[...], wb0_ref[...], preferred_element_type=F32)
             + g[:, D_MODEL:] * jnp.dot(osb_ref[...], wb1_ref[...], preferred_element_type=F32))
    x1 = x_ref[...] + _dot(mixed, wout_ref[...])
    x1_ref[...] = x1
    qx_ref[...] = _dot(_rms(x1, gx_ref[...]), wxq_ref[...]).astype(BF16)


def merge_out(o_rw, o_sb, gate, x, wb0, wb1, wout, gx, wxq):
    n = x.shape[0]
    tm = min(ROW_TILE, n)
    row = lambda width: pl.BlockSpec((tm, width), lambda i: (i, 0))
    return pl.pallas_call(
        _merge_kernel,
        out_shape=(jax.ShapeDtypeStruct((n, D_MODEL), F32), jax.ShapeDtypeStruct((n, D_MODEL), BF16)),
        grid=(n // tm,),
        in_specs=[row(D_MIX), row(D_MIX), row(GATE_COLS), row(D_MODEL),
                  _resident((D_MIX, D_MODEL)), _resident((D_MIX, D_MODEL)), _resident((D_MODEL, D_MODEL)),
                  _resident((1, D_MODEL)), _resident((D_MODEL, D_MODEL))],
        out_specs=(row(D_MODEL), row(D_MODEL)),
        compiler_params=_cparams("parallel"),
        name="merge_out",
    )(o_rw, o_sb, gate, x, wb0, wb1, wout, gx.reshape(1, D_MODEL), wxq)


def _mem_kv_kernel(m_ref, g_ref, wk_ref, wv_ref, k_ref, v_ref):
    h = _rms(m_ref[...], g_ref[...]).astype(BF16)
    k_ref[...] = jnp.dot(h, wk_ref[...], preferred_element_type=F32)
    v_ref[...] = jnp.dot(h, wv_ref[...], preferred_element_type=F32)


def mem_kv(mem, g, wk, wv):
    n = mem.shape[0]
    tm = min(ROW_TILE, n)
    row = pl.BlockSpec((tm, D_MODEL), lambda i: (i, 0))
    return pl.pallas_call(
        _mem_kv_kernel,
        out_shape=(jax.ShapeDtypeStruct((n, D_MODEL), F32),) * 2,
        grid=(n // tm,),
        in_specs=[row, _resident((1, D_MODEL)), _resident((D_MODEL, D_MODEL)), _resident((D_MODEL, D_MODEL))],
        out_specs=(row, row),
        compiler_params=_cparams("parallel"),
        name="mem_kv",
    )(mem, g.reshape(1, D_MODEL), wk, wv)


ROUTER_LANES = 128


def _xattn_kernel(q_ref, x1_ref, mk_ref, mv_ref, wxo_ref, gf_ref, wr_ref, br_ref,
                  x2_ref, hf_ref, idx_ref, gates_ref, counts_ref):
    q = q_ref[0]
    heads = [None] * N_X_HEADS

    def head(h):
        sl = slice(h * X_HEAD_DIM, (h + 1) * X_HEAD_DIM)
        s = _dot(q[:, sl], mk_ref[0, :, sl], _NT) * (X_HEAD_DIM ** -0.5)
        yield
        e = jnp.exp(s - jnp.max(s, axis=-1, keepdims=True))
        p = e * (1.0 / jnp.sum(e, axis=-1, keepdims=True))
        heads[h] = _dot(p, mv_ref[0, :, sl])

    _interleave(head(h) for h in range(N_X_HEADS))
    x2 = x1_ref[0] + _dot(jnp.concatenate(heads, axis=1), wxo_ref[...])
    x2_ref[0] = x2
    hf = _rms(x2, gf_ref[...])
    hf_ref[0] = hf
    logits = _dot3(hf, wr_ref[...]) + br_ref[...]
    lane = _iota(logits.shape, 1)
    vals, idx_out = [], jnp.zeros(logits.shape, jnp.int32)
    chosen = jnp.zeros(logits.shape, F32)
    for j in range(TOP_K):
        m = jnp.max(logits, axis=-1, keepdims=True)
        pick = jnp.min(jnp.where(logits == m, lane, ROUTER_LANES), axis=-1, keepdims=True)
        vals.append(m)
        idx_out = jnp.where(lane == j, pick, idx_out)
        chosen = chosen + (lane == pick).astype(F32)
        logits = jnp.where(lane == pick, -jnp.inf, logits)

    @pl.when((pl.program_id(0) == 0) & (pl.program_id(1) == 0))
    def _():
        counts_ref[...] = jnp.zeros_like(counts_ref)

    counts_ref[...] += jnp.broadcast_to(jnp.sum(chosen, axis=0, keepdims=True), counts_ref.shape)
    exps = [jnp.exp(v - vals[0]) for v in vals]
    denom = exps[0] + exps[1] + exps[2] + exps[3]
    gates = jnp.zeros(logits.shape, F32)
    for j in range(TOP_K):
        gates = jnp.where(lane == j, exps[j] / denom, gates)
    idx_ref[0] = idx_out
    gates_ref[0] = gates


def xattn_router(qx, x1, mk, mv, wxo, g_ffn, w_router, b_router):
    b, t, _ = x1.shape
    tm = min(ROW_TILE, t)
    n_mem = mk.shape[1]
    wr = jnp.zeros((D_MODEL, ROUTER_LANES), F32).at[:, :N_EXPERTS].set(w_router)
    br = jnp.full((1, ROUTER_LANES), -jnp.inf, F32).at[0, :N_EXPERTS].set(b_router)
    tile = pl.BlockSpec((1, tm, D_MODEL), lambda i, j: (i, j, 0))
    mem = pl.BlockSpec((1, n_mem, D_MODEL), lambda i, j: (i, 0, 0))
    small = pl.BlockSpec((1, tm, ROUTER_LANES), lambda i, j: (i, j, 0))
    return pl.pallas_call(
        _xattn_kernel,
        out_shape=(jax.ShapeDtypeStruct((b, t, D_MODEL), F32), jax.ShapeDtypeStruct((b, t, D_MODEL), F32),
                   jax.ShapeDtypeStruct((b, t, ROUTER_LANES), jnp.int32),
                   jax.ShapeDtypeStruct((b, t, ROUTER_LANES), F32),
                   jax.ShapeDtypeStruct((ROUTE_SUBLANES, ROUTER_LANES), F32)),
        grid=(b, t // tm),
        in_specs=[tile, tile, mem, mem, _resident((D_MODEL, D_MODEL)), _resident((1, D_MODEL)),
                  _resident((D_MODEL, ROUTER_LANES)), _resident((1, ROUTER_LANES))],
        out_specs=(tile, tile, small, small, _resident((ROUTE_SUBLANES, ROUTER_LANES))),
        compiler_params=_cparams("arbitrary", "arbitrary"),
        name="xattn_router",
    )(qx, x1, mk, mv, wxo, g_ffn.reshape(1, D_MODEL), wr, br)


ROW_GROUP = 8
ROUTE_SUBLANES = 8


def _route_kernel(idx_ref, total_ref, dest_ref, carry_ref, start_ref):
    tm = idx_ref.shape[0]
    idx = idx_ref[...]
    lane = _iota((tm, ROUTER_LANES), 1)
    picks = [idx[:, j:j + 1] for j in range(TOP_K)]
    onehot = jnp.zeros((tm, ROUTER_LANES), F32)
    for pick in picks:
        onehot = onehot + (lane == pick).astype(F32)

    @pl.when(pl.program_id(0) == 0)
    def _():
        carry_ref[...] = jnp.zeros_like(carry_ref)
        padded = jnp.floor((total_ref[...] + (MOE_BLOCK - 1.0)) * (1.0 / MOE_BLOCK)) * MOE_BLOCK
        earlier = (_iota((ROUTER_LANES, ROUTER_LANES), 0) < _iota((ROUTER_LANES, ROUTER_LANES), 1))
        start_ref[...] = _dot_exact_rhs(padded, earlier.astype(BF16))

    below = (_iota((tm, tm), 1) < _iota((tm, tm), 0)).astype(BF16)
    row = _dot(below, onehot) + (carry_ref[0:1, :] + start_ref[0:1, :])
    dest = jnp.zeros((tm, ROUTER_LANES), F32)
    for j, pick in enumerate(picks):
        mine = jnp.sum(jnp.where(lane == pick, row, 0.0), axis=1, keepdims=True)
        dest = jnp.where(lane == j, mine, dest)
    dest_ref[0] = dest.T[:ROUTE_SUBLANES].astype(jnp.int32)
    carry_ref[...] += jnp.broadcast_to(jnp.sum(onehot, axis=0, keepdims=True), carry_ref.shape)


def moe_route(idx, total, tm):
    n = idx.shape[0]
    n_tiles = n // tm
    return pl.pallas_call(
        _route_kernel,
        out_shape=jax.ShapeDtypeStruct((n_tiles, ROUTE_SUBLANES, tm), jnp.int32),
        grid=(n_tiles,),
        in_specs=[pl.BlockSpec((tm, ROUTER_LANES), lambda t: (t, 0)), _resident((ROUTE_SUBLANES, ROUTER_LANES))],
        out_specs=pl.BlockSpec((1, ROUTE_SUBLANES, tm), lambda t: (t, 0, 0)),
        scratch_shapes=[pltpu.VMEM((ROUTE_SUBLANES, ROUTER_LANES), F32),
                        pltpu.VMEM((ROUTE_SUBLANES, ROUTER_LANES), F32)],
        compiler_params=_cparams("arbitrary"),
        name="moe_route",
    )(idx, total)


def _block_tables(counts, n_blocks):
    cnt = counts[0, :N_EXPERTS].astype(jnp.int32)
    padded = (cnt + MOE_BLOCK - 1) // MOE_BLOCK * MOE_BLOCK
    pends = jnp.cumsum(padded).astype(jnp.int32)
    block_start = jnp.arange(n_blocks, dtype=jnp.int32) * MOE_BLOCK
    block_expert = jnp.minimum(jnp.sum((pends[None, :] <= block_start[:, None]).astype(jnp.int32), axis=1),
                               N_EXPERTS - 1)
    n_used = (pends[-1:] // MOE_BLOCK).astype(jnp.int32)
    return pends, block_expert, n_used


def _dispatch_kernel(bounds, pends_ref, dest_ref, *rest):
    hf_refs = rest[:len(bounds) - 1]
    xs_hbm, zbuf, sem = rest[len(bounds) - 1:]
    i = pl.program_id(0)

    @pl.when(i == 0)
    def _():
        zbuf[...] = jnp.zeros_like(zbuf)
        n_blocks = xs_hbm.shape[0] // MOE_BLOCK
        n_used = pends_ref[N_EXPERTS - 1] // MOE_BLOCK
        for wait in (False, True):
            def unused_block(blk, carry):
                cp = pltpu.make_async_copy(
                    zbuf, xs_hbm.at[pl.ds(pl.multiple_of(blk * MOE_BLOCK, MOE_BLOCK), MOE_BLOCK)], sem)
                cp.wait() if wait else cp.start()
                return carry

            lax.fori_loop(n_used, n_blocks, unused_block, 0)
            for e in range(N_EXPERTS):
                end = pends_ref[e]
                begin = pends_ref[e - 1] if e else 0

                @pl.when(end > begin)
                def _():
                    last = pl.multiple_of(end - MOE_BLOCK, MOE_BLOCK)
                    cp = pltpu.make_async_copy(zbuf, xs_hbm.at[pl.ds(last, MOE_BLOCK)], sem)
                    cp.wait() if wait else cp.start()

    for hf_ref, lo, hi in zip(hf_refs, bounds[:-1], bounds[1:]):
        @pl.when((i >= lo) & (i < hi))
        def _():
            tm = hf_ref.shape[0]
            for j in range(TOP_K):
                def issue(g, carry):
                    base = pl.multiple_of(g * ROW_GROUP, ROW_GROUP)
                    rows = hf_ref.at[pl.ds(base, ROW_GROUP)]
                    for u in range(ROW_GROUP):
                        pltpu.make_async_copy(rows.at[pl.ds(u, 1)], xs_hbm.at[pl.ds(dest_ref[j * tm + base + u], 1)],
                                              sem).start()
                    return carry

                lax.fori_loop(0, tm // ROW_GROUP, issue, 0)
            for j in range(TOP_K):
                pltpu.make_async_copy(hf_ref, xs_hbm.at[pl.ds(0, tm)], sem).wait()


def moe_dispatch(hfs, bounds, dest, pends, n_rows):
    n_tiles, _, tm = dest.shape
    hf_spec = lambda lo, hi: pl.BlockSpec((tm, D_MODEL), lambda i, pe: (jnp.clip(i - lo, 0, hi - lo - 1), 0))
    grid_spec = pltpu.PrefetchScalarGridSpec(
        num_scalar_prefetch=1,
        grid=(n_tiles,),
        in_specs=[pl.BlockSpec((ROUTE_SUBLANES * tm,), lambda i, pe: (i,), memory_space=pltpu.SMEM)]
        + [hf_spec(lo, hi) for lo, hi in zip(bounds[:-1], bounds[1:])],
        out_specs=pl.BlockSpec(memory_space=pl.ANY),
        scratch_shapes=[pltpu.VMEM((MOE_BLOCK, D_MODEL), F32), pltpu.SemaphoreType.DMA(())],
    )
    return pl.pallas_call(
        functools.partial(_dispatch_kernel, tuple(bounds)),
        out_shape=jax.ShapeDtypeStruct((n_rows, D_MODEL), F32),
        grid_spec=grid_spec,
        compiler_params=_cparams_gather(),
        name="moe_dispatch",
    )(pends, dest.reshape(-1), *hfs)


W1_CHUNKS = 4
W2_CHUNKS = 2


def _moe_kernel(be_ref, nused_ref, x_ref, *rest):
    w1 = rest[:W1_CHUNKS]
    b1_ref = rest[W1_CHUNKS]
    w2 = rest[W1_CHUNKS + 1:W1_CHUNKS + 1 + W2_CHUNKS]
    b2_ref, o_ref, w1_bf, w2_bf = rest[W1_CHUNKS + 1 + W2_CHUNKS:]
    i = pl.program_id(0)
    half = W1_CHUNKS // 2
    wide = 2 * D_FF // W1_CHUNKS

    @pl.when((i < nused_ref[0]) & ((i == 0) | (be_ref[i] != be_ref[jnp.maximum(i - 1, 0)])))
    def _():
        for c in range(W1_CHUNKS):
            w1_bf[c] = w1[c][0].astype(BF16)
        for c in range(W2_CHUNKS):
            w2_bf[c] = w2[c][0].astype(BF16)

    @pl.when(i < nused_ref[0])
    def _():
        x = x_ref[...].astype(BF16)
        parts = [None] * half

        def ff_slice(c):
            glu = jnp.dot(x, w1_bf[c], preferred_element_type=F32) + b1_ref[0, :, c * wide:(c + 1) * wide]
            lin = (jnp.dot(x, w1_bf[half + c], preferred_element_type=F32)
                   + b1_ref[0, :, D_FF + c * wide:D_FF + (c + 1) * wide])
            yield
            glu = jnp.minimum(glu, SWIGLU_LIMIT)
            lin = jnp.clip(lin, -SWIGLU_LIMIT, SWIGLU_LIMIT)
            act = glu * jax.nn.sigmoid(SWIGLU_ALPHA * glu) * (lin + 1.0)
            parts[c] = _dot(act, w2_bf[c])

        _interleave(ff_slice(c) for c in range(half))
        o_ref[...] = b2_ref[0] + sum(parts[1:], parts[0])

    @pl.when(i >= nused_ref[0])
    def _():
        o_ref[...] = jnp.zeros_like(o_ref)


def moe_experts(xs, block_expert, n_used, w1, b1, w2, b2):
    n_blocks = block_expert.shape[0]
    assert W1_CHUNKS // 2 == W2_CHUNKS
    c1 = 2 * D_FF // W1_CHUNKS
    c2 = D_FF // W2_CHUNKS
    w1_spec = lambda c: pl.BlockSpec((1, D_MODEL, c1), lambda i, be, nu: (be[i], 0, c))
    w2_spec = lambda c: pl.BlockSpec((1, c2, D_MODEL), lambda i, be, nu: (be[i], c, 0))
    grid_spec = pltpu.PrefetchScalarGridSpec(
        num_scalar_prefetch=2,
        grid=(n_blocks,),
        in_specs=[pl.BlockSpec((MOE_BLOCK, D_MODEL), lambda i, be, nu: (jnp.minimum(i, nu[0] - 1), 0))]
        + [w1_spec(c) for c in range(W1_CHUNKS)]
        + [pl.BlockSpec((1, 1, 2 * D_FF), lambda i, be, nu: (be[i], 0, 0))]
        + [w2_spec(c) for c in range(W2_CHUNKS)]
        + [pl.BlockSpec((1, 1, D_MODEL), lambda i, be, nu: (be[i], 0, 0))],
        out_specs=pl.BlockSpec((MOE_BLOCK, D_MODEL), lambda i, be, nu: (i, 0)),
        scratch_shapes=[pltpu.VMEM((W1_CHUNKS, D_MODEL, c1), BF16), pltpu.VMEM((W2_CHUNKS, c2, D_MODEL), BF16)],
    )
    return pl.pallas_call(
        _moe_kernel,
        out_shape=jax.ShapeDtypeStruct((n_blocks * MOE_BLOCK, D_MODEL), F32),
        grid_spec=grid_spec,
        compiler_params=_cparams("arbitrary"),
        name="moe_experts",
    )(block_expert, n_used, xs, *([w1] * W1_CHUNKS), b1.reshape(N_EXPERTS, 1, 2 * D_FF),
      *([w2] * W2_CHUNKS), b2.reshape(N_EXPERTS, 1, D_MODEL))


def _start_combine_gather(dest_ref, os_hbm, dst, sem):
    tm = dst.shape[0] // TOP_K
    for j in range(TOP_K):
        def issue(g, carry):
            base = pl.multiple_of(g * ROW_GROUP, ROW_GROUP)
            rows = dst.at[pl.ds(j * tm + base, ROW_GROUP)]
            for u in range(ROW_GROUP):
                pltpu.make_async_copy(os_hbm.at[pl.ds(dest_ref[j * tm + base + u], 1)], rows.at[pl.ds(u, 1)],
                                      sem).start()
            return carry

        lax.fori_loop(0, tm // ROW_GROUP, issue, 0)


def _combine_kernel(n_tiles, dest_ref, dest_next_ref, os_hbm, x2_ref, gates_ref, gfin_ref, y_ref, buf, sem):
    tm = x2_ref.shape[0]
    i = pl.program_id(0)
    slot = i % 2

    @pl.when(i == 0)
    def _():
        _start_combine_gather(dest_ref, os_hbm, buf.at[0], sem.at[0])

    if n_tiles > 1:
        @pl.when(i + 1 < n_tiles)
        def _():
            _start_combine_gather(dest_next_ref, os_hbm, buf.at[1 - slot], sem.at[1 - slot])

    pltpu.make_async_copy(os_hbm.at[pl.ds(0, TOP_K * tm)], buf.at[slot], sem.at[slot]).wait()
    y = x2_ref[...]
    gates = gates_ref[...]
    for j in range(TOP_K):
        y = y + gates[:, j:j + 1] * buf[slot, j * tm:(j + 1) * tm, :]
    y_ref[...] = _rms(y, gfin_ref[...])


def moe_combine(os_rows, dest, x2, gates, g_final):
    n = x2.shape[0]
    n_tiles, _, tm = dest.shape
    dest_spec = lambda step: pl.BlockSpec(
        (ROUTE_SUBLANES * tm,), lambda i: (jnp.minimum(i + step, n_tiles - 1),), memory_space=pltpu.SMEM)
    dest = dest.reshape(-1)
    return pl.pallas_call(
        functools.partial(_combine_kernel, n_tiles),
        out_shape=jax.ShapeDtypeStruct((n, D_MODEL), F32),
        grid=(n_tiles,),
        in_specs=[dest_spec(0), dest_spec(1),
                  pl.BlockSpec(memory_space=pl.ANY),
                  pl.BlockSpec((tm, D_MODEL), lambda i: (i, 0)),
                  pl.BlockSpec((tm, ROUTER_LANES), lambda i: (i, 0)),
                  _resident((1, D_MODEL))],
        out_specs=pl.BlockSpec((tm, D_MODEL), lambda i: (i, 0)),
        scratch_shapes=[pltpu.VMEM((2, TOP_K * tm, D_MODEL), F32), pltpu.SemaphoreType.DMA((2,))],
        compiler_params=_cparams_gather(),
        name="moe_combine",
    )(dest, dest, os_rows, x2, gates, g_final.reshape(1, D_MODEL))


def moe_ffn(groups, w):
    sizes = [g[0].shape[0] for g in groups]
    n = sum(sizes)
    tm = next(c for c in (256, 128, 64, 32, 16, 8) if all(s % c == 0 for s in sizes))
    counts = sum(g[4] for g in groups[1:]) + groups[0][4]
    dest = moe_route(jnp.concatenate([g[2] for g in groups], axis=0), counts, tm)
    n_blocks = -(-(n * TOP_K + N_EXPERTS * (MOE_BLOCK - 1)) // MOE_BLOCK)
    pends, block_expert, n_used = _block_tables(counts, n_blocks)
    bounds = [0]
    for s in sizes:
        bounds.append(bounds[-1] + s // tm)
    xs = moe_dispatch([g[1] for g in groups], bounds, dest, pends, n_blocks * MOE_BLOCK)
    os_rows = moe_experts(xs, block_expert, n_used, w["w_e1"], w["b_e1"], w["w_e2"], w["b_e2"])
    return [moe_combine(os_rows, dest[lo:hi], g[0], g[3], w["g_final"])
            for g, lo, hi in zip(groups, bounds[:-1], bounds[1:])]


def _mixers(x, prev0, s0_t, k_past, v_past, mk, mv, w, chunk, prep_tile, tq):
    b, t, _ = x.shape
    n = b * t
    xf = x.reshape(n, D_MODEL)
    q, k_sb, v_sb, rw, gate, k_heads, v_heads = in_proj(xf, w["g_mix"], w["w_in"])
    shift = norm_rows(x[:, -1], w["g_mix"])
    prep = rw_prep(rw.reshape(b, t, D_RW_COLS), prev0, w["rw"], chunk, prep_tile)
    o_rw, s_t = rw_chunks(prep, s0_t, w["gn_w"], w["gn_b"], chunk)
    seq = lambda a: a.reshape(b, t, D_MIX)
    o_sb = stick_breaking(seq(q), seq(k_sb), seq(v_sb), k_past, v_past, tq)
    x1, qx = merge_out(o_rw.reshape(n, D_MIX), o_sb.reshape(n, D_MIX), gate, xf,
                       w["wb0"], w["wb1"], w["w_out"], w["g_xattn"], w["w_xq"])
    x2, hf, idx, gates, counts = xattn_router(qx.reshape(b, t, D_MODEL), x1.reshape(b, t, D_MODEL), mk, mv,
                                              w["w_xo"], w["g_ffn"], w["w_router"], w["b_router"])
    tok = (x2.reshape(n, D_MODEL), hf.reshape(n, D_MODEL), idx.reshape(n, ROUTER_LANES),
           gates.reshape(n, ROUTER_LANES), counts)
    return tok, shift, s_t, k_heads, v_heads


def _state_to_t(s):
    b = s.shape[0]
    return s.transpose(0, 3, 1, 2).reshape(b, HEAD_DIM, D_MIX)


def _state_from_t(s_t):
    b = s_t.shape[0]
    return s_t.reshape(b, HEAD_DIM, N_HEADS, HEAD_DIM).transpose(0, 2, 3, 1)


def kernel(x_prompt, x_sample, state_rw_shift, state_rw_wkv, cache_sb_k, cache_sb_v, cache_mem_k, cache_mem_v, mem_prompt, g_mix, w_in, rw_mu, rw_w0, rw_w_up, rw_a0, rw_a_up, rw_g_up, rw_k_k, rw_k_a, rw_r_k, rw_gn_w, rw_gn_b, w_branch, w_out, g_xattn, g_mem, w_xq, w_mk, w_mv, w_xo, g_ffn, w_router, b_router, w_e1, b_e1, w_e2, b_e2, g_final):
    assert g_mix.shape[0] == 1, "single-layer trunk"
    row = lambda a: a.reshape(1, -1)
    w = dict(
        g_mix=g_mix[0], w_in=w_in[0].astype(BF16),
        rw=dict(mu=row(rw_mu[0]), w0=row(rw_w0[0]), a0=row(rw_a0[0]), k_k=row(rw_k_k[0]), k_a=row(rw_k_a[0]),
                r_k=row(rw_r_k[0]), w_up=rw_w_up[0].astype(BF16), a_up=rw_a_up[0].astype(BF16),
                g_up=rw_g_up[0].astype(BF16)),
        gn_w=row(rw_gn_w[0]), gn_b=row(rw_gn_b[0]),
        wb0=w_branch[0, 0].astype(BF16), wb1=w_branch[0, 1].astype(BF16), w_out=w_out[0].astype(BF16),
        g_xattn=g_xattn[0], w_xq=w_xq[0].astype(BF16), w_xo=w_xo[0].astype(BF16),
        g_ffn=g_ffn[0], w_router=w_router[0], b_router=b_router[0],
        w_e1=w_e1[0], b_e1=b_e1[0], w_e2=w_e2[0], b_e2=b_e2[0],
        g_final=g_final,
    )
    bp, t, _ = x_prompt.shape
    bs, ts, _ = x_sample.shape
    n_mem = mem_prompt.shape[1]

    mk_p, mv_p = mem_kv(mem_prompt.reshape(bp * n_mem, D_MODEL), g_mem[0],
                        w_mk[0].astype(BF16), w_mv[0].astype(BF16))
    mk_p = mk_p.reshape(bp, n_mem, D_MODEL)
    mv_p = mv_p.reshape(bp, n_mem, D_MODEL)
    tok_p, sh_p, st_p, k_p, v_p = _mixers(
        x_prompt, jnp.zeros((bp, 1, D_RW_COLS), F32), jnp.zeros((bp, HEAD_DIM, D_MIX), F32),
        None, None, mk_p, mv_p, w, chunk=64, prep_tile=256, tq=256)

    prev_s = plain_proj(state_rw_shift[0], w["w_in"][:, SB_COLS:SB_COLS + D_RW_COLS])
    past = cache_sb_k.shape[2]
    tok_s, sh_s, st_s, k_s, v_s = _mixers(
        x_sample, prev_s.reshape(bs, 1, D_RW_COLS), _state_to_t(state_rw_wkv[0]),
        cache_sb_k[0].reshape(bs, past, D_MIX), cache_sb_v[0].reshape(bs, past, D_MIX),
        cache_mem_k[0].reshape(bs, n_mem, D_MODEL), cache_mem_v[0].reshape(bs, n_mem, D_MODEL),
        w, chunk=ts, prep_tile=ts, tq=ts)

    y_p, y_s = moe_ffn([tok_p, tok_s], w)

    heads = lambda a, b_, t_: a.reshape(1, b_, t_, N_HEADS, HEAD_DIM)
    xh = lambda a: a.reshape(1, bp, n_mem, N_X_HEADS, X_HEAD_DIM)
    return (y_p.reshape(bp, t, D_MODEL), y_s.reshape(bs, ts, D_MODEL),
            sh_p[None], _state_from_t(st_p)[None], heads(k_p, bp, t), heads(v_p, bp, t), xh(mk_p), xh(mv_p),
            sh_s[None], _state_from_t(st_s)[None], heads(k_s, bs, ts), heads(v_s, bs, ts))
```

```python
import functools

import jax
import jax.numpy as jnp
from jax import lax
from jax.experimental import pallas as pl
from jax.experimental.pallas import tpu as pltpu

F32 = jnp.float32
BF16 = jnp.bfloat16

D_MODEL = 1024
HEAD_DIM = 64
N_HEADS = 8
D_MIX = N_HEADS * HEAD_DIM
LORA_W, LORA_A, LORA_G = 64, 64, 128
D_RW_COLS = 3 * D_MIX + LORA_W + LORA_A + LORA_G
N_X_HEADS = 4
X_HEAD_DIM = 256
N_EXPERTS = 32
TOP_K = 4
D_FF = 1024
SWIGLU_LIMIT = 7.0
SWIGLU_ALPHA = 1.702
MOE_BLOCK = 512
RMS_EPS = 1e-5
GN_EPS = 64e-5

ROW_TILE = 256
ROW_TILE_SHAPE = (8, D_MODEL // 8)
RW_GROUP = 4
RW_LANES = RW_GROUP * HEAD_DIM
RW_SEQS = 4
VMEM_LIMIT = 56 * 1024 * 1024


def _cparams(*sem):
    return pltpu.CompilerParams(dimension_semantics=sem, vmem_limit_bytes=VMEM_LIMIT)


def _cparams_gather():
    return pltpu.CompilerParams(dimension_semantics=("arbitrary",), vmem_limit_bytes=VMEM_LIMIT,
                                disable_bounds_checks=True)


def _resident(shape):
    nd = len(shape)
    return pl.BlockSpec(shape, lambda *_: (0,) * nd)


_NN = ((1,), (0,))
_NT = ((1,), (1,))
_TN = ((0,), (0,))


def _dot(a, b, dims=_NN):
    return lax.dot_general(a.astype(BF16), b.astype(BF16), (dims, ((), ())),
                           preferred_element_type=F32)


def _split(x, n):
    parts, rem = [], x
    for i in range(n):
        p = rem.astype(BF16)
        parts.append(p)
        if i + 1 < n:
            rem = rem - p.astype(F32)
    return parts


def _dot_exact_rhs(a, b_bf16, dims=_NN, n=3):
    out = None
    for p in _split(a, n):
        t = lax.dot_general(p, b_bf16, (dims, ((), ())), preferred_element_type=F32)
        out = t if out is None else out + t
    return out


def _dot_exact_lhs(a_bf16, b, dims=_NN, n=3):
    out = None
    for p in _split(b, n):
        t = lax.dot_general(a_bf16, p, (dims, ((), ())), preferred_element_type=F32)
        out = t if out is None else out + t
    return out


def _dot3(a, b, dims=_NN):
    ah, al = _split(a, 2)
    bh, bl = _split(b, 2)
    dn = (dims, ((), ()))
    return (lax.dot_general(ah, bh, dn, preferred_element_type=F32)
            + (lax.dot_general(ah, bl, dn, preferred_element_type=F32)
               + lax.dot_general(al, bh, dn, preferred_element_type=F32)))


def _rms(x, g):
    return x * lax.rsqrt(jnp.mean(x * x, axis=-1, keepdims=True) + RMS_EPS) * g


def _iota(shape, dim):
    return lax.broadcasted_iota(jnp.int32, shape, dim)


def _block_ones(n, width):
    return (_iota((n, n), 0) // width == _iota((n, n), 1) // width).astype(BF16)


def _bd_rows(x, nblk, width):
    blk = _iota(x.shape, 1) // width
    zero = jnp.zeros_like(x)
    return jnp.concatenate([jnp.where(blk == h, x, zero) for h in range(nblk)], axis=0)


def _norm_rows_kernel(x_ref, g_ref, o_ref):
    o_ref[...] = _rms(x_ref[...], g_ref[...])


def norm_rows(x, g):
    r, d = x.shape
    return pl.pallas_call(
        _norm_rows_kernel,
        out_shape=jax.ShapeDtypeStruct((r, d), F32),
        grid=(1,),
        in_specs=[_resident((r, d)), _resident((1, d))],
        out_specs=_resident((r, d)),
        name="norm_rows",
    )(x, g.reshape(1, d))


SB_COLS = 3 * D_MIX
GATE_COLS = 2 * D_MODEL
D_IN = SB_COLS + D_RW_COLS + GATE_COLS


def _in_proj_kernel(x_ref, g_ref, w_ref, q_ref, k_ref, v_ref, rw_ref, gate_ref, kh_ref, vh_ref):
    h = _rms(x_ref[...], g_ref[...]).astype(BF16)

    def proj(lo, hi):
        return jnp.dot(h, w_ref[:, lo:hi], preferred_element_type=F32)

    q_ref[...] = proj(0, D_MIX).astype(BF16)
    k = proj(D_MIX, 2 * D_MIX)
    v = proj(2 * D_MIX, SB_COLS)
    k_ref[...] = k
    v_ref[...] = v
    kh_ref[...] = k.reshape(kh_ref.shape)
    vh_ref[...] = v.reshape(vh_ref.shape)
    rw_ref[...] = proj(SB_COLS, SB_COLS + D_RW_COLS)
    gate_ref[...] = proj(SB_COLS + D_RW_COLS, D_IN).astype(BF16)


def in_proj(x, g, w_bf16):
    n = x.shape[0]
    tm = min(ROW_TILE, n)
    row = lambda width: pl.BlockSpec((tm, width), lambda i: (i, 0))
    heads = pl.BlockSpec((tm, N_HEADS, HEAD_DIM), lambda i: (i, 0, 0))
    return pl.pallas_call(
        _in_proj_kernel,
        out_shape=(jax.ShapeDtypeStruct((n, D_MIX), BF16),
                   jax.ShapeDtypeStruct((n, D_MIX), F32),
                   jax.ShapeDtypeStruct((n, D_MIX), F32),
                   jax.ShapeDtypeStruct((n, D_RW_COLS), F32),
                   jax.ShapeDtypeStruct((n, GATE_COLS), BF16),
                   jax.ShapeDtypeStruct((n, N_HEADS, HEAD_DIM), F32),
                   jax.ShapeDtypeStruct((n, N_HEADS, HEAD_DIM), F32)),
        grid=(n // tm,),
        in_specs=[row(D_MODEL), _resident((1, D_MODEL)), _resident((D_MODEL, D_IN))],
        out_specs=(row(D_MIX), row(D_MIX), row(D_MIX), row(D_RW_COLS), row(GATE_COLS), heads, heads),
        compiler_params=_cparams("parallel"),
        name="in_proj",
    )(x, g.reshape(1, D_MODEL), w_bf16)


def _plain_proj_kernel(x_ref, w_ref, o_ref):
    o_ref[...] = _dot(x_ref[...], w_ref[...])


def plain_proj(x, w_bf16):
    r, m = x.shape[0], w_bf16.shape[1]
    return pl.pallas_call(
        _plain_proj_kernel,
        out_shape=jax.ShapeDtypeStruct((r, m), F32),
        grid=(1,),
        in_specs=[_resident(x.shape), _resident(w_bf16.shape)],
        out_specs=_resident((r, m)),
        name="plain_proj",
    )(x, w_bf16)


def _rw_prep_kernel(chunk, rw_ref, prev0_ref, mu_ref, w0_ref, a0_ref, kk_ref, ka_ref, rk_ref,
                    wup_ref, aup_ref, gup_ref,
                    rt_ref, at_ref, bt_ref, kt_ref, bh_ref, kh_ref, v_ref, g_ref, bonus_ref, wc_ref,
                    carry_ref):
    tt = rw_ref.shape[1]

    @pl.when(pl.program_id(1) == 0)
    def _():
        carry_ref[...] = prev0_ref[0]

    p = rw_ref[0]
    prev = jnp.where(_iota(p.shape, 0) == 0, carry_ref[...], pltpu.roll(p, 1, 0))
    carry_ref[...] = p[tt - 1:tt]
    xs = p + mu_ref[...] * (prev - p)
    r = xs[:, 0:D_MIX]
    k = xs[:, D_MIX:2 * D_MIX]
    v = xs[:, 2 * D_MIX:3 * D_MIX]
    lo = 3 * D_MIX
    wd = xs[:, lo:lo + LORA_W]
    ad = xs[:, lo + LORA_W:lo + LORA_W + LORA_A]
    gd = xs[:, lo + LORA_W + LORA_A:D_RW_COLS]

    pre_w = w0_ref[...] + _dot(jnp.tanh(wd), wup_ref[...])
    w_log = -jax.nn.softplus(-pre_w) - 0.5
    logw = -jnp.exp(w_log)
    a = jax.nn.sigmoid(a0_ref[...] + _dot(ad, aup_ref[...]))
    g_ref[0] = _dot(jax.nn.sigmoid(gd), gup_ref[...])

    head_ones = _block_ones(D_MIX, HEAD_DIM)
    kk = k * kk_ref[...]
    ss = _dot_exact_rhs(kk * kk, head_ones, n=2)
    kk = kk * jnp.minimum(lax.rsqrt(ss), 1e12)
    k2 = k * (1.0 + (a - 1.0) * ka_ref[...])
    nb = -(kk * a)
    bonus_ref[0] = _dot_exact_rhs(r * k2 * rk_ref[...], head_ones, n=2) * v
    v_ref[0] = v.astype(BF16)

    ti, tj = _iota((tt, tt), 0), _iota((tt, tt), 1)
    same = ti // chunk == tj // chunk
    cum = _dot_exact_lhs((same & (tj <= ti)).astype(BF16), logw)
    tot = jnp.concatenate(
        [jnp.broadcast_to(cum[(c + 1) * chunk - 1:(c + 1) * chunk], (chunk, D_MIX)) for c in range(tt // chunk)],
        axis=0)
    e_in = jnp.exp(cum)
    e_out = jnp.exp(-cum)
    e_end = jnp.exp(tot - cum)
    rt_ref[0] = (r * e_in).astype(BF16)
    at_ref[0] = (kk * jnp.exp(cum - logw)).astype(BF16)
    bt_ref[0] = (nb * e_out).astype(BF16)
    kt_ref[0] = (k2 * e_out).astype(BF16)
    bh_ref[0] = (nb * e_end).astype(BF16)
    kh_ref[0] = (k2 * e_end).astype(BF16)
    etot = jnp.exp(tot)
    for c in range(tt // chunk):
        wc_ref[0, c] = etot[c * chunk:c * chunk + 1]


def rw_prep(rw, prev0, p, chunk, tt):
    b, t, _ = rw.shape
    seq = lambda width, dt: jax.ShapeDtypeStruct((b, t, width), dt)
    tile = lambda width: pl.BlockSpec((1, tt, width), lambda i, j: (i, j, 0))
    vec = lambda width: _resident((1, width))
    n_c = tt // chunk
    return pl.pallas_call(
        functools.partial(_rw_prep_kernel, chunk),
        out_shape=tuple(seq(D_MIX, BF16) for _ in range(7))
        + (seq(D_MIX, F32), seq(D_MIX, F32), jax.ShapeDtypeStruct((b, t // chunk, 1, D_MIX), F32)),
        grid=(b, t // tt),
        in_specs=[tile(D_RW_COLS), pl.BlockSpec((1, 1, D_RW_COLS), lambda i, j: (i, 0, 0)),
                  vec(D_RW_COLS), vec(D_MIX), vec(D_MIX), vec(D_MIX), vec(D_MIX), vec(D_MIX),
                  _resident((LORA_W, D_MIX)), _resident((LORA_A, D_MIX)), _resident((LORA_G, D_MIX))],
        out_specs=tuple(tile(D_MIX) for _ in range(9))
        + (pl.BlockSpec((1, n_c, 1, D_MIX), lambda i, j: (i, j, 0, 0)),),
        scratch_shapes=[pltpu.VMEM((1, D_RW_COLS), F32)],
        compiler_params=_cparams("parallel", "arbitrary"),
        name="rw_prep",
    )(rw, prev0, p["mu"], p["w0"], p["a0"], p["k_k"], p["k_a"], p["r_k"],
      p["w_up"], p["a_up"], p["g_up"])


def _unit_lower_inverse(a, chunk):
    shape = a.shape
    eye = (_iota(shape, 1) % chunk == _iota(shape, 0)).astype(F32)
    res = eye + a
    power = a.astype(BF16)
    for _ in range(chunk.bit_length() - 2):
        power = _dot(power, _bd_rows(power, RW_GROUP, chunk)).astype(BF16)
        yield
        res = res + _dot(res, _bd_rows(power, RW_GROUP, chunk))
    return res


def _interleave(chains):
    live = list(chains)
    while live:
        live = [c for c in live if next(c, StopIteration) is not StopIteration]


def _rw_chunk_kernel(chunk, rt_ref, at_ref, bt_ref, kt_ref, bh_ref, kh_ref, v_ref, g_ref, bonus_ref,
                     wc_ref, s0_ref, gnw_ref, gnb_ref, o_ref, s_out_ref, st_ref):
    c = pl.program_id(1)

    @pl.when(c == 0)
    def _():
        st_ref[...] = s0_ref[...]

    cat = (chunk, RW_GROUP * chunk)
    col_t = _iota(cat, 1) % chunk
    strict = col_t < _iota(cat, 0)
    incl = col_t <= _iota(cat, 0)
    head_ones = _block_ones(RW_LANES, HEAD_DIM)
    bd = lambda x: _bd_rows(x, RW_GROUP, HEAD_DIM)
    n_cat = RW_GROUP * chunk
    def chain(bi, grp):
        sl = slice(grp * RW_LANES, (grp + 1) * RW_LANES)
        rt, at, bt, kt = rt_ref[bi, :, sl], at_ref[bi, :, sl], bt_ref[bi, :, sl], kt_ref[bi, :, sl]
        bh, kh, v = bh_ref[bi, :, sl], kh_ref[bi, :, sl], v_ref[bi, :, sl]
        st = st_ref[bi, :, sl]

        gram = _dot(jnp.concatenate([at, rt], axis=0),
                    jnp.concatenate([bd(bt), bd(kt)], axis=0), _NT)
        yield
        zero = jnp.zeros(cat, F32)
        a_ab = jnp.where(strict, gram[:chunk, :n_cat], zero)
        a_ak = jnp.where(strict, gram[:chunk, n_cat:], zero)
        a_r = jnp.concatenate([jnp.where(incl, gram[chunk:, :n_cat], zero),
                               jnp.where(incl, gram[chunk:, n_cat:], zero)], axis=1).astype(BF16)
        v_bd = bd(v)
        x = _dot(a_ak, v_bd).astype(BF16)
        t_inv = yield from _unit_lower_inverse(a_ab, chunk)
        yield
        ua = _dot(t_inv, jnp.concatenate([bd(x), bd(at)], axis=1))
        st_bd = bd(st.astype(BF16))
        yield
        from_state = _dot(jnp.concatenate([ua[:, RW_LANES:].astype(BF16), rt], axis=0), st_bd)
        yield
        u = (ua[:, :RW_LANES] + from_state[:chunk]).astype(BF16)
        y = from_state[chunk:] + _dot(a_r, jnp.concatenate([bd(u), v_bd], axis=0))
        m = _dot(jnp.concatenate([bh, kh], axis=0), jnp.concatenate([u, v], axis=0), _TN)
        yield
        lane_head = _iota((HEAD_DIM, RW_LANES), 1) // HEAD_DIM
        fold = jnp.zeros((HEAD_DIM, RW_LANES), F32)
        for h in range(RW_GROUP):
            fold = fold + jnp.where(lane_head == h, m[h * HEAD_DIM:(h + 1) * HEAD_DIM], 0.0)
        diag = _iota((HEAD_DIM, RW_LANES), 1) % HEAD_DIM == _iota((HEAD_DIM, RW_LANES), 0)
        w_rows = jnp.where(diag, wc_ref[bi, 0, :, sl], 0.0)
        w_t = _dot_exact_rhs(w_rows, head_ones)
        st_ref[bi, :, sl] = st * w_t + fold

        mean = _dot_exact_rhs(y, head_ones, n=2) * (1.0 / HEAD_DIM)
        yield
        d = y - mean
        var = _dot_exact_rhs(d * d, head_ones, n=2) * (1.0 / HEAD_DIM)
        yield
        yn = d * lax.rsqrt(var + GN_EPS) * gnw_ref[:, sl] + gnb_ref[:, sl]
        o_ref[bi, :, sl] = ((yn + bonus_ref[bi, :, sl]) * g_ref[bi, :, sl]).astype(BF16)

    _interleave(chain(bi, grp) for bi in range(st_ref.shape[0]) for grp in range(N_HEADS // RW_GROUP))

    @pl.when(c == pl.num_programs(1) - 1)
    def _():
        s_out_ref[...] = st_ref[...]


def rw_chunks(prep, s0_t, gn_w, gn_b, chunk):
    rt, at, bt, kt, bh, kh, v, g, bonus, wc = prep
    b, t, _ = rt.shape
    nb = RW_SEQS if b % RW_SEQS == 0 else 1
    tile = pl.BlockSpec((nb, chunk, D_MIX), lambda i, j: (i, j, 0))
    state = pl.BlockSpec((nb, HEAD_DIM, D_MIX), lambda i, j: (i, 0, 0))
    return pl.pallas_call(
        functools.partial(_rw_chunk_kernel, chunk),
        out_shape=(jax.ShapeDtypeStruct((b, t, D_MIX), BF16),
                   jax.ShapeDtypeStruct((b, HEAD_DIM, D_MIX), F32)),
        grid=(b // nb, t // chunk),
        in_specs=[tile] * 9 + [pl.BlockSpec((nb, 1, 1, D_MIX), lambda i, j: (i, j, 0, 0)), state,
                               _resident((1, D_MIX)), _resident((1, D_MIX))],
        out_specs=(tile, state),
        scratch_shapes=[pltpu.VMEM((nb, HEAD_DIM, D_MIX), F32)],
        compiler_params=_cparams("parallel", "arbitrary"),
        name="rw_chunks",
    )(rt, at, bt, kt, bh, kh, v, g, bonus, wc, s0_t, gn_w, gn_b)


SB_PAIR = 2 * HEAD_DIM
SB_PAST_BLOCK = 256
SB_VISITS = 4


def _sb_kernel(tq, n_past, n_q, q_ref, kn_ref, vn_ref, *rest):
    if n_past:
        kp_ref, vp_ref, o_ref, acc_ref, c_ref = rest
    else:
        o_ref, acc_ref, c_ref = rest
    qi = pl.program_id(2)
    q2 = q_ref[0] * (HEAD_DIM ** -0.5)
    lane = _iota(q2.shape, 1)
    zero_q = jnp.zeros_like(q2)
    q_st = jnp.concatenate([jnp.where(lane < HEAD_DIM, q2, zero_q), jnp.where(lane >= HEAD_DIM, q2, zero_q)],
                           axis=0)
    acc_ref[...] = jnp.zeros_like(acc_ref)
    c_ref[...] = jnp.zeros_like(c_ref)

    def visit(k_blk, v_blk, masked):
        tk = k_blk.shape[0]
        later = (_iota((tk, tk), 0) > _iota((tk, tk), 1)).astype(BF16)
        z = _dot(q_st, k_blk, _NT)
        sp = jnp.maximum(z, 0.0) + jnp.log(1.0 + jnp.exp(-jnp.abs(z)))
        if masked:
            before = _iota((2 * tq, tk), 1) < _iota((2 * tq, tk), 0) % tq
            drop = jnp.where(before, sp, 0.0)
        else:
            drop = sp
        srev = _dot(drop, later)
        yield
        c = c_ref[...]
        att = jnp.exp((z - sp) - (c + srev))
        if masked:
            att = jnp.where(before, att, 0.0)
        acc_ref[...] += _dot(att, v_blk)
        c_ref[...] = c + (srev[:, 0:1] + drop[:, 0:1])

    def new_block(blk, masked=False):
        s = pl.multiple_of(blk * tq, tq)
        return visit(kn_ref[0, pl.ds(s, tq), :], vn_ref[0, pl.ds(s, tq), :], masked)

    lead = SB_VISITS - 1
    for n_with in range(min(lead, n_q - 1) + 1):
        @pl.when((qi == n_with) if n_with < lead else (qi >= lead))
        def _():
            _interleave([new_block(qi, masked=True)] + [new_block(qi - 1 - u) for u in range(n_with)])

    left = jnp.maximum(qi - lead, 0)

    def earlier_group(i, carry):
        _interleave(new_block(left - 1 - u - SB_VISITS * i) for u in range(SB_VISITS))
        return carry

    lax.fori_loop(0, left // SB_VISITS, earlier_group, 0)
    rest = left % SB_VISITS

    @pl.when(rest >= 2)
    def _():
        _interleave([new_block(rest - 1), new_block(rest - 2)])

    @pl.when(rest % 2 == 1)
    def _():
        _interleave([new_block(0)])

    if n_past:
        def past_block(blk):
            s = pl.multiple_of(blk * SB_PAST_BLOCK, SB_PAST_BLOCK)
            return visit(kp_ref[0, pl.ds(s, SB_PAST_BLOCK), :], vp_ref[0, pl.ds(s, SB_PAST_BLOCK), :], False)

        def past_group(i, carry):
            _interleave(past_block(n_past - 1 - u - SB_VISITS * i) for u in range(SB_VISITS))
            return carry

        lax.fori_loop(0, n_past // SB_VISITS, past_group, 0)
        _interleave(past_block(blk) for blk in reversed(range(n_past % SB_VISITS)))

    o_ref[0] = jnp.where(lane < HEAD_DIM, acc_ref[:tq], acc_ref[tq:]).astype(BF16)


def stick_breaking(q, k_new, v_new, k_past, v_past, tq):
    b, t, _ = q.shape
    n_pairs = D_MIX // SB_PAIR
    qspec = pl.BlockSpec((1, tq, SB_PAIR), lambda i, p, j: (i, j, p))
    seq = lambda length: pl.BlockSpec((1, length, SB_PAIR), lambda i, p, j: (i, 0, p))
    args, specs, n_past = [q, k_new, v_new], [qspec, seq(t), seq(t)], 0
    if k_past is not None:
        past_len = k_past.shape[1]
        n_past = past_len // SB_PAST_BLOCK
        args += [k_past, v_past]
        specs += [seq(past_len), seq(past_len)]
    return pl.pallas_call(
        functools.partial(_sb_kernel, tq, n_past, t // tq),
        out_shape=jax.ShapeDtypeStruct((b, t, D_MIX), BF16),
        grid=(b, n_pairs, t // tq),
        in_specs=specs,
        out_specs=qspec,
        scratch_shapes=[pltpu.VMEM((2 * tq, SB_PAIR), F32), pltpu.VMEM((2 * tq, 1), F32)],
        compiler_params=_cparams("parallel", "parallel", "arbitrary"),
        name="stick_breaking",
    )(*args)


def _merge_kernel(orw_ref, osb_ref, gate_ref, x_ref, wb0_ref, wb1_ref, wout_ref, gx_ref, wxq_ref,
                  x1_ref, qx_ref):
    g = jax.nn.sigmoid(gate_ref[...].astype(F32))
    mixed = (g[:, :D_MODEL] * jnp.dot(orw_ref[...], wb0_ref[...], preferred_element_type=F32)
             + g[:, D_MODEL:] * jnp.dot(osb_ref[...], wb1_ref[...], preferred_element_type=F32))
    x1 = x_ref[...] + _dot(mixed, wout_ref[...])
    x1_ref[...] = x1
    qx_ref[...] = _dot(_rms(x1, gx_ref[...]), wxq_ref[...]).astype(BF16)


def merge_out(o_rw, o_sb, gate, x, wb0, wb1, wout, gx, wxq):
    n = x.shape[0]
    tm = min(ROW_TILE, n)
    row = lambda width: pl.BlockSpec((tm, width), lambda i: (i, 0))
    return pl.pallas_call(
        _merge_kernel,
        out_shape=(jax.ShapeDtypeStruct((n, D_MODEL), F32), jax.ShapeDtypeStruct((n, D_MODEL), BF16)),
        grid=(n // tm,),
        in_specs=[row(D_MIX), row(D_MIX), row(GATE_COLS), row(D_MODEL),
                  _resident((D_MIX, D_MODEL)), _resident((D_MIX, D_MODEL)), _resident((D_MODEL, D_MODEL)),
                  _resident((1, D_MODEL)), _resident((D_MODEL, D_MODEL))],
        out_specs=(row(D_MODEL), row(D_MODEL)),
        compiler_params=_cparams("parallel"),
        name="merge_out",
    )(o_rw, o_sb, gate, x, wb0, wb1, wout, gx.reshape(1, D_MODEL), wxq)


def _mem_kv_kernel(m_ref, g_ref, wk_ref, wv_ref, k_ref, v_ref):
    h = _rms(m_ref[...], g_ref[...]).astype(BF16)
    k_ref[...] = jnp.dot(h, wk_ref[...], preferred_element_type=F32)
    v_ref[...] = jnp.dot(h, wv_ref[...], preferred_element_type=F32)


def mem_kv(mem, g, wk, wv):
    n = mem.shape[0]
    tm = min(ROW_TILE, n)
    row = pl.BlockSpec((tm, D_MODEL), lambda i: (i, 0))
    return pl.pallas_call(
        _mem_kv_kernel,
        out_shape=(jax.ShapeDtypeStruct((n, D_MODEL), F32),) * 2,
        grid=(n // tm,),
        in_specs=[row, _resident((1, D_MODEL)), _resident((D_MODEL, D_MODEL)), _resident((D_MODEL, D_MODEL))],
        out_specs=(row, row),
        compiler_params=_cparams("parallel"),
        name="mem_kv",
    )(mem, g.reshape(1, D_MODEL), wk, wv)


ROUTER_LANES = 128


def _xattn_kernel(q_ref, x1_ref, mk_ref, mv_ref, wxo_ref, gf_ref, wr_ref, br_ref,
                  x2_ref, hf_ref, idx_ref, gates_ref, counts_ref):
    q = q_ref[0]
    heads = [None] * N_X_HEADS

    def head(h):
        sl = slice(h * X_HEAD_DIM, (h + 1) * X_HEAD_DIM)
        s = _dot(q[:, sl], mk_ref[0, :, sl], _NT) * (X_HEAD_DIM ** -0.5)
        yield
        e = jnp.exp(s - jnp.max(s, axis=-1, keepdims=True))
        p = e * (1.0 / jnp.sum(e, axis=-1, keepdims=True))
        heads[h] = _dot(p, mv_ref[0, :, sl])

    _interleave(head(h) for h in range(N_X_HEADS))
    x2 = x1_ref[0] + _dot(jnp.concatenate(heads, axis=1), wxo_ref[...])
    x2_ref[0] = x2
    hf = _rms(x2, gf_ref[...])
    hf_ref[0] = hf.reshape(hf_ref.shape[1:])
    logits = _dot3(hf, wr_ref[...]) + br_ref[...]
    lane = _iota(logits.shape, 1)
    vals, idx_out = [], jnp.zeros(logits.shape, jnp.int32)
    chosen = jnp.zeros(logits.shape, F32)
    for j in range(TOP_K):
        m = jnp.max(logits, axis=-1, keepdims=True)
        pick = jnp.min(jnp.where(logits == m, lane, ROUTER_LANES), axis=-1, keepdims=True)
        vals.append(m)
        idx_out = jnp.where(lane == j, pick, idx_out)
        chosen = chosen + (lane == pick).astype(F32)
        logits = jnp.where(lane == pick, -jnp.inf, logits)

    @pl.when((pl.program_id(0) == 0) & (pl.program_id(1) == 0))
    def _():
        counts_ref[...] = jnp.zeros_like(counts_ref)

    counts_ref[...] += jnp.broadcast_to(jnp.sum(chosen, axis=0, keepdims=True), counts_ref.shape)
    exps = [jnp.exp(v - vals[0]) for v in vals]
    denom = exps[0] + exps[1] + exps[2] + exps[3]
    gates = jnp.zeros(logits.shape, F32)
    for j in range(TOP_K):
        gates = jnp.where(lane == j, exps[j] / denom, gates)
    idx_ref[0] = idx_out
    gates_ref[0] = gates


def xattn_router(qx, x1, mk, mv, wxo, g_ffn, w_router, b_router):
    b, t, _ = x1.shape
    tm = min(ROW_TILE, t)
    n_mem = mk.shape[1]
    wr = jnp.zeros((D_MODEL, ROUTER_LANES), F32).at[:, :N_EXPERTS].set(w_router)
    br = jnp.full((1, ROUTER_LANES), -jnp.inf, F32).at[0, :N_EXPERTS].set(b_router)
    tile = pl.BlockSpec((1, tm, D_MODEL), lambda i, j: (i, j, 0))
    mem = pl.BlockSpec((1, n_mem, D_MODEL), lambda i, j: (i, 0, 0))
    small = pl.BlockSpec((1, tm, ROUTER_LANES), lambda i, j: (i, j, 0))
    return pl.pallas_call(
        _xattn_kernel,
        out_shape=(jax.ShapeDtypeStruct((b, t, D_MODEL), F32), jax.ShapeDtypeStruct((b, t) + ROW_TILE_SHAPE, F32),
                   jax.ShapeDtypeStruct((b, t, ROUTER_LANES), jnp.int32),
                   jax.ShapeDtypeStruct((b, t, ROUTER_LANES), F32),
                   jax.ShapeDtypeStruct((ROUTE_SUBLANES, ROUTER_LANES), F32)),
        grid=(b, t // tm),
        in_specs=[tile, tile, mem, mem, _resident((D_MODEL, D_MODEL)), _resident((1, D_MODEL)),
                  _resident((D_MODEL, ROUTER_LANES)), _resident((1, ROUTER_LANES))],
        out_specs=(tile, pl.BlockSpec((1, tm) + ROW_TILE_SHAPE, lambda i, j: (i, j, 0, 0)), small, small,
                   _resident((ROUTE_SUBLANES, ROUTER_LANES))),
        compiler_params=_cparams("arbitrary", "arbitrary"),
        name="xattn_router",
    )(qx, x1, mk, mv, wxo, g_ffn.reshape(1, D_MODEL), wr, br)


ROW_GROUP = 8
ROUTE_SUBLANES = 8


def _route_kernel(idx_ref, total_ref, dest_ref, carry_ref, start_ref):
    tm = idx_ref.shape[0]
    idx = idx_ref[...]
    lane = _iota((tm, ROUTER_LANES), 1)
    picks = [idx[:, j:j + 1] for j in range(TOP_K)]
    onehot = jnp.zeros((tm, ROUTER_LANES), F32)
    for pick in picks:
        onehot = onehot + (lane == pick).astype(F32)

    @pl.when(pl.program_id(0) == 0)
    def _():
        carry_ref[...] = jnp.zeros_like(carry_ref)
        padded = jnp.floor((total_ref[...] + (MOE_BLOCK - 1.0)) * (1.0 / MOE_BLOCK)) * MOE_BLOCK
        earlier = (_iota((ROUTER_LANES, ROUTER_LANES), 0) < _iota((ROUTER_LANES, ROUTER_LANES), 1))
        start_ref[...] = _dot_exact_rhs(padded, earlier.astype(BF16))

    below = (_iota((tm, tm), 1) < _iota((tm, tm), 0)).astype(BF16)
    row = _dot(below, onehot) + (carry_ref[0:1, :] + start_ref[0:1, :])
    dest = jnp.zeros((tm, ROUTER_LANES), F32)
    for j, pick in enumerate(picks):
        mine = jnp.sum(jnp.where(lane == pick, row, 0.0), axis=1, keepdims=True)
        dest = jnp.where(lane == j, mine, dest)
    dest_ref[0] = dest.T[:ROUTE_SUBLANES].astype(jnp.int32)
    carry_ref[...] += jnp.broadcast_to(jnp.sum(onehot, axis=0, keepdims=True), carry_ref.shape)


def moe_route(idx, total, tm):
    n = idx.shape[0]
    n_tiles = n // tm
    return pl.pallas_call(
        _route_kernel,
        out_shape=jax.ShapeDtypeStruct((n_tiles, ROUTE_SUBLANES, tm), jnp.int32),
        grid=(n_tiles,),
        in_specs=[pl.BlockSpec((tm, ROUTER_LANES), lambda t: (t, 0)), _resident((ROUTE_SUBLANES, ROUTER_LANES))],
        out_specs=pl.BlockSpec((1, ROUTE_SUBLANES, tm), lambda t: (t, 0, 0)),
        scratch_shapes=[pltpu.VMEM((ROUTE_SUBLANES, ROUTER_LANES), F32),
                        pltpu.VMEM((ROUTE_SUBLANES, ROUTER_LANES), F32)],
        compiler_params=_cparams("arbitrary"),
        name="moe_route",
    )(idx, total)


def _block_tables(counts, n_blocks):
    cnt = counts[0, :N_EXPERTS].astype(jnp.int32)
    padded = (cnt + MOE_BLOCK - 1) // MOE_BLOCK * MOE_BLOCK
    pends = jnp.cumsum(padded).astype(jnp.int32)
    block_start = jnp.arange(n_blocks, dtype=jnp.int32) * MOE_BLOCK
    block_expert = jnp.minimum(jnp.sum((pends[None, :] <= block_start[:, None]).astype(jnp.int32), axis=1),
                               N_EXPERTS - 1)
    n_used = (pends[-1:] // MOE_BLOCK).astype(jnp.int32)
    return pends, block_expert, n_used


def _dispatch_kernel(bounds, pends_ref, dest_ref, *rest):
    hf_refs = rest[:len(bounds) - 1]
    xs_hbm, zbuf, sem = rest[len(bounds) - 1:]
    i = pl.program_id(0)

    @pl.when(i == 0)
    def _():
        zbuf[...] = jnp.zeros_like(zbuf)
        n_blocks = xs_hbm.shape[0] // MOE_BLOCK
        n_used = pends_ref[N_EXPERTS - 1] // MOE_BLOCK
        for wait in (False, True):
            def unused_block(blk, carry):
                cp = pltpu.make_async_copy(
                    zbuf, xs_hbm.at[pl.ds(pl.multiple_of(blk * MOE_BLOCK, MOE_BLOCK), MOE_BLOCK)], sem)
                cp.wait() if wait else cp.start()
                return carry

            lax.fori_loop(n_used, n_blocks, unused_block, 0)
            for e in range(N_EXPERTS):
                end = pends_ref[e]
                begin = pends_ref[e - 1] if e else 0

                @pl.when(end > begin)
                def _():
                    last = pl.multiple_of(end - MOE_BLOCK, MOE_BLOCK)
                    cp = pltpu.make_async_copy(zbuf, xs_hbm.at[pl.ds(last, MOE_BLOCK)], sem)
                    cp.wait() if wait else cp.start()

    for hf_ref, lo, hi in zip(hf_refs, bounds[:-1], bounds[1:]):
        @pl.when((i >= lo) & (i < hi))
        def _():
            tm = hf_ref.shape[0]
            for j in range(TOP_K):
                def issue(g, carry):
                    base = pl.multiple_of(g * ROW_GROUP, ROW_GROUP)
                    rows = hf_ref.at[pl.ds(base, ROW_GROUP)]
                    for u in range(ROW_GROUP):
                        pltpu.make_async_copy(rows.at[pl.ds(u, 1)], xs_hbm.at[pl.ds(dest_ref[j * tm + base + u], 1)],
                                              sem).start()
                    return carry

                lax.fori_loop(0, tm // ROW_GROUP, issue, 0)
            for j in range(TOP_K):
                pltpu.make_async_copy(hf_ref, xs_hbm.at[pl.ds(0, tm)], sem).wait()


def moe_dispatch(hfs, bounds, dest, pends, n_rows):
    n_tiles, _, tm = dest.shape
    hf_spec = lambda lo, hi: pl.BlockSpec((tm,) + ROW_TILE_SHAPE,
                                          lambda i, pe: (jnp.clip(i - lo, 0, hi - lo - 1), 0, 0))
    grid_spec = pltpu.PrefetchScalarGridSpec(
        num_scalar_prefetch=1,
        grid=(n_tiles,),
        in_specs=[pl.BlockSpec((ROUTE_SUBLANES * tm,), lambda i, pe: (i,), memory_space=pltpu.SMEM)]
        + [hf_spec(lo, hi) for lo, hi in zip(bounds[:-1], bounds[1:])],
        out_specs=pl.BlockSpec(memory_space=pl.ANY),
        scratch_shapes=[pltpu.VMEM((MOE_BLOCK,) + ROW_TILE_SHAPE, F32), pltpu.SemaphoreType.DMA(())],
    )
    return pl.pallas_call(
        functools.partial(_dispatch_kernel, tuple(bounds)),
        out_shape=jax.ShapeDtypeStruct((n_rows,) + ROW_TILE_SHAPE, F32),
        grid_spec=grid_spec,
        compiler_params=_cparams_gather(),
        name="moe_dispatch",
    )(pends, dest.reshape(-1), *hfs)


W1_CHUNKS = 4
W2_CHUNKS = 2


def _moe_kernel(be_ref, nused_ref, x_ref, *rest):
    w1 = rest[:W1_CHUNKS]
    b1_ref = rest[W1_CHUNKS]
    w2 = rest[W1_CHUNKS + 1:W1_CHUNKS + 1 + W2_CHUNKS]
    b2_ref, o_ref, w1_bf, w2_bf = rest[W1_CHUNKS + 1 + W2_CHUNKS:]
    i = pl.program_id(0)
    half = W1_CHUNKS // 2
    wide = 2 * D_FF // W1_CHUNKS

    @pl.when((i < nused_ref[0]) & ((i == 0) | (be_ref[i] != be_ref[jnp.maximum(i - 1, 0)])))
    def _():
        for c in range(W1_CHUNKS):
            w1_bf[c] = w1[c][0].astype(BF16)
        for c in range(W2_CHUNKS):
            w2_bf[c] = w2[c][0].astype(BF16)

    @pl.when(i < nused_ref[0])
    def _():
        x = x_ref[...].reshape(MOE_BLOCK, D_MODEL).astype(BF16)
        parts = [None] * half

        def ff_slice(c):
            glu = jnp.dot(x, w1_bf[c], preferred_element_type=F32) + b1_ref[0, :, c * wide:(c + 1) * wide]
            lin = (jnp.dot(x, w1_bf[half + c], preferred_element_type=F32)
                   + b1_ref[0, :, D_FF + c * wide:D_FF + (c + 1) * wide])
            yield
            glu = jnp.minimum(glu, SWIGLU_LIMIT)
            lin = jnp.clip(lin, -SWIGLU_LIMIT, SWIGLU_LIMIT)
            act = glu * jax.nn.sigmoid(SWIGLU_ALPHA * glu) * (lin + 1.0)
            parts[c] = _dot(act, w2_bf[c])

        _interleave(ff_slice(c) for c in range(half))
        o_ref[...] = (b2_ref[0] + sum(parts[1:], parts[0])).reshape(o_ref.shape)

    @pl.when(i >= nused_ref[0])
    def _():
        o_ref[...] = jnp.zeros_like(o_ref)


def moe_experts(xs, block_expert, n_used, w1, b1, w2, b2):
    n_blocks = block_expert.shape[0]
    assert W1_CHUNKS // 2 == W2_CHUNKS
    c1 = 2 * D_FF // W1_CHUNKS
    c2 = D_FF // W2_CHUNKS
    w1_spec = lambda c: pl.BlockSpec((1, D_MODEL, c1), lambda i, be, nu: (be[i], 0, c))
    w2_spec = lambda c: pl.BlockSpec((1, c2, D_MODEL), lambda i, be, nu: (be[i], c, 0))
    grid_spec = pltpu.PrefetchScalarGridSpec(
        num_scalar_prefetch=2,
        grid=(n_blocks,),
        in_specs=[pl.BlockSpec((MOE_BLOCK,) + ROW_TILE_SHAPE, lambda i, be, nu: (jnp.minimum(i, nu[0] - 1), 0, 0))]
        + [w1_spec(c) for c in range(W1_CHUNKS)]
        + [pl.BlockSpec((1, 1, 2 * D_FF), lambda i, be, nu: (be[i], 0, 0))]
        + [w2_spec(c) for c in range(W2_CHUNKS)]
        + [pl.BlockSpec((1, 1, D_MODEL), lambda i, be, nu: (be[i], 0, 0))],
        out_specs=pl.BlockSpec((MOE_BLOCK,) + ROW_TILE_SHAPE, lambda i, be, nu: (i, 0, 0)),
        scratch_shapes=[pltpu.VMEM((W1_CHUNKS, D_MODEL, c1), BF16), pltpu.VMEM((W2_CHUNKS, c2, D_MODEL), BF16)],
    )
    return pl.pallas_call(
        _moe_kernel,
        out_shape=jax.ShapeDtypeStruct((n_blocks * MOE_BLOCK,) + ROW_TILE_SHAPE, F32),
        grid_spec=grid_spec,
        compiler_params=_cparams("arbitrary"),
        name="moe_experts",
    )(block_expert, n_used, xs, *([w1] * W1_CHUNKS), b1.reshape(N_EXPERTS, 1, 2 * D_FF),
      *([w2] * W2_CHUNKS), b2.reshape(N_EXPERTS, 1, D_MODEL))


def _start_combine_gather(dest_ref, os_hbm, dst, sem):
    tm = dst.shape[0] // TOP_K
    for j in range(TOP_K):
        def issue(g, carry):
            base = pl.multiple_of(g * ROW_GROUP, ROW_GROUP)
            rows = dst.at[pl.ds(j * tm + base, ROW_GROUP)]
            for u in range(ROW_GROUP):
                pltpu.make_async_copy(os_hbm.at[pl.ds(dest_ref[j * tm + base + u], 1)], rows.at[pl.ds(u, 1)],
                                      sem).start()
            return carry

        lax.fori_loop(0, tm // ROW_GROUP, issue, 0)


def _combine_kernel(n_tiles, dest_ref, dest_next_ref, os_hbm, x2_ref, gates_ref, gfin_ref, y_ref, buf, sem):
    tm = x2_ref.shape[0]
    i = pl.program_id(0)
    slot = i % 2

    @pl.when(i == 0)
    def _():
        _start_combine_gather(dest_ref, os_hbm, buf.at[0], sem.at[0])

    if n_tiles > 1:
        @pl.when(i + 1 < n_tiles)
        def _():
            _start_combine_gather(dest_next_ref, os_hbm, buf.at[1 - slot], sem.at[1 - slot])

    pltpu.make_async_copy(os_hbm.at[pl.ds(0, TOP_K * tm)], buf.at[slot], sem.at[slot]).wait()
    y = x2_ref[...]
    gates = gates_ref[...]
    for j in range(TOP_K):
        y = y + gates[:, j:j + 1] * buf[slot, j * tm:(j + 1) * tm].reshape(tm, D_MODEL)
    y_ref[...] = _rms(y, gfin_ref[...])


def moe_combine(os_rows, dest, x2, gates, g_final):
    n = x2.shape[0]
    n_tiles, _, tm = dest.shape
    dest_spec = lambda step: pl.BlockSpec(
        (ROUTE_SUBLANES * tm,), lambda i: (jnp.minimum(i + step, n_tiles - 1),), memory_space=pltpu.SMEM)
    dest = dest.reshape(-1)
    return pl.pallas_call(
        functools.partial(_combine_kernel, n_tiles),
        out_shape=jax.ShapeDtypeStruct((n, D_MODEL), F32),
        grid=(n_tiles,),
        in_specs=[dest_spec(0), dest_spec(1),
                  pl.BlockSpec(memory_space=pl.ANY),
                  pl.BlockSpec((tm, D_MODEL), lambda i: (i, 0)),
                  pl.BlockSpec((tm, ROUTER_LANES), lambda i: (i, 0)),
                  _resident((1, D_MODEL))],
        out_specs=pl.BlockSpec((tm, D_MODEL), lambda i: (i, 0)),
        scratch_shapes=[pltpu.VMEM((2, TOP_K * tm) + ROW_TILE_SHAPE, F32), pltpu.SemaphoreType.DMA((2,))],
        compiler_params=_cparams_gather(),
        name="moe_combine",
    )(dest, dest, os_rows, x2, gates, g_final.reshape(1, D_MODEL))


def moe_ffn(groups, w):
    sizes = [g[0].shape[0] for g in groups]
    n = sum(sizes)
    tm = next(c for c in (256, 128, 64, 32, 16, 8) if all(s % c == 0 for s in sizes))
    counts = sum(g[4] for g in groups[1:]) + groups[0][4]
    dest = moe_route(jnp.concatenate([g[2] for g in groups], axis=0), counts, tm)
    n_blocks = -(-(n * TOP_K + N_EXPERTS * (MOE_BLOCK - 1)) // MOE_BLOCK)
    pends, block_expert, n_used = _block_tables(counts, n_blocks)
    bounds = [0]
    for s in sizes:
        bounds.append(bounds[-1] + s // tm)
    xs = moe_dispatch([g[1] for g in groups], bounds, dest, pends, n_blocks * MOE_BLOCK)
    os_rows = moe_experts(xs, block_expert, n_used, w["w_e1"], w["b_e1"], w["w_e2"], w["b_e2"])
    return [moe_combine(os_rows, dest[lo:hi], g[0], g[3], w["g_final"])
            for g, lo, hi in zip(groups, bounds[:-1], bounds[1:])]


def _mixers(x, prev0, s0_t, k_past, v_past, mk, mv, w, chunk, prep_tile, tq):
    b, t, _ = x.shape
    n = b * t
    xf = x.reshape(n, D_MODEL)
    q, k_sb, v_sb, rw, gate, k_heads, v_heads = in_proj(xf, w["g_mix"], w["w_in"])
    shift = norm_rows(x[:, -1], w["g_mix"])
    prep = rw_prep(rw.reshape(b, t, D_RW_COLS), prev0, w["rw"], chunk, prep_tile)
    o_rw, s_t = rw_chunks(prep, s0_t, w["gn_w"], w["gn_b"], chunk)
    seq = lambda a: a.reshape(b, t, D_MIX)
    o_sb = stick_breaking(seq(q), seq(k_sb), seq(v_sb), k_past, v_past, tq)
    x1, qx = merge_out(o_rw.reshape(n, D_MIX), o_sb.reshape(n, D_MIX), gate, xf,
                       w["wb0"], w["wb1"], w["w_out"], w["g_xattn"], w["w_xq"])
    x2, hf, idx, gates, counts = xattn_router(qx.reshape(b, t, D_MODEL), x1.reshape(b, t, D_MODEL), mk, mv,
                                              w["w_xo"], w["g_ffn"], w["w_router"], w["b_router"])
    tok = (x2.reshape(n, D_MODEL), hf.reshape((n,) + ROW_TILE_SHAPE), idx.reshape(n, ROUTER_LANES),
           gates.reshape(n, ROUTER_LANES), counts)
    return tok, shift, s_t, k_heads, v_heads


def _state_to_t(s):
    b = s.shape[0]
    return s.transpose(0, 3, 1, 2).reshape(b, HEAD_DIM, D_MIX)


def _state_from_t(s_t):
    b = s_t.shape[0]
    return s_t.reshape(b, HEAD_DIM, N_HEADS, HEAD_DIM).transpose(0, 2, 3, 1)


def kernel(x_prompt, x_sample, state_rw_shift, state_rw_wkv, cache_sb_k, cache_sb_v, cache_mem_k, cache_mem_v, mem_prompt, g_mix, w_in, rw_mu, rw_w0, rw_w_up, rw_a0, rw_a_up, rw_g_up, rw_k_k, rw_k_a, rw_r_k, rw_gn_w, rw_gn_b, w_branch, w_out, g_xattn, g_mem, w_xq, w_mk, w_mv, w_xo, g_ffn, w_router, b_router, w_e1, b_e1, w_e2, b_e2, g_final):
    assert g_mix.shape[0] == 1, "single-layer trunk"
    row = lambda a: a.reshape(1, -1)
    w = dict(
        g_mix=g_mix[0], w_in=w_in[0].astype(BF16),
        rw=dict(mu=row(rw_mu[0]), w0=row(rw_w0[0]), a0=row(rw_a0[0]), k_k=row(rw_k_k[0]), k_a=row(rw_k_a[0]),
                r_k=row(rw_r_k[0]), w_up=rw_w_up[0].astype(BF16), a_up=rw_a_up[0].astype(BF16),
                g_up=rw_g_up[0].astype(BF16)),
        gn_w=row(rw_gn_w[0]), gn_b=row(rw_gn_b[0]),
        wb0=w_branch[0, 0].astype(BF16), wb1=w_branch[0, 1].astype(BF16), w_out=w_out[0].astype(BF16),
        g_xattn=g_xattn[0], w_xq=w_xq[0].astype(BF16), w_xo=w_xo[0].astype(BF16),
        g_ffn=g_ffn[0], w_router=w_router[0], b_router=b_router[0],
        w_e1=w_e1[0], b_e1=b_e1[0], w_e2=w_e2[0], b_e2=b_e2[0],
        g_final=g_final,
    )
    bp, t, _ = x_prompt.shape
    bs, ts, _ = x_sample.shape
    n_mem = mem_prompt.shape[1]

    mk_p, mv_p = mem_kv(mem_prompt.reshape(bp * n_mem, D_MODEL), g_mem[0],
                        w_mk[0].astype(BF16), w_mv[0].astype(BF16))
    mk_p = mk_p.reshape(bp, n_mem, D_MODEL)
    mv_p = mv_p.reshape(bp, n_mem, D_MODEL)
    tok_p, sh_p, st_p, k_p, v_p = _mixers(
        x_prompt, jnp.zeros((bp, 1, D_RW_COLS), F32), jnp.zeros((bp, HEAD_DIM, D_MIX), F32),
        None, None, mk_p, mv_p, w, chunk=64, prep_tile=256, tq=256)

    prev_s = plain_proj(state_rw_shift[0], w["w_in"][:, SB_COLS:SB_COLS + D_RW_COLS])
    past = cache_sb_k.shape[2]
    tok_s, sh_s, st_s, k_s, v_s = _mixers(
        x_sample, prev_s.reshape(bs, 1, D_RW_COLS), _state_to_t(state_rw_wkv[0]),
        cache_sb_k[0].reshape(bs, past, D_MIX), cache_sb_v[0].reshape(bs, past, D_MIX),
        cache_mem_k[0].reshape(bs, n_mem, D_MODEL), cache_mem_v[0].reshape(bs, n_mem, D_MODEL),
        w, chunk=ts, prep_tile=ts, tq=ts)

    y_p, y_s = moe_ffn([tok_p, tok_s], w)

    heads = lambda a, b_, t_: a.reshape(1, b_, t_, N_HEADS, HEAD_DIM)
    xh = lambda a: a.reshape(1, bp, n_mem, N_X_HEADS, X_HEAD_DIM)
    return (y_p.reshape(bp, t, D_MODEL), y_s.reshape(bs, ts, D_MODEL),
            sh_p[None], _state_from_t(st_p)[None], heads(k_p, bp, t), heads(v_p, bp, t), xh(mk_p), xh(mv_p),
            sh_s[None], _state_from_t(st_s)[None], heads(k_s, bs, ts), heads(v_s, bs, ts))
```

```python
import functools

import jax
import jax.numpy as jnp
from jax import lax
from jax.experimental import pallas as pl
from jax.experimental.pallas import tpu as pltpu

F32 = jnp.float32
BF16 = jnp.bfloat16

D_MODEL = 1024
HEAD_DIM = 64
N_HEADS = 8
D_MIX = N_HEADS * HEAD_DIM
LORA_W, LORA_A, LORA_G = 64, 64, 128
D_RW_COLS = 3 * D_MIX + LORA_W + LORA_A + LORA_G
N_X_HEADS = 4
X_HEAD_DIM = 256
N_EXPERTS = 32
TOP_K = 4
D_FF = 1024
SWIGLU_LIMIT = 7.0
SWIGLU_ALPHA = 1.702
MOE_BLOCK = 512
RMS_EPS = 1e-5
GN_EPS = 64e-5

ROW_TILE = 256
ROW_TILE_SHAPE = (8, D_MODEL // 8)
RW_GROUP = 4
RW_LANES = RW_GROUP * HEAD_DIM
RW_SEQS = 4
VMEM_LIMIT = 56 * 1024 * 1024


def _cparams(*sem):
    return pltpu.CompilerParams(dimension_semantics=sem, vmem_limit_bytes=VMEM_LIMIT)


def _cparams_gather():
    return pltpu.CompilerParams(dimension_semantics=("arbitrary",), vmem_limit_bytes=VMEM_LIMIT,
                                disable_bounds_checks=True)


def _resident(shape):
    nd = len(shape)
    return pl.BlockSpec(shape, lambda *_: (0,) * nd)


_NN = ((1,), (0,))
_NT = ((1,), (1,))
_TN = ((0,), (0,))


def _dot(a, b, dims=_NN):
    return lax.dot_general(a.astype(BF16), b.astype(BF16), (dims, ((), ())),
                           preferred_element_type=F32)


def _split(x, n):
    parts, rem = [], x
    for i in range(n):
        p = rem.astype(BF16)
        parts.append(p)
        if i + 1 < n:
            rem = rem - p.astype(F32)
    return parts


def _dot_exact_rhs(a, b_bf16, dims=_NN, n=3):
    out = None
    for p in _split(a, n):
        t = lax.dot_general(p, b_bf16, (dims, ((), ())), preferred_element_type=F32)
        out = t if out is None else out + t
    return out


def _dot_exact_lhs(a_bf16, b, dims=_NN, n=3):
    out = None
    for p in _split(b, n):
        t = lax.dot_general(a_bf16, p, (dims, ((), ())), preferred_element_type=F32)
        out = t if out is None else out + t
    return out


def _dot3(a, b, dims=_NN):
    ah, al = _split(a, 2)
    bh, bl = _split(b, 2)
    dn = (dims, ((), ()))
    return (lax.dot_general(ah, bh, dn, preferred_element_type=F32)
            + (lax.dot_general(ah, bl, dn, preferred_element_type=F32)
               + lax.dot_general(al, bh, dn, preferred_element_type=F32)))


def _rms(x, g):
    return x * lax.rsqrt(jnp.mean(x * x, axis=-1, keepdims=True) + RMS_EPS) * g


def _iota(shape, dim):
    return lax.broadcasted_iota(jnp.int32, shape, dim)


def _block_ones(n, width):
    return (_iota((n, n), 0) // width == _iota((n, n), 1) // width).astype(BF16)


def _bd_rows(x, nblk, width):
    blk = _iota(x.shape, 1) // width
    zero = jnp.zeros_like(x)
    return jnp.concatenate([jnp.where(blk == h, x, zero) for h in range(nblk)], axis=0)


def _norm_rows_kernel(x_ref, g_ref, o_ref):
    o_ref[...] = _rms(x_ref[...], g_ref[...])


def norm_rows(x, g):
    r, d = x.shape
    return pl.pallas_call(
        _norm_rows_kernel,
        out_shape=jax.ShapeDtypeStruct((r, d), F32),
        grid=(1,),
        in_specs=[_resident((r, d)), _resident((1, d))],
        out_specs=_resident((r, d)),
        name="norm_rows",
    )(x, g.reshape(1, d))


SB_COLS = 3 * D_MIX
GATE_COLS = 2 * D_MODEL
D_IN = SB_COLS + D_RW_COLS + GATE_COLS


def _in_proj_kernel(x_ref, g_ref, w_ref, q_ref, k_ref, v_ref, rw_ref, gate_ref, kh_ref, vh_ref):
    h = _rms(x_ref[...], g_ref[...]).astype(BF16)

    def proj(lo, hi):
        return jnp.dot(h, w_ref[:, lo:hi], preferred_element_type=F32)

    q_ref[...] = proj(0, D_MIX).astype(BF16)
    k = proj(D_MIX, 2 * D_MIX)
    v = proj(2 * D_MIX, SB_COLS)
    k_ref[...] = k
    v_ref[...] = v
    kh_ref[...] = k.reshape(kh_ref.shape)
    vh_ref[...] = v.reshape(vh_ref.shape)
    rw_ref[...] = proj(SB_COLS, SB_COLS + D_RW_COLS)
    gate_ref[...] = proj(SB_COLS + D_RW_COLS, D_IN).astype(BF16)


def in_proj(x, g, w_bf16):
    n = x.shape[0]
    tm = min(ROW_TILE, n)
    row = lambda width: pl.BlockSpec((tm, width), lambda i: (i, 0))
    heads = pl.BlockSpec((tm, N_HEADS, HEAD_DIM), lambda i: (i, 0, 0))
    return pl.pallas_call(
        _in_proj_kernel,
        out_shape=(jax.ShapeDtypeStruct((n, D_MIX), BF16),
                   jax.ShapeDtypeStruct((n, D_MIX), F32),
                   jax.ShapeDtypeStruct((n, D_MIX), F32),
                   jax.ShapeDtypeStruct((n, D_RW_COLS), F32),
                   jax.ShapeDtypeStruct((n, GATE_COLS), BF16),
                   jax.ShapeDtypeStruct((n, N_HEADS, HEAD_DIM), F32),
                   jax.ShapeDtypeStruct((n, N_HEADS, HEAD_DIM), F32)),
        grid=(n // tm,),
        in_specs=[row(D_MODEL), _resident((1, D_MODEL)), _resident((D_MODEL, D_IN))],
        out_specs=(row(D_MIX), row(D_MIX), row(D_MIX), row(D_RW_COLS), row(GATE_COLS), heads, heads),
        compiler_params=_cparams("parallel"),
        name="in_proj",
    )(x, g.reshape(1, D_MODEL), w_bf16)


def _plain_proj_kernel(x_ref, w_ref, o_ref):
    o_ref[...] = _dot(x_ref[...], w_ref[...])


def plain_proj(x, w_bf16):
    r, m = x.shape[0], w_bf16.shape[1]
    return pl.pallas_call(
        _plain_proj_kernel,
        out_shape=jax.ShapeDtypeStruct((r, m), F32),
        grid=(1,),
        in_specs=[_resident(x.shape), _resident(w_bf16.shape)],
        out_specs=_resident((r, m)),
        name="plain_proj",
    )(x, w_bf16)


def _rw_prep_kernel(chunk, rw_ref, prev0_ref, mu_ref, w0_ref, a0_ref, kk_ref, ka_ref, rk_ref,
                    wup_ref, aup_ref, gup_ref,
                    rt_ref, at_ref, bt_ref, kt_ref, bh_ref, kh_ref, v_ref, g_ref, bonus_ref, wc_ref,
                    carry_ref):
    tt = rw_ref.shape[1]

    @pl.when(pl.program_id(1) == 0)
    def _():
        carry_ref[...] = prev0_ref[0]

    p = rw_ref[0]
    prev = jnp.where(_iota(p.shape, 0) == 0, carry_ref[...], pltpu.roll(p, 1, 0))
    carry_ref[...] = p[tt - 1:tt]
    xs = p + mu_ref[...] * (prev - p)
    r = xs[:, 0:D_MIX]
    k = xs[:, D_MIX:2 * D_MIX]
    v = xs[:, 2 * D_MIX:3 * D_MIX]
    lo = 3 * D_MIX
    wd = xs[:, lo:lo + LORA_W]
    ad = xs[:, lo + LORA_W:lo + LORA_W + LORA_A]
    gd = xs[:, lo + LORA_W + LORA_A:D_RW_COLS]

    pre_w = w0_ref[...] + _dot(jnp.tanh(wd), wup_ref[...])
    w_log = -jax.nn.softplus(-pre_w) - 0.5
    logw = -jnp.exp(w_log)
    a = jax.nn.sigmoid(a0_ref[...] + _dot(ad, aup_ref[...]))
    g_ref[0] = _dot(jax.nn.sigmoid(gd), gup_ref[...])

    head_ones = _block_ones(D_MIX, HEAD_DIM)
    kk = k * kk_ref[...]
    ss = _dot_exact_rhs(kk * kk, head_ones, n=2)
    kk = kk * jnp.minimum(lax.rsqrt(ss), 1e12)
    k2 = k * (1.0 + (a - 1.0) * ka_ref[...])
    nb = -(kk * a)
    bonus_ref[0] = _dot_exact_rhs(r * k2 * rk_ref[...], head_ones, n=2) * v
    v_ref[0] = v.astype(BF16)

    ti, tj = _iota((tt, tt), 0), _iota((tt, tt), 1)
    same = ti // chunk == tj // chunk
    cum = _dot_exact_lhs((same & (tj <= ti)).astype(BF16), logw)
    tot = jnp.concatenate(
        [jnp.broadcast_to(cum[(c + 1) * chunk - 1:(c + 1) * chunk], (chunk, D_MIX)) for c in range(tt // chunk)],
        axis=0)
    e_in = jnp.exp(cum)
    e_out = jnp.exp(-cum)
    e_end = jnp.exp(tot - cum)
    rt_ref[0] = (r * e_in).astype(BF16)
    at_ref[0] = (kk * jnp.exp(cum - logw)).astype(BF16)
    bt_ref[0] = (nb * e_out).astype(BF16)
    kt_ref[0] = (k2 * e_out).astype(BF16)
    bh_ref[0] = (nb * e_end).astype(BF16)
    kh_ref[0] = (k2 * e_end).astype(BF16)
    etot = jnp.exp(tot)
    for c in range(tt // chunk):
        wc_ref[0, c] = etot[c * chunk:c * chunk + 1]


def rw_prep(rw, prev0, p, chunk, tt):
    b, t, _ = rw.shape
    seq = lambda width, dt: jax.ShapeDtypeStruct((b, t, width), dt)
    tile = lambda width: pl.BlockSpec((1, tt, width), lambda i, j: (i, j, 0))
    vec = lambda width: _resident((1, width))
    n_c = tt // chunk
    return pl.pallas_call(
        functools.partial(_rw_prep_kernel, chunk),
        out_shape=tuple(seq(D_MIX, BF16) for _ in range(7))
        + (seq(D_MIX, F32), seq(D_MIX, F32), jax.ShapeDtypeStruct((b, t // chunk, 1, D_MIX), F32)),
        grid=(b, t // tt),
        in_specs=[tile(D_RW_COLS), pl.BlockSpec((1, 1, D_RW_COLS), lambda i, j: (i, 0, 0)),
                  vec(D_RW_COLS), vec(D_MIX), vec(D_MIX), vec(D_MIX), vec(D_MIX), vec(D_MIX),
                  _resident((LORA_W, D_MIX)), _resident((LORA_A, D_MIX)), _resident((LORA_G, D_MIX))],
        out_specs=tuple(tile(D_MIX) for _ in range(9))
        + (pl.BlockSpec((1, n_c, 1, D_MIX), lambda i, j: (i, j, 0, 0)),),
        scratch_shapes=[pltpu.VMEM((1, D_RW_COLS), F32)],
        compiler_params=_cparams("parallel", "arbitrary"),
        name="rw_prep",
    )(rw, prev0, p["mu"], p["w0"], p["a0"], p["k_k"], p["k_a"], p["r_k"],
      p["w_up"], p["a_up"], p["g_up"])


def _unit_lower_inverse(a, chunk):
    shape = a.shape
    eye = (_iota(shape, 1) % chunk == _iota(shape, 0)).astype(F32)
    res = eye + a
    power = a.astype(BF16)
    for _ in range(chunk.bit_length() - 2):
        power = _dot(power, _bd_rows(power, RW_GROUP, chunk)).astype(BF16)
        yield
        res = res + _dot(res, _bd_rows(power, RW_GROUP, chunk))
    return res


def _interleave(chains):
    live = list(chains)
    while live:
        live = [c for c in live if next(c, StopIteration) is not StopIteration]


def _rw_chunk_kernel(chunk, rt_ref, at_ref, bt_ref, kt_ref, bh_ref, kh_ref, v_ref, g_ref, bonus_ref,
                     wc_ref, s0_ref, gnw_ref, gnb_ref, o_ref, s_out_ref, st_ref):
    c = pl.program_id(1)

    @pl.when(c == 0)
    def _():
        st_ref[...] = s0_ref[...]

    cat = (chunk, RW_GROUP * chunk)
    col_t = _iota(cat, 1) % chunk
    strict = col_t < _iota(cat, 0)
    incl = col_t <= _iota(cat, 0)
    head_ones = _block_ones(RW_LANES, HEAD_DIM)
    bd = lambda x: _bd_rows(x, RW_GROUP, HEAD_DIM)
    n_cat = RW_GROUP * chunk
    def chain(bi, grp):
        sl = slice(grp * RW_LANES, (grp + 1) * RW_LANES)
        rt, at, bt, kt = rt_ref[bi, :, sl], at_ref[bi, :, sl], bt_ref[bi, :, sl], kt_ref[bi, :, sl]
        bh, kh, v = bh_ref[bi, :, sl], kh_ref[bi, :, sl], v_ref[bi, :, sl]
        st = st_ref[bi, :, sl]

        gram = _dot(jnp.concatenate([at, rt], axis=0),
                    jnp.concatenate([bd(bt), bd(kt)], axis=0), _NT)
        yield
        zero = jnp.zeros(cat, F32)
        a_ab = jnp.where(strict, gram[:chunk, :n_cat], zero)
        a_ak = jnp.where(strict, gram[:chunk, n_cat:], zero)
        a_r = jnp.concatenate([jnp.where(incl, gram[chunk:, :n_cat], zero),
                               jnp.where(incl, gram[chunk:, n_cat:], zero)], axis=1).astype(BF16)
        v_bd = bd(v)
        x = _dot(a_ak, v_bd).astype(BF16)
        t_inv = yield from _unit_lower_inverse(a_ab, chunk)
        yield
        ua = _dot(t_inv, jnp.concatenate([bd(x), bd(at)], axis=1))
        st_bd = bd(st.astype(BF16))
        yield
        from_state = _dot(jnp.concatenate([ua[:, RW_LANES:].astype(BF16), rt], axis=0), st_bd)
        yield
        u = (ua[:, :RW_LANES] + from_state[:chunk]).astype(BF16)
        y = from_state[chunk:] + _dot(a_r, jnp.concatenate([bd(u), v_bd], axis=0))
        m = _dot(jnp.concatenate([bh, kh], axis=0), jnp.concatenate([u, v], axis=0), _TN)
        yield
        lane_head = _iota((HEAD_DIM, RW_LANES), 1) // HEAD_DIM
        fold = jnp.zeros((HEAD_DIM, RW_LANES), F32)
        for h in range(RW_GROUP):
            fold = fold + jnp.where(lane_head == h, m[h * HEAD_DIM:(h + 1) * HEAD_DIM], 0.0)
        diag = _iota((HEAD_DIM, RW_LANES), 1) % HEAD_DIM == _iota((HEAD_DIM, RW_LANES), 0)
        w_rows = jnp.where(diag, wc_ref[bi, 0, :, sl], 0.0)
        w_t = _dot_exact_rhs(w_rows, head_ones)
        st_ref[bi, :, sl] = st * w_t + fold

        mean = _dot_exact_rhs(y, head_ones, n=2) * (1.0 / HEAD_DIM)
        yield
        d = y - mean
        var = _dot_exact_rhs(d * d, head_ones, n=2) * (1.0 / HEAD_DIM)
        yield
        yn = d * lax.rsqrt(var + GN_EPS) * gnw_ref[:, sl] + gnb_ref[:, sl]
        o_ref[bi, :, sl] = ((yn + bonus_ref[bi, :, sl]) * g_ref[bi, :, sl]).astype(BF16)

    _interleave(chain(bi, grp) for bi in range(st_ref.shape[0]) for grp in range(N_HEADS // RW_GROUP))

    @pl.when(c == pl.num_programs(1) - 1)
    def _():
        s_out_ref[...] = st_ref[...]


def rw_chunks(prep, s0_t, gn_w, gn_b, chunk):
    rt, at, bt, kt, bh, kh, v, g, bonus, wc = prep
    b, t, _ = rt.shape
    nb = RW_SEQS if b % RW_SEQS == 0 else 1
    tile = pl.BlockSpec((nb, chunk, D_MIX), lambda i, j: (i, j, 0))
    state = pl.BlockSpec((nb, HEAD_DIM, D_MIX), lambda i, j: (i, 0, 0))
    return pl.pallas_call(
        functools.partial(_rw_chunk_kernel, chunk),
        out_shape=(jax.ShapeDtypeStruct((b, t, D_MIX), BF16),
                   jax.ShapeDtypeStruct((b, HEAD_DIM, D_MIX), F32)),
        grid=(b // nb, t // chunk),
        in_specs=[tile] * 9 + [pl.BlockSpec((nb, 1, 1, D_MIX), lambda i, j: (i, j, 0, 0)), state,
                               _resident((1, D_MIX)), _resident((1, D_MIX))],
        out_specs=(tile, state),
        scratch_shapes=[pltpu.VMEM((nb, HEAD_DIM, D_MIX), F32)],
        compiler_params=_cparams("parallel", "arbitrary"),
        name="rw_chunks",
    )(rt, at, bt, kt, bh, kh, v, g, bonus, wc, s0_t, gn_w, gn_b)


SB_PAIR = 2 * HEAD_DIM
SB_PAST_BLOCK = 256
SB_VISITS = 4


def _sb_kernel(tq, n_past, n_q, q_ref, kn_ref, vn_ref, *rest):
    if n_past:
        kp_ref, vp_ref, o_ref, acc_ref, c_ref = rest
    else:
        o_ref, acc_ref, c_ref = rest
    qi = pl.program_id(2)
    q2 = q_ref[0] * (HEAD_DIM ** -0.5)
    lane = _iota(q2.shape, 1)
    zero_q = jnp.zeros_like(q2)
    q_st = jnp.concatenate([jnp.where(lane < HEAD_DIM, q2, zero_q), jnp.where(lane >= HEAD_DIM, q2, zero_q)],
                           axis=0)
    acc_ref[...] = jnp.zeros_like(acc_ref)
    c_ref[...] = jnp.zeros_like(c_ref)

    def visit(k_blk, v_blk, masked):
        tk = k_blk.shape[0]
        later = (_iota((tk, tk), 0) > _iota((tk, tk), 1)).astype(BF16)
        z = _dot(q_st, k_blk, _NT)
        sp = jnp.maximum(z, 0.0) + jnp.log(1.0 + jnp.exp(-jnp.abs(z)))
        if masked:
            before = _iota((2 * tq, tk), 1) < _iota((2 * tq, tk), 0) % tq
            drop = jnp.where(before, sp, 0.0)
        else:
            drop = sp
        srev = _dot(drop, later)
        yield
        c = c_ref[...]
        att = jnp.exp((z - sp) - (c + srev))
        if masked:
            att = jnp.where(before, att, 0.0)
        acc_ref[...] += _dot(att, v_blk)
        c_ref[...] = c + (srev[:, 0:1] + drop[:, 0:1])

    def new_block(blk, masked=False):
        s = pl.multiple_of(blk * tq, tq)
        return visit(kn_ref[0, pl.ds(s, tq), :], vn_ref[0, pl.ds(s, tq), :], masked)

    lead = SB_VISITS - 1
    for n_with in range(min(lead, n_q - 1) + 1):
        @pl.when((qi == n_with) if n_with < lead else (qi >= lead))
        def _():
            _interleave([new_block(qi, masked=True)] + [new_block(qi - 1 - u) for u in range(n_with)])

    left = jnp.maximum(qi - lead, 0)

    def earlier_group(i, carry):
        _interleave(new_block(left - 1 - u - SB_VISITS * i) for u in range(SB_VISITS))
        return carry

    lax.fori_loop(0, left // SB_VISITS, earlier_group, 0)
    rest = left % SB_VISITS

    @pl.when(rest >= 2)
    def _():
        _interleave([new_block(rest - 1), new_block(rest - 2)])

    @pl.when(rest % 2 == 1)
    def _():
        _interleave([new_block(0)])

    if n_past:
        def past_block(blk):
            s = pl.multiple_of(blk * SB_PAST_BLOCK, SB_PAST_BLOCK)
            return visit(kp_ref[0, pl.ds(s, SB_PAST_BLOCK), :], vp_ref[0, pl.ds(s, SB_PAST_BLOCK), :], False)

        def past_group(i, carry):
            _interleave(past_block(n_past - 1 - u - SB_VISITS * i) for u in range(SB_VISITS))
            return carry

        lax.fori_loop(0, n_past // SB_VISITS, past_group, 0)
        _interleave(past_block(blk) for blk in reversed(range(n_past % SB_VISITS)))

    o_ref[0] = jnp.where(lane < HEAD_DIM, acc_ref[:tq], acc_ref[tq:]).astype(BF16)


def stick_breaking(q, k_new, v_new, k_past, v_past, tq):
    b, t, _ = q.shape
    n_pairs = D_MIX // SB_PAIR
    qspec = pl.BlockSpec((1, tq, SB_PAIR), lambda i, p, j: (i, j, p))
    seq = lambda length: pl.BlockSpec((1, length, SB_PAIR), lambda i, p, j: (i, 0, p))
    args, specs, n_past = [q, k_new, v_new], [qspec, seq(t), seq(t)], 0
    if k_past is not None:
        past_len = k_past.shape[1]
        n_past = past_len // SB_PAST_BLOCK
        args += [k_past, v_past]
        specs += [seq(past_len), seq(past_len)]
    return pl.pallas_call(
        functools.partial(_sb_kernel, tq, n_past, t // tq),
        out_shape=jax.ShapeDtypeStruct((b, t, D_MIX), BF16),
        grid=(b, n_pairs, t // tq),
        in_specs=specs,
        out_specs=qspec,
        scratch_shapes=[pltpu.VMEM((2 * tq, SB_PAIR), F32), pltpu.VMEM((2 * tq, 1), F32)],
        compiler_params=_cparams("parallel", "parallel", "arbitrary"),
        name="stick_breaking",
    )(*args)


def _merge_kernel(orw_ref, osb_ref, gate_ref, x_ref, wb0_ref, wb1_ref, wout_ref, gx_ref, wxq_ref,
                  x1_ref, qx_ref):
    g = jax.nn.sigmoid(gate_ref[...].astype(F32))
    mixed = (g[:, :D_MODEL] * jnp.dot(orw_ref[...], wb0_ref[...], preferred_element_type=F32)
             + g[:, D_MODEL:] * jnp.dot(osb_ref[...], wb1_ref[...], preferred_element_type=F32))
    x1 = x_ref[...] + _dot(mixed, wout_ref[...])
    x1_ref[...] = x1
    qx_ref[...] = _dot(_rms(x1, gx_ref[...]), wxq_ref[...]).astype(BF16)


def merge_out(o_rw, o_sb, gate, x, wb0, wb1, wout, gx, wxq):
    n = x.shape[0]
    tm = min(ROW_TILE, n)
    row = lambda width: pl.BlockSpec((tm, width), lambda i: (i, 0))
    return pl.pallas_call(
        _merge_kernel,
        out_shape=(jax.ShapeDtypeStruct((n, D_MODEL), F32), jax.ShapeDtypeStruct((n, D_MODEL), BF16)),
        grid=(n // tm,),
        in_specs=[row(D_MIX), row(D_MIX), row(GATE_COLS), row(D_MODEL),
                  _resident((D_MIX, D_MODEL)), _resident((D_MIX, D_MODEL)), _resident((D_MODEL, D_MODEL)),
                  _resident((1, D_MODEL)), _resident((D_MODEL, D_MODEL))],
        out_specs=(row(D_MODEL), row(D_MODEL)),
        compiler_params=_cparams("parallel"),
        name="merge_out",
    )(o_rw, o_sb, gate, x, wb0, wb1, wout, gx.reshape(1, D_MODEL), wxq)


def _mem_kv_kernel(m_ref, g_ref, wk_ref, wv_ref, k_ref, v_ref):
    h = _rms(m_ref[...], g_ref[...]).astype(BF16)
    k_ref[...] = jnp.dot(h, wk_ref[...], preferred_element_type=F32)
    v_ref[...] = jnp.dot(h, wv_ref[...], preferred_element_type=F32)


def mem_kv(mem, g, wk, wv):
    n = mem.shape[0]
    tm = min(ROW_TILE, n)
    row = pl.BlockSpec((tm, D_MODEL), lambda i: (i, 0))
    return pl.pallas_call(
        _mem_kv_kernel,
        out_shape=(jax.ShapeDtypeStruct((n, D_MODEL), F32),) * 2,
        grid=(n // tm,),
        in_specs=[row, _resident((1, D_MODEL)), _resident((D_MODEL, D_MODEL)), _resident((D_MODEL, D_MODEL))],
        out_specs=(row, row),
        compiler_params=_cparams("parallel"),
        name="mem_kv",
    )(mem, g.reshape(1, D_MODEL), wk, wv)


ROUTER_LANES = 128


def _xattn_kernel(q_ref, x1_ref, mk_ref, mv_ref, wxo_ref, gf_ref, wr_ref, br_ref,
                  x2_ref, hf_ref, idx_ref, gates_ref, counts_ref):
    q = q_ref[0]
    heads = [None] * N_X_HEADS

    def head(h):
        sl = slice(h * X_HEAD_DIM, (h + 1) * X_HEAD_DIM)
        s = _dot(q[:, sl], mk_ref[0, :, sl], _NT) * (X_HEAD_DIM ** -0.5)
        yield
        e = jnp.exp(s - jnp.max(s, axis=-1, keepdims=True))
        p = e * (1.0 / jnp.sum(e, axis=-1, keepdims=True))
        heads[h] = _dot(p, mv_ref[0, :, sl])

    _interleave(head(h) for h in range(N_X_HEADS))
    x2 = x1_ref[0] + _dot(jnp.concatenate(heads, axis=1), wxo_ref[...])
    x2_ref[0] = x2
    hf = _rms(x2, gf_ref[...])
    hf_ref[0] = hf.reshape(hf_ref.shape[1:])
    logits = _dot3(hf, wr_ref[...]) + br_ref[...]
    lane = _iota(logits.shape, 1)
    vals, idx_out = [], jnp.zeros(logits.shape, jnp.int32)
    chosen = jnp.zeros(logits.shape, F32)
    for j in range(TOP_K):
        m = jnp.max(logits, axis=-1, keepdims=True)
        pick = jnp.min(jnp.where(logits == m, lane, ROUTER_LANES), axis=-1, keepdims=True)
        vals.append(m)
        idx_out = jnp.where(lane == j, pick, idx_out)
        chosen = chosen + (lane == pick).astype(F32)
        logits = jnp.where(lane == pick, -jnp.inf, logits)

    @pl.when((pl.program_id(0) == 0) & (pl.program_id(1) == 0))
    def _():
        counts_ref[...] = jnp.zeros_like(counts_ref)

    counts_ref[...] += jnp.broadcast_to(jnp.sum(chosen, axis=0, keepdims=True), counts_ref.shape)
    exps = [jnp.exp(v - vals[0]) for v in vals]
    denom = exps[0] + exps[1] + exps[2] + exps[3]
    gates = jnp.zeros(logits.shape, F32)
    for j in range(TOP_K):
        gates = jnp.where(lane == j, exps[j] / denom, gates)
    idx_ref[0] = idx_out
    gates_ref[0] = gates


def xattn_router(qx, x1, mk, mv, wxo, g_ffn, w_router, b_router):
    b, t, _ = x1.shape
    tm = min(ROW_TILE, t)
    n_mem = mk.shape[1]
    wr = jnp.zeros((D_MODEL, ROUTER_LANES), F32).at[:, :N_EXPERTS].set(w_router)
    br = jnp.full((1, ROUTER_LANES), -jnp.inf, F32).at[0, :N_EXPERTS].set(b_router)
    tile = pl.BlockSpec((1, tm, D_MODEL), lambda i, j: (i, j, 0))
    mem = pl.BlockSpec((1, n_mem, D_MODEL), lambda i, j: (i, 0, 0))
    small = pl.BlockSpec((1, tm, ROUTER_LANES), lambda i, j: (i, j, 0))
    return pl.pallas_call(
        _xattn_kernel,
        out_shape=(jax.ShapeDtypeStruct((b, t, D_MODEL), F32), jax.ShapeDtypeStruct((b, t) + ROW_TILE_SHAPE, F32),
                   jax.ShapeDtypeStruct((b, t, ROUTER_LANES), jnp.int32),
                   jax.ShapeDtypeStruct((b, t, ROUTER_LANES), F32),
                   jax.ShapeDtypeStruct((ROUTE_SUBLANES, ROUTER_LANES), F32)),
        grid=(b, t // tm),
        in_specs=[tile, tile, mem, mem, _resident((D_MODEL, D_MODEL)), _resident((1, D_MODEL)),
                  _resident((D_MODEL, ROUTER_LANES)), _resident((1, ROUTER_LANES))],
        out_specs=(tile, pl.BlockSpec((1, tm) + ROW_TILE_SHAPE, lambda i, j: (i, j, 0, 0)), small, small,
                   _resident((ROUTE_SUBLANES, ROUTER_LANES))),
        compiler_params=_cparams("arbitrary", "arbitrary"),
        name="xattn_router",
    )(qx, x1, mk, mv, wxo, g_ffn.reshape(1, D_MODEL), wr, br)


ROW_GROUP = 8
DMA_QUEUES = 2
ROUTE_SUBLANES = 8


def _route_kernel(idx_ref, total_ref, dest_ref, carry_ref, start_ref):
    tm = idx_ref.shape[0]
    idx = idx_ref[...]
    lane = _iota((tm, ROUTER_LANES), 1)
    picks = [idx[:, j:j + 1] for j in range(TOP_K)]
    onehot = jnp.zeros((tm, ROUTER_LANES), F32)
    for pick in picks:
        onehot = onehot + (lane == pick).astype(F32)

    @pl.when(pl.program_id(0) == 0)
    def _():
        carry_ref[...] = jnp.zeros_like(carry_ref)
        padded = jnp.floor((total_ref[...] + (MOE_BLOCK - 1.0)) * (1.0 / MOE_BLOCK)) * MOE_BLOCK
        earlier = (_iota((ROUTER_LANES, ROUTER_LANES), 0) < _iota((ROUTER_LANES, ROUTER_LANES), 1))
        start_ref[...] = _dot_exact_rhs(padded, earlier.astype(BF16))

    below = (_iota((tm, tm), 1) < _iota((tm, tm), 0)).astype(BF16)
    row = _dot(below, onehot) + (carry_ref[0:1, :] + start_ref[0:1, :])
    dest = jnp.zeros((tm, ROUTER_LANES), F32)
    for j, pick in enumerate(picks):
        mine = jnp.sum(jnp.where(lane == pick, row, 0.0), axis=1, keepdims=True)
        dest = jnp.where(lane == j, mine, dest)
    dest_ref[0] = dest.T[:ROUTE_SUBLANES].astype(jnp.int32)
    carry_ref[...] += jnp.broadcast_to(jnp.sum(onehot, axis=0, keepdims=True), carry_ref.shape)


def moe_route(idx, total, tm):
    n = idx.shape[0]
    n_tiles = n // tm
    return pl.pallas_call(
        _route_kernel,
        out_shape=jax.ShapeDtypeStruct((n_tiles, ROUTE_SUBLANES, tm), jnp.int32),
        grid=(n_tiles,),
        in_specs=[pl.BlockSpec((tm, ROUTER_LANES), lambda t: (t, 0)), _resident((ROUTE_SUBLANES, ROUTER_LANES))],
        out_specs=pl.BlockSpec((1, ROUTE_SUBLANES, tm), lambda t: (t, 0, 0)),
        scratch_shapes=[pltpu.VMEM((ROUTE_SUBLANES, ROUTER_LANES), F32),
                        pltpu.VMEM((ROUTE_SUBLANES, ROUTER_LANES), F32)],
        compiler_params=_cparams("arbitrary"),
        name="moe_route",
    )(idx, total)


def _block_tables(counts, n_blocks):
    cnt = counts[0, :N_EXPERTS].astype(jnp.int32)
    padded = (cnt + MOE_BLOCK - 1) // MOE_BLOCK * MOE_BLOCK
    pends = jnp.cumsum(padded).astype(jnp.int32)
    block_start = jnp.arange(n_blocks, dtype=jnp.int32) * MOE_BLOCK
    block_expert = jnp.minimum(jnp.sum((pends[None, :] <= block_start[:, None]).astype(jnp.int32), axis=1),
                               N_EXPERTS - 1)
    n_used = (pends[-1:] // MOE_BLOCK).astype(jnp.int32)
    return pends, block_expert, n_used


def _dispatch_kernel(bounds, pends_ref, dest_ref, *rest):
    hf_refs = rest[:len(bounds) - 1]
    xs_hbm, zbuf, sem = rest[len(bounds) - 1:]
    i = pl.program_id(0)

    @pl.when(i == 0)
    def _():
        zbuf[...] = jnp.zeros_like(zbuf)
        n_blocks = xs_hbm.shape[0] // MOE_BLOCK
        n_used = pends_ref[N_EXPERTS - 1] // MOE_BLOCK
        for wait in (False, True):
            def unused_block(blk, carry):
                cp = pltpu.make_async_copy(
                    zbuf, xs_hbm.at[pl.ds(pl.multiple_of(blk * MOE_BLOCK, MOE_BLOCK), MOE_BLOCK)], sem)
                cp.wait() if wait else cp.start()
                return carry

            lax.fori_loop(n_used, n_blocks, unused_block, 0)
            for e in range(N_EXPERTS):
                end = pends_ref[e]
                begin = pends_ref[e - 1] if e else 0

                @pl.when(end > begin)
                def _():
                    last = pl.multiple_of(end - MOE_BLOCK, MOE_BLOCK)
                    cp = pltpu.make_async_copy(zbuf, xs_hbm.at[pl.ds(last, MOE_BLOCK)], sem)
                    cp.wait() if wait else cp.start()

    for hf_ref, lo, hi in zip(hf_refs, bounds[:-1], bounds[1:]):
        @pl.when((i >= lo) & (i < hi))
        def _():
            tm = hf_ref.shape[0]
            for j in range(TOP_K):
                def issue(g, carry):
                    base = pl.multiple_of(g * ROW_GROUP, ROW_GROUP)
                    rows = hf_ref.at[pl.ds(base, ROW_GROUP)]
                    for u in range(ROW_GROUP):
                        pltpu.make_async_copy(rows.at[pl.ds(u, 1)], xs_hbm.at[pl.ds(dest_ref[j * tm + base + u], 1)],
                                              sem).start(priority=u % DMA_QUEUES)
                    return carry

                lax.fori_loop(0, tm // ROW_GROUP, issue, 0)
            for j in range(TOP_K):
                pltpu.make_async_copy(hf_ref, xs_hbm.at[pl.ds(0, tm)], sem).wait()


def moe_dispatch(hfs, bounds, dest, pends, n_rows):
    n_tiles, _, tm = dest.shape
    hf_spec = lambda lo, hi: pl.BlockSpec((tm,) + ROW_TILE_SHAPE,
                                          lambda i, pe: (jnp.clip(i - lo, 0, hi - lo - 1), 0, 0))
    grid_spec = pltpu.PrefetchScalarGridSpec(
        num_scalar_prefetch=1,
        grid=(n_tiles,),
        in_specs=[pl.BlockSpec((ROUTE_SUBLANES * tm,), lambda i, pe: (i,), memory_space=pltpu.SMEM)]
        + [hf_spec(lo, hi) for lo, hi in zip(bounds[:-1], bounds[1:])],
        out_specs=pl.BlockSpec(memory_space=pl.ANY),
        scratch_shapes=[pltpu.VMEM((MOE_BLOCK,) + ROW_TILE_SHAPE, F32), pltpu.SemaphoreType.DMA(())],
    )
    return pl.pallas_call(
        functools.partial(_dispatch_kernel, tuple(bounds)),
        out_shape=jax.ShapeDtypeStruct((n_rows,) + ROW_TILE_SHAPE, F32),
        grid_spec=grid_spec,
        compiler_params=_cparams_gather(),
        name="moe_dispatch",
    )(pends, dest.reshape(-1), *hfs)


W1_CHUNKS = 4
W2_CHUNKS = 2


def _moe_kernel(be_ref, nused_ref, x_ref, *rest):
    w1 = rest[:W1_CHUNKS]
    b1_ref = rest[W1_CHUNKS]
    w2 = rest[W1_CHUNKS + 1:W1_CHUNKS + 1 + W2_CHUNKS]
    b2_ref, o_ref, w1_bf, w2_bf = rest[W1_CHUNKS + 1 + W2_CHUNKS:]
    i = pl.program_id(0)
    half = W1_CHUNKS // 2
    wide = 2 * D_FF // W1_CHUNKS

    @pl.when((i < nused_ref[0]) & ((i == 0) | (be_ref[i] != be_ref[jnp.maximum(i - 1, 0)])))
    def _():
        for c in range(W1_CHUNKS):
            w1_bf[c] = w1[c][0].astype(BF16)
        for c in range(W2_CHUNKS):
            w2_bf[c] = w2[c][0].astype(BF16)

    @pl.when(i < nused_ref[0])
    def _():
        x = x_ref[...].reshape(MOE_BLOCK, D_MODEL).astype(BF16)
        parts = [None] * half

        def ff_slice(c):
            glu = jnp.dot(x, w1_bf[c], preferred_element_type=F32) + b1_ref[0, :, c * wide:(c + 1) * wide]
            lin = (jnp.dot(x, w1_bf[half + c], preferred_element_type=F32)
                   + b1_ref[0, :, D_FF + c * wide:D_FF + (c + 1) * wide])
            yield
            glu = jnp.minimum(glu, SWIGLU_LIMIT)
            lin = jnp.clip(lin, -SWIGLU_LIMIT, SWIGLU_LIMIT)
            act = glu * jax.nn.sigmoid(SWIGLU_ALPHA * glu) * (lin + 1.0)
            parts[c] = _dot(act, w2_bf[c])

        _interleave(ff_slice(c) for c in range(half))
        o_ref[...] = (b2_ref[0] + sum(parts[1:], parts[0])).reshape(o_ref.shape)

    @pl.when(i >= nused_ref[0])
    def _():
        o_ref[...] = jnp.zeros_like(o_ref)


def moe_experts(xs, block_expert, n_used, w1, b1, w2, b2):
    n_blocks = block_expert.shape[0]
    assert W1_CHUNKS // 2 == W2_CHUNKS
    c1 = 2 * D_FF // W1_CHUNKS
    c2 = D_FF // W2_CHUNKS
    w1_spec = lambda c: pl.BlockSpec((1, D_MODEL, c1), lambda i, be, nu: (be[i], 0, c))
    w2_spec = lambda c: pl.BlockSpec((1, c2, D_MODEL), lambda i, be, nu: (be[i], c, 0))
    grid_spec = pltpu.PrefetchScalarGridSpec(
        num_scalar_prefetch=2,
        grid=(n_blocks,),
        in_specs=[pl.BlockSpec((MOE_BLOCK,) + ROW_TILE_SHAPE, lambda i, be, nu: (jnp.minimum(i, nu[0] - 1), 0, 0))]
        + [w1_spec(c) for c in range(W1_CHUNKS)]
        + [pl.BlockSpec((1, 1, 2 * D_FF), lambda i, be, nu: (be[i], 0, 0))]
        + [w2_spec(c) for c in range(W2_CHUNKS)]
        + [pl.BlockSpec((1, 1, D_MODEL), lambda i, be, nu: (be[i], 0, 0))],
        out_specs=pl.BlockSpec((MOE_BLOCK,) + ROW_TILE_SHAPE, lambda i, be, nu: (i, 0, 0)),
        scratch_shapes=[pltpu.VMEM((W1_CHUNKS, D_MODEL, c1), BF16), pltpu.VMEM((W2_CHUNKS, c2, D_MODEL), BF16)],
    )
    return pl.pallas_call(
        _moe_kernel,
        out_shape=jax.ShapeDtypeStruct((n_blocks * MOE_BLOCK,) + ROW_TILE_SHAPE, F32),
        grid_spec=grid_spec,
        compiler_params=_cparams("arbitrary"),
        name="moe_experts",
    )(block_expert, n_used, xs, *([w1] * W1_CHUNKS), b1.reshape(N_EXPERTS, 1, 2 * D_FF),
      *([w2] * W2_CHUNKS), b2.reshape(N_EXPERTS, 1, D_MODEL))


def _start_combine_gather(dest_ref, os_hbm, dst, sem):
    tm = dst.shape[0] // TOP_K
    for j in range(TOP_K):
        def issue(g, carry):
            base = pl.multiple_of(g * ROW_GROUP, ROW_GROUP)
            rows = dst.at[pl.ds(j * tm + base, ROW_GROUP)]
            for u in range(ROW_GROUP):
                pltpu.make_async_copy(os_hbm.at[pl.ds(dest_ref[j * tm + base + u], 1)], rows.at[pl.ds(u, 1)],
                                      sem).start(priority=u % DMA_QUEUES)
            return carry

        lax.fori_loop(0, tm // ROW_GROUP, issue, 0)


def _combine_kernel(n_tiles, dest_ref, dest_next_ref, os_hbm, x2_ref, gates_ref, gfin_ref, y_ref, buf, sem):
    tm = x2_ref.shape[0]
    i = pl.program_id(0)
    slot = i % 2

    @pl.when(i == 0)
    def _():
        _start_combine_gather(dest_ref, os_hbm, buf.at[0], sem.at[0])

    if n_tiles > 1:
        @pl.when(i + 1 < n_tiles)
        def _():
            _start_combine_gather(dest_next_ref, os_hbm, buf.at[1 - slot], sem.at[1 - slot])

    pltpu.make_async_copy(os_hbm.at[pl.ds(0, TOP_K * tm)], buf.at[slot], sem.at[slot]).wait()
    y = x2_ref[...]
    gates = gates_ref[...]
    for j in range(TOP_K):
        y = y + gates[:, j:j + 1] * buf[slot, j * tm:(j + 1) * tm].reshape(tm, D_MODEL)
    y_ref[...] = _rms(y, gfin_ref[...])


def moe_combine(os_rows, dest, x2, gates, g_final):
    n = x2.shape[0]
    n_tiles, _, tm = dest.shape
    dest_spec = lambda step: pl.BlockSpec(
        (ROUTE_SUBLANES * tm,), lambda i: (jnp.minimum(i + step, n_tiles - 1),), memory_space=pltpu.SMEM)
    dest = dest.reshape(-1)
    return pl.pallas_call(
        functools.partial(_combine_kernel, n_tiles),
        out_shape=jax.ShapeDtypeStruct((n, D_MODEL), F32),
        grid=(n_tiles,),
        in_specs=[dest_spec(0), dest_spec(1),
                  pl.BlockSpec(memory_space=pl.ANY),
                  pl.BlockSpec((tm, D_MODEL), lambda i: (i, 0)),
                  pl.BlockSpec((tm, ROUTER_LANES), lambda i: (i, 0)),
                  _resident((1, D_MODEL))],
        out_specs=pl.BlockSpec((tm, D_MODEL), lambda i: (i, 0)),
        scratch_shapes=[pltpu.VMEM((2, TOP_K * tm) + ROW_TILE_SHAPE, F32), pltpu.SemaphoreType.DMA((2,))],
        compiler_params=_cparams_gather(),
        name="moe_combine",
    )(dest, dest, os_rows, x2, gates, g_final.reshape(1, D_MODEL))


def moe_ffn(groups, w):
    sizes = [g[0].shape[0] for g in groups]
    n = sum(sizes)
    tm = next(c for c in (256, 128, 64, 32, 16, 8) if all(s % c == 0 for s in sizes))
    counts = sum(g[4] for g in groups[1:]) + groups[0][4]
    dest = moe_route(jnp.concatenate([g[2] for g in groups], axis=0), counts, tm)
    n_blocks = -(-(n * TOP_K + N_EXPERTS * (MOE_BLOCK - 1)) // MOE_BLOCK)
    pends, block_expert, n_used = _block_tables(counts, n_blocks)
    bounds = [0]
    for s in sizes:
        bounds.append(bounds[-1] + s // tm)
    xs = moe_dispatch([g[1] for g in groups], bounds, dest, pends, n_blocks * MOE_BLOCK)
    os_rows = moe_experts(xs, block_expert, n_used, w["w_e1"], w["b_e1"], w["w_e2"], w["b_e2"])
    return [moe_combine(os_rows, dest[lo:hi], g[0], g[3], w["g_final"])
            for g, lo, hi in zip(groups, bounds[:-1], bounds[1:])]


def _mixers(x, prev0, s0_t, k_past, v_past, mk, mv, w, chunk, prep_tile, tq):
    b, t, _ = x.shape
    n = b * t
    xf = x.reshape(n, D_MODEL)
    q, k_sb, v_sb, rw, gate, k_heads, v_heads = in_proj(xf, w["g_mix"], w["w_in"])
    shift = norm_rows(x[:, -1], w["g_mix"])
    prep = rw_prep(rw.reshape(b, t, D_RW_COLS), prev0, w["rw"], chunk, prep_tile)
    o_rw, s_t = rw_chunks(prep, s0_t, w["gn_w"], w["gn_b"], chunk)
    seq = lambda a: a.reshape(b, t, D_MIX)
    o_sb = stick_breaking(seq(q), seq(k_sb), seq(v_sb), k_past, v_past, tq)
    x1, qx = merge_out(o_rw.reshape(n, D_MIX), o_sb.reshape(n, D_MIX), gate, xf,
                       w["wb0"], w["wb1"], w["w_out"], w["g_xattn"], w["w_xq"])
    x2, hf, idx, gates, counts = xattn_router(qx.reshape(b, t, D_MODEL), x1.reshape(b, t, D_MODEL), mk, mv,
                                              w["w_xo"], w["g_ffn"], w["w_router"], w["b_router"])
    tok = (x2.reshape(n, D_MODEL), hf.reshape((n,) + ROW_TILE_SHAPE), idx.reshape(n, ROUTER_LANES),
           gates.reshape(n, ROUTER_LANES), counts)
    return tok, shift, s_t, k_heads, v_heads


def _state_to_t(s):
    b = s.shape[0]
    return s.transpose(0, 3, 1, 2).reshape(b, HEAD_DIM, D_MIX)


def _state_from_t(s_t):
    b = s_t.shape[0]
    return s_t.reshape(b, HEAD_DIM, N_HEADS, HEAD_DIM).transpose(0, 2, 3, 1)


def kernel(x_prompt, x_sample, state_rw_shift, state_rw_wkv, cache_sb_k, cache_sb_v, cache_mem_k, cache_mem_v, mem_prompt, g_mix, w_in, rw_mu, rw_w0, rw_w_up, rw_a0, rw_a_up, rw_g_up, rw_k_k, rw_k_a, rw_r_k, rw_gn_w, rw_gn_b, w_branch, w_out, g_xattn, g_mem, w_xq, w_mk, w_mv, w_xo, g_ffn, w_router, b_router, w_e1, b_e1, w_e2, b_e2, g_final):
    assert g_mix.shape[0] == 1, "single-layer trunk"
    row = lambda a: a.reshape(1, -1)
    w = dict(
        g_mix=g_mix[0], w_in=w_in[0].astype(BF16),
        rw=dict(mu=row(rw_mu[0]), w0=row(rw_w0[0]), a0=row(rw_a0[0]), k_k=row(rw_k_k[0]), k_a=row(rw_k_a[0]),
                r_k=row(rw_r_k[0]), w_up=rw_w_up[0].astype(BF16), a_up=rw_a_up[0].astype(BF16),
                g_up=rw_g_up[0].astype(BF16)),
        gn_w=row(rw_gn_w[0]), gn_b=row(rw_gn_b[0]),
        wb0=w_branch[0, 0].astype(BF16), wb1=w_branch[0, 1].astype(BF16), w_out=w_out[0].astype(BF16),
        g_xattn=g_xattn[0], w_xq=w_xq[0].astype(BF16), w_xo=w_xo[0].astype(BF16),
        g_ffn=g_ffn[0], w_router=w_router[0], b_router=b_router[0],
        w_e1=w_e1[0], b_e1=b_e1[0], w_e2=w_e2[0], b_e2=b_e2[0],
        g_final=g_final,
    )
    bp, t, _ = x_prompt.shape
    bs, ts, _ = x_sample.shape
    n_mem = mem_prompt.shape[1]

    mk_p, mv_p = mem_kv(mem_prompt.reshape(bp * n_mem, D_MODEL), g_mem[0],
                        w_mk[0].astype(BF16), w_mv[0].astype(BF16))
    mk_p = mk_p.reshape(bp, n_mem, D_MODEL)
    mv_p = mv_p.reshape(bp, n_mem, D_MODEL)
    tok_p, sh_p, st_p, k_p, v_p = _mixers(
        x_prompt, jnp.zeros((bp, 1, D_RW_COLS), F32), jnp.zeros((bp, HEAD_DIM, D_MIX), F32),
        None, None, mk_p, mv_p, w, chunk=64, prep_tile=256, tq=256)

    prev_s = plain_proj(state_rw_shift[0], w["w_in"][:, SB_COLS:SB_COLS + D_RW_COLS])
    past = cache_sb_k.shape[2]
    tok_s, sh_s, st_s, k_s, v_s = _mixers(
        x_sample, prev_s.reshape(bs, 1, D_RW_COLS), _state_to_t(state_rw_wkv[0]),
        cache_sb_k[0].reshape(bs, past, D_MIX), cache_sb_v[0].reshape(bs, past, D_MIX),
        cache_mem_k[0].reshape(bs, n_mem, D_MODEL), cache_mem_v[0].reshape(bs, n_mem, D_MODEL),
        w, chunk=ts, prep_tile=ts, tq=ts)

    y_p, y_s = moe_ffn([tok_p, tok_s], w)

    heads = lambda a, b_, t_: a.reshape(1, b_, t_, N_HEADS, HEAD_DIM)
    xh = lambda a: a.reshape(1, bp, n_mem, N_X_HEADS, X_HEAD_DIM)
    return (y_p.reshape(bp, t, D_MODEL), y_s.reshape(bs, ts, D_MODEL),
            sh_p[None], _state_from_t(st_p)[None], heads(k_p, bp, t), heads(v_p, bp, t), xh(mk_p), xh(mv_p),
            sh_s[None], _state_from_t(st_s)[None], heads(k_s, bs, ts), heads(v_s, bs, ts))
```

```python
import functools

import jax
import jax.numpy as jnp
from jax import lax
from jax.experimental import pallas as pl
from jax.experimental.pallas import tpu as pltpu

F32 = jnp.float32
BF16 = jnp.bfloat16

D_MODEL = 1024
HEAD_DIM = 64
N_HEADS = 8
D_MIX = N_HEADS * HEAD_DIM
LORA_W, LORA_A, LORA_G = 64, 64, 128
D_RW_COLS = 3 * D_MIX + LORA_W + LORA_A + LORA_G
N_X_HEADS = 4
X_HEAD_DIM = 256
N_EXPERTS = 32
TOP_K = 4
D_FF = 1024
SWIGLU_LIMIT = 7.0
SWIGLU_ALPHA = 1.702
MOE_BLOCK = 512
RMS_EPS = 1e-5
GN_EPS = 64e-5

ROW_TILE = 256
ROW_TILE_SHAPE = (8, D_MODEL // 8)
RW_GROUP = 4
RW_LANES = RW_GROUP * HEAD_DIM
RW_SEQS = 4
VMEM_LIMIT = 56 * 1024 * 1024


def _cparams(*sem):
    return pltpu.CompilerParams(dimension_semantics=sem, vmem_limit_bytes=VMEM_LIMIT)


def _cparams_gather():
    return pltpu.CompilerParams(dimension_semantics=("arbitrary",), vmem_limit_bytes=VMEM_LIMIT,
                                disable_bounds_checks=True)


def _resident(shape):
    nd = len(shape)
    return pl.BlockSpec(shape, lambda *_: (0,) * nd)


_NN = ((1,), (0,))
_NT = ((1,), (1,))
_TN = ((0,), (0,))


def _dot(a, b, dims=_NN):
    return lax.dot_general(a.astype(BF16), b.astype(BF16), (dims, ((), ())),
                           preferred_element_type=F32)


def _split(x, n):
    parts, rem = [], x
    for i in range(n):
        p = rem.astype(BF16)
        parts.append(p)
        if i + 1 < n:
            rem = rem - p.astype(F32)
    return parts


def _dot_exact_rhs(a, b_bf16, dims=_NN, n=3):
    out = None
    for p in _split(a, n):
        t = lax.dot_general(p, b_bf16, (dims, ((), ())), preferred_element_type=F32)
        out = t if out is None else out + t
    return out


def _dot_exact_lhs(a_bf16, b, dims=_NN, n=3):
    out = None
    for p in _split(b, n):
        t = lax.dot_general(a_bf16, p, (dims, ((), ())), preferred_element_type=F32)
        out = t if out is None else out + t
    return out


def _dot3(a, b, dims=_NN):
    ah, al = _split(a, 2)
    bh, bl = _split(b, 2)
    dn = (dims, ((), ()))
    return (lax.dot_general(ah, bh, dn, preferred_element_type=F32)
            + (lax.dot_general(ah, bl, dn, preferred_element_type=F32)
               + lax.dot_general(al, bh, dn, preferred_element_type=F32)))


def _rms(x, g):
    return x * lax.rsqrt(jnp.mean(x * x, axis=-1, keepdims=True) + RMS_EPS) * g


def _iota(shape, dim):
    return lax.broadcasted_iota(jnp.int32, shape, dim)


def _block_ones(n, width):
    return (_iota((n, n), 0) // width == _iota((n, n), 1) // width).astype(BF16)


def _bd_rows(x, nblk, width):
    blk = _iota(x.shape, 1) // width
    zero = jnp.zeros_like(x)
    return jnp.concatenate([jnp.where(blk == h, x, zero) for h in range(nblk)], axis=0)


def _norm_rows_kernel(x_ref, g_ref, o_ref):
    o_ref[...] = _rms(x_ref[...], g_ref[...])


def norm_rows(x, g):
    r, d = x.shape
    return pl.pallas_call(
        _norm_rows_kernel,
        out_shape=jax.ShapeDtypeStruct((r, d), F32),
        grid=(1,),
        in_specs=[_resident((r, d)), _resident((1, d))],
        out_specs=_resident((r, d)),
        name="norm_rows",
    )(x, g.reshape(1, d))


SB_COLS = 3 * D_MIX
GATE_COLS = 2 * D_MODEL
D_IN = SB_COLS + D_RW_COLS + GATE_COLS


def _in_proj_kernel(x_ref, g_ref, w_ref, q_ref, k_ref, v_ref, rw_ref, gate_ref, kh_ref, vh_ref):
    h = _rms(x_ref[...], g_ref[...]).astype(BF16)

    def proj(lo, hi):
        return jnp.dot(h, w_ref[:, lo:hi], preferred_element_type=F32)

    q_ref[...] = proj(0, D_MIX).astype(BF16)
    k = proj(D_MIX, 2 * D_MIX)
    v = proj(2 * D_MIX, SB_COLS)
    k_ref[...] = k
    v_ref[...] = v
    kh_ref[...] = k.reshape(kh_ref.shape)
    vh_ref[...] = v.reshape(vh_ref.shape)
    rw_ref[...] = proj(SB_COLS, SB_COLS + D_RW_COLS)
    gate_ref[...] = proj(SB_COLS + D_RW_COLS, D_IN).astype(BF16)


def in_proj(x, g, w_bf16):
    n = x.shape[0]
    tm = min(ROW_TILE, n)
    row = lambda width: pl.BlockSpec((tm, width), lambda i: (i, 0))
    heads = pl.BlockSpec((tm, N_HEADS, HEAD_DIM), lambda i: (i, 0, 0))
    return pl.pallas_call(
        _in_proj_kernel,
        out_shape=(jax.ShapeDtypeStruct((n, D_MIX), BF16),
                   jax.ShapeDtypeStruct((n, D_MIX), F32),
                   jax.ShapeDtypeStruct((n, D_MIX), F32),
                   jax.ShapeDtypeStruct((n, D_RW_COLS), F32),
                   jax.ShapeDtypeStruct((n, GATE_COLS), BF16),
                   jax.ShapeDtypeStruct((n, N_HEADS, HEAD_DIM), F32),
                   jax.ShapeDtypeStruct((n, N_HEADS, HEAD_DIM), F32)),
        grid=(n // tm,),
        in_specs=[row(D_MODEL), _resident((1, D_MODEL)), _resident((D_MODEL, D_IN))],
        out_specs=(row(D_MIX), row(D_MIX), row(D_MIX), row(D_RW_COLS), row(GATE_COLS), heads, heads),
        compiler_params=_cparams("parallel"),
        name="in_proj",
    )(x, g.reshape(1, D_MODEL), w_bf16)


def _plain_proj_kernel(x_ref, w_ref, o_ref):
    o_ref[...] = _dot(x_ref[...], w_ref[...])


def plain_proj(x, w_bf16):
    r, m = x.shape[0], w_bf16.shape[1]
    return pl.pallas_call(
        _plain_proj_kernel,
        out_shape=jax.ShapeDtypeStruct((r, m), F32),
        grid=(1,),
        in_specs=[_resident(x.shape), _resident(w_bf16.shape)],
        out_specs=_resident((r, m)),
        name="plain_proj",
    )(x, w_bf16)


def _rw_prep_kernel(chunk, rw_ref, prev0_ref, mu_ref, w0_ref, a0_ref, kk_ref, ka_ref, rk_ref,
                    wup_ref, aup_ref, gup_ref,
                    rt_ref, at_ref, bt_ref, kt_ref, bh_ref, kh_ref, v_ref, g_ref, bonus_ref, wc_ref,
                    carry_ref):
    tt = rw_ref.shape[1]

    @pl.when(pl.program_id(1) == 0)
    def _():
        carry_ref[...] = prev0_ref[0]

    p = rw_ref[0]
    prev = jnp.where(_iota(p.shape, 0) == 0, carry_ref[...], pltpu.roll(p, 1, 0))
    carry_ref[...] = p[tt - 1:tt]
    xs = p + mu_ref[...] * (prev - p)
    r = xs[:, 0:D_MIX]
    k = xs[:, D_MIX:2 * D_MIX]
    v = xs[:, 2 * D_MIX:3 * D_MIX]
    lo = 3 * D_MIX
    wd = xs[:, lo:lo + LORA_W]
    ad = xs[:, lo + LORA_W:lo + LORA_W + LORA_A]
    gd = xs[:, lo + LORA_W + LORA_A:D_RW_COLS]

    pre_w = w0_ref[...] + _dot(jnp.tanh(wd), wup_ref[...])
    w_log = -jax.nn.softplus(-pre_w) - 0.5
    logw = -jnp.exp(w_log)
    a = jax.nn.sigmoid(a0_ref[...] + _dot(ad, aup_ref[...]))
    g_ref[0] = _dot(jax.nn.sigmoid(gd), gup_ref[...])

    head_ones = _block_ones(D_MIX, HEAD_DIM)
    kk = k * kk_ref[...]
    ss = _dot_exact_rhs(kk * kk, head_ones, n=2)
    kk = kk * jnp.minimum(lax.rsqrt(ss), 1e12)
    k2 = k * (1.0 + (a - 1.0) * ka_ref[...])
    nb = -(kk * a)
    bonus_ref[0] = _dot_exact_rhs(r * k2 * rk_ref[...], head_ones, n=2) * v
    v_ref[0] = v.astype(BF16)

    ti, tj = _iota((tt, tt), 0), _iota((tt, tt), 1)
    same = ti // chunk == tj // chunk
    cum = _dot_exact_lhs((same & (tj <= ti)).astype(BF16), logw)
    tot = jnp.concatenate(
        [jnp.broadcast_to(cum[(c + 1) * chunk - 1:(c + 1) * chunk], (chunk, D_MIX)) for c in range(tt // chunk)],
        axis=0)
    e_in = jnp.exp(cum)
    e_out = jnp.exp(-cum)
    e_end = jnp.exp(tot - cum)
    rt_ref[0] = (r * e_in).astype(BF16)
    at_ref[0] = (kk * jnp.exp(cum - logw)).astype(BF16)
    bt_ref[0] = (nb * e_out).astype(BF16)
    kt_ref[0] = (k2 * e_out).astype(BF16)
    bh_ref[0] = (nb * e_end).astype(BF16)
    kh_ref[0] = (k2 * e_end).astype(BF16)
    etot = jnp.exp(tot)
    for c in range(tt // chunk):
        wc_ref[0, c] = etot[c * chunk:c * chunk + 1]


def rw_prep(rw, prev0, p, chunk, tt):
    b, t, _ = rw.shape
    seq = lambda width, dt: jax.ShapeDtypeStruct((b, t, width), dt)
    tile = lambda width: pl.BlockSpec((1, tt, width), lambda i, j: (i, j, 0))
    vec = lambda width: _resident((1, width))
    n_c = tt // chunk
    return pl.pallas_call(
        functools.partial(_rw_prep_kernel, chunk),
        out_shape=tuple(seq(D_MIX, BF16) for _ in range(7))
        + (seq(D_MIX, F32), seq(D_MIX, F32), jax.ShapeDtypeStruct((b, t // chunk, 1, D_MIX), F32)),
        grid=(b, t // tt),
        in_specs=[tile(D_RW_COLS), pl.BlockSpec((1, 1, D_RW_COLS), lambda i, j: (i, 0, 0)),
                  vec(D_RW_COLS), vec(D_MIX), vec(D_MIX), vec(D_MIX), vec(D_MIX), vec(D_MIX),
                  _resident((LORA_W, D_MIX)), _resident((LORA_A, D_MIX)), _resident((LORA_G, D_MIX))],
        out_specs=tuple(tile(D_MIX) for _ in range(9))
        + (pl.BlockSpec((1, n_c, 1, D_MIX), lambda i, j: (i, j, 0, 0)),),
        scratch_shapes=[pltpu.VMEM((1, D_RW_COLS), F32)],
        compiler_params=_cparams("parallel", "arbitrary"),
        name="rw_prep",
    )(rw, prev0, p["mu"], p["w0"], p["a0"], p["k_k"], p["k_a"], p["r_k"],
      p["w_up"], p["a_up"], p["g_up"])


def _unit_lower_inverse(a, chunk):
    shape = a.shape
    eye = (_iota(shape, 1) % chunk == _iota(shape, 0)).astype(F32)
    res = eye + a
    power = a.astype(BF16)
    for _ in range(chunk.bit_length() - 2):
        power = _dot(power, _bd_rows(power, RW_GROUP, chunk)).astype(BF16)
        yield
        res = res + _dot(res, _bd_rows(power, RW_GROUP, chunk))
    return res


def _interleave(chains):
    live = list(chains)
    while live:
        live = [c for c in live if next(c, StopIteration) is not StopIteration]


def _rw_chunk_kernel(chunk, rt_ref, at_ref, bt_ref, kt_ref, bh_ref, kh_ref, v_ref, g_ref, bonus_ref,
                     wc_ref, s0_ref, gnw_ref, gnb_ref, o_ref, s_out_ref, st_ref):
    c = pl.program_id(1)

    @pl.when(c == 0)
    def _():
        st_ref[...] = s0_ref[...]

    cat = (chunk, RW_GROUP * chunk)
    col_t = _iota(cat, 1) % chunk
    strict = col_t < _iota(cat, 0)
    incl = col_t <= _iota(cat, 0)
    head_ones = _block_ones(RW_LANES, HEAD_DIM)
    bd = lambda x: _bd_rows(x, RW_GROUP, HEAD_DIM)
    n_cat = RW_GROUP * chunk
    def chain(bi, grp):
        sl = slice(grp * RW_LANES, (grp + 1) * RW_LANES)
        rt, at, bt, kt = rt_ref[bi, :, sl], at_ref[bi, :, sl], bt_ref[bi, :, sl], kt_ref[bi, :, sl]
        bh, kh, v = bh_ref[bi, :, sl], kh_ref[bi, :, sl], v_ref[bi, :, sl]
        st = st_ref[bi, :, sl]

        gram = _dot(jnp.concatenate([at, rt], axis=0),
                    jnp.concatenate([bd(bt), bd(kt)], axis=0), _NT)
        yield
        zero = jnp.zeros(cat, F32)
        a_ab = jnp.where(strict, gram[:chunk, :n_cat], zero)
        a_ak = jnp.where(strict, gram[:chunk, n_cat:], zero)
        a_r = jnp.concatenate([jnp.where(incl, gram[chunk:, :n_cat], zero),
                               jnp.where(incl, gram[chunk:, n_cat:], zero)], axis=1).astype(BF16)
        v_bd = bd(v)
        x = _dot(a_ak, v_bd).astype(BF16)
        t_inv = yield from _unit_lower_inverse(a_ab, chunk)
        yield
        ua = _dot(t_inv, jnp.concatenate([bd(x), bd(at)], axis=1))
        st_bd = bd(st.astype(BF16))
        yield
        from_state = _dot(jnp.concatenate([ua[:, RW_LANES:].astype(BF16), rt], axis=0), st_bd)
        yield
        u = (ua[:, :RW_LANES] + from_state[:chunk]).astype(BF16)
        y = from_state[chunk:] + _dot(a_r, jnp.concatenate([bd(u), v_bd], axis=0))
        m = _dot(jnp.concatenate([bh, kh], axis=0), jnp.concatenate([u, v], axis=0), _TN)
        yield
        lane_head = _iota((HEAD_DIM, RW_LANES), 1) // HEAD_DIM
        fold = jnp.zeros((HEAD_DIM, RW_LANES), F32)
        for h in range(RW_GROUP):
            fold = fold + jnp.where(lane_head == h, m[h * HEAD_DIM:(h + 1) * HEAD_DIM], 0.0)
        diag = _iota((HEAD_DIM, RW_LANES), 1) % HEAD_DIM == _iota((HEAD_DIM, RW_LANES), 0)
        w_rows = jnp.where(diag, wc_ref[bi, 0, :, sl], 0.0)
        w_t = _dot_exact_rhs(w_rows, head_ones)
        st_ref[bi, :, sl] = st * w_t + fold

        mean = _dot_exact_rhs(y, head_ones, n=2) * (1.0 / HEAD_DIM)
        yield
        d = y - mean
        var = _dot_exact_rhs(d * d, head_ones, n=2) * (1.0 / HEAD_DIM)
        yield
        yn = d * lax.rsqrt(var + GN_EPS) * gnw_ref[:, sl] + gnb_ref[:, sl]
        o_ref[bi, :, sl] = ((yn + bonus_ref[bi, :, sl]) * g_ref[bi, :, sl]).astype(BF16)

    _interleave(chain(bi, grp) for bi in range(st_ref.shape[0]) for grp in range(N_HEADS // RW_GROUP))

    @pl.when(c == pl.num_programs(1) - 1)
    def _():
        s_out_ref[...] = st_ref[...]


def rw_chunks(prep, s0_t, gn_w, gn_b, chunk):
    rt, at, bt, kt, bh, kh, v, g, bonus, wc = prep
    b, t, _ = rt.shape
    nb = RW_SEQS if b % RW_SEQS == 0 else 1
    tile = pl.BlockSpec((nb, chunk, D_MIX), lambda i, j: (i, j, 0))
    state = pl.BlockSpec((nb, HEAD_DIM, D_MIX), lambda i, j: (i, 0, 0))
    return pl.pallas_call(
        functools.partial(_rw_chunk_kernel, chunk),
        out_shape=(jax.ShapeDtypeStruct((b, t, D_MIX), BF16),
                   jax.ShapeDtypeStruct((b, HEAD_DIM, D_MIX), F32)),
        grid=(b // nb, t // chunk),
        in_specs=[tile] * 9 + [pl.BlockSpec((nb, 1, 1, D_MIX), lambda i, j: (i, j, 0, 0)), state,
                               _resident((1, D_MIX)), _resident((1, D_MIX))],
        out_specs=(tile, state),
        scratch_shapes=[pltpu.VMEM((nb, HEAD_DIM, D_MIX), F32)],
        compiler_params=_cparams("parallel", "arbitrary"),
        name="rw_chunks",
    )(rt, at, bt, kt, bh, kh, v, g, bonus, wc, s0_t, gn_w, gn_b)


SB_PAIR = 2 * HEAD_DIM
SB_PAST_BLOCK = 256
SB_VISITS = 4


def _sb_kernel(tq, n_past, n_q, q_ref, kn_ref, vn_ref, *rest):
    if n_past:
        kp_ref, vp_ref, o_ref, acc_ref, c_ref = rest
    else:
        o_ref, acc_ref, c_ref = rest
    qi = pl.program_id(2)
    q2 = q_ref[0] * (HEAD_DIM ** -0.5)
    lane = _iota(q2.shape, 1)
    zero_q = jnp.zeros_like(q2)
    q_st = jnp.concatenate([jnp.where(lane < HEAD_DIM, q2, zero_q), jnp.where(lane >= HEAD_DIM, q2, zero_q)],
                           axis=0)
    acc_ref[...] = jnp.zeros_like(acc_ref)
    c_ref[...] = jnp.zeros_like(c_ref)

    def visit(k_blk, v_blk, masked):
        tk = k_blk.shape[0]
        later = (_iota((tk, tk), 0) > _iota((tk, tk), 1)).astype(BF16)
        z = _dot(q_st, k_blk, _NT)
        sp = jnp.maximum(z, 0.0) + jnp.log(1.0 + jnp.exp(-jnp.abs(z)))
        if masked:
            before = _iota((2 * tq, tk), 1) < _iota((2 * tq, tk), 0) % tq
            drop = jnp.where(before, sp, 0.0)
        else:
            drop = sp
        srev = _dot(drop, later)
        yield
        c = c_ref[...]
        att = jnp.exp((z - sp) - (c + srev))
        if masked:
            att = jnp.where(before, att, 0.0)
        acc_ref[...] += _dot(att, v_blk)
        c_ref[...] = c + (srev[:, 0:1] + drop[:, 0:1])

    def new_block(blk, masked=False):
        s = pl.multiple_of(blk * tq, tq)
        return visit(kn_ref[0, pl.ds(s, tq), :], vn_ref[0, pl.ds(s, tq), :], masked)

    lead = SB_VISITS - 1
    for n_with in range(min(lead, n_q - 1) + 1):
        @pl.when((qi == n_with) if n_with < lead else (qi >= lead))
        def _():
            _interleave([new_block(qi, masked=True)] + [new_block(qi - 1 - u) for u in range(n_with)])

    left = jnp.maximum(qi - lead, 0)

    def earlier_group(i, carry):
        _interleave(new_block(left - 1 - u - SB_VISITS * i) for u in range(SB_VISITS))
        return carry

    lax.fori_loop(0, left // SB_VISITS, earlier_group, 0)
    rest = left % SB_VISITS

    @pl.when(rest >= 2)
    def _():
        _interleave([new_block(rest - 1), new_block(rest - 2)])

    @pl.when(rest % 2 == 1)
    def _():
        _interleave([new_block(0)])

    if n_past:
        def past_block(blk):
            s = pl.multiple_of(blk * SB_PAST_BLOCK, SB_PAST_BLOCK)
            return visit(kp_ref[0, pl.ds(s, SB_PAST_BLOCK), :], vp_ref[0, pl.ds(s, SB_PAST_BLOCK), :], False)

        def past_group(i, carry):
            _interleave(past_block(n_past - 1 - u - SB_VISITS * i) for u in range(SB_VISITS))
            return carry

        lax.fori_loop(0, n_past // SB_VISITS, past_group, 0)
        _interleave(past_block(blk) for blk in reversed(range(n_past % SB_VISITS)))

    o_ref[0] = jnp.where(lane < HEAD_DIM, acc_ref[:tq], acc_ref[tq:]).astype(BF16)


def stick_breaking(q, k_new, v_new, k_past, v_past, tq):
    b, t, _ = q.shape
    n_pairs = D_MIX // SB_PAIR
    qspec = pl.BlockSpec((1, tq, SB_PAIR), lambda i, p, j: (i, j, p))
    seq = lambda length: pl.BlockSpec((1, length, SB_PAIR), lambda i, p, j: (i, 0, p))
    args, specs, n_past = [q, k_new, v_new], [qspec, seq(t), seq(t)], 0
    if k_past is not None:
        past_len = k_past.shape[1]
        n_past = past_len // SB_PAST_BLOCK
        args += [k_past, v_past]
        specs += [seq(past_len), seq(past_len)]
    return pl.pallas_call(
        functools.partial(_sb_kernel, tq, n_past, t // tq),
        out_shape=jax.ShapeDtypeStruct((b, t, D_MIX), BF16),
        grid=(b, n_pairs, t // tq),
        in_specs=specs,
        out_specs=qspec,
        scratch_shapes=[pltpu.VMEM((2 * tq, SB_PAIR), F32), pltpu.VMEM((2 * tq, 1), F32)],
        compiler_params=_cparams("parallel", "parallel", "arbitrary"),
        name="stick_breaking",
    )(*args)


def _merge_kernel(orw_ref, osb_ref, gate_ref, x_ref, wb0_ref, wb1_ref, wout_ref, gx_ref, wxq_ref,
                  x1_ref, qx_ref):
    g = jax.nn.sigmoid(gate_ref[...].astype(F32))
    mixed = (g[:, :D_MODEL] * jnp.dot(orw_ref[...], wb0_ref[...], preferred_element_type=F32)
             + g[:, D_MODEL:] * jnp.dot(osb_ref[...], wb1_ref[...], preferred_element_type=F32))
    x1 = x_ref[...] + _dot(mixed, wout_ref[...])
    x1_ref[...] = x1
    qx_ref[...] = _dot(_rms(x1, gx_ref[...]), wxq_ref[...]).astype(BF16)


def merge_out(o_rw, o_sb, gate, x, wb0, wb1, wout, gx, wxq):
    n = x.shape[0]
    tm = min(ROW_TILE, n)
    row = lambda width: pl.BlockSpec((tm, width), lambda i: (i, 0))
    return pl.pallas_call(
        _merge_kernel,
        out_shape=(jax.ShapeDtypeStruct((n, D_MODEL), F32), jax.ShapeDtypeStruct((n, D_MODEL), BF16)),
        grid=(n // tm,),
        in_specs=[row(D_MIX), row(D_MIX), row(GATE_COLS), row(D_MODEL),
                  _resident((D_MIX, D_MODEL)), _resident((D_MIX, D_MODEL)), _resident((D_MODEL, D_MODEL)),
                  _resident((1, D_MODEL)), _resident((D_MODEL, D_MODEL))],
        out_specs=(row(D_MODEL), row(D_MODEL)),
        compiler_params=_cparams("parallel"),
        name="merge_out",
    )(o_rw, o_sb, gate, x, wb0, wb1, wout, gx.reshape(1, D_MODEL), wxq)


def _mem_kv_kernel(m_ref, g_ref, wk_ref, wv_ref, k_ref, v_ref):
    h = _rms(m_ref[...], g_ref[...]).astype(BF16)
    k_ref[...] = jnp.dot(h, wk_ref[...], preferred_element_type=F32)
    v_ref[...] = jnp.dot(h, wv_ref[...], preferred_element_type=F32)


def mem_kv(mem, g, wk, wv):
    n = mem.shape[0]
    tm = min(ROW_TILE, n)
    row = pl.BlockSpec((tm, D_MODEL), lambda i: (i, 0))
    return pl.pallas_call(
        _mem_kv_kernel,
        out_shape=(jax.ShapeDtypeStruct((n, D_MODEL), F32),) * 2,
        grid=(n // tm,),
        in_specs=[row, _resident((1, D_MODEL)), _resident((D_MODEL, D_MODEL)), _resident((D_MODEL, D_MODEL))],
        out_specs=(row, row),
        compiler_params=_cparams("parallel"),
        name="mem_kv",
    )(mem, g.reshape(1, D_MODEL), wk, wv)


ROUTER_LANES = 128


def _xattn_kernel(q_ref, x1_ref, mk_ref, mv_ref, wxo_ref, gf_ref, wr_ref, br_ref,
                  x2_ref, hf_ref, idx_ref, gates_ref, counts_ref):
    q = q_ref[0]
    heads = [None] * N_X_HEADS

    def head(h):
        sl = slice(h * X_HEAD_DIM, (h + 1) * X_HEAD_DIM)
        s = _dot(q[:, sl], mk_ref[0, :, sl], _NT) * (X_HEAD_DIM ** -0.5)
        yield
        e = jnp.exp(s - jnp.max(s, axis=-1, keepdims=True))
        p = e * (1.0 / jnp.sum(e, axis=-1, keepdims=True))
        heads[h] = _dot(p, mv_ref[0, :, sl])

    _interleave(head(h) for h in range(N_X_HEADS))
    x2 = x1_ref[0] + _dot(jnp.concatenate(heads, axis=1), wxo_ref[...])
    x2_ref[0] = x2
    hf = _rms(x2, gf_ref[...])
    hf_ref[0] = hf.reshape(hf_ref.shape[1:])
    logits = _dot3(hf, wr_ref[...]) + br_ref[...]
    lane = _iota(logits.shape, 1)
    vals, idx_out = [], jnp.zeros(logits.shape, jnp.int32)
    chosen = jnp.zeros(logits.shape, F32)
    for j in range(TOP_K):
        m = jnp.max(logits, axis=-1, keepdims=True)
        pick = jnp.min(jnp.where(logits == m, lane, ROUTER_LANES), axis=-1, keepdims=True)
        vals.append(m)
        idx_out = jnp.where(lane == j, pick, idx_out)
        chosen = chosen + (lane == pick).astype(F32)
        logits = jnp.where(lane == pick, -jnp.inf, logits)

    @pl.when((pl.program_id(0) == 0) & (pl.program_id(1) == 0))
    def _():
        counts_ref[...] = jnp.zeros_like(counts_ref)

    counts_ref[...] += jnp.broadcast_to(jnp.sum(chosen, axis=0, keepdims=True), counts_ref.shape)
    exps = [jnp.exp(v - vals[0]) for v in vals]
    denom = exps[0] + exps[1] + exps[2] + exps[3]
    gates = jnp.zeros(logits.shape, F32)
    for j in range(TOP_K):
        gates = jnp.where(lane == j, exps[j] / denom, gates)
    idx_ref[0] = idx_out
    gates_ref[0] = gates


def xattn_router(qx, x1, mk, mv, wxo, g_ffn, w_router, b_router):
    b, t, _ = x1.shape
    tm = min(ROW_TILE, t)
    n_mem = mk.shape[1]
    wr = jnp.zeros((D_MODEL, ROUTER_LANES), F32).at[:, :N_EXPERTS].set(w_router)
    br = jnp.full((1, ROUTER_LANES), -jnp.inf, F32).at[0, :N_EXPERTS].set(b_router)
    tile = pl.BlockSpec((1, tm, D_MODEL), lambda i, j: (i, j, 0))
    mem = pl.BlockSpec((1, n_mem, D_MODEL), lambda i, j: (i, 0, 0))
    small = pl.BlockSpec((1, tm, ROUTER_LANES), lambda i, j: (i, j, 0))
    return pl.pallas_call(
        _xattn_kernel,
        out_shape=(jax.ShapeDtypeStruct((b, t, D_MODEL), F32), jax.ShapeDtypeStruct((b, t) + ROW_TILE_SHAPE, F32),
                   jax.ShapeDtypeStruct((b, t, ROUTER_LANES), jnp.int32),
                   jax.ShapeDtypeStruct((b, t, ROUTER_LANES), F32),
                   jax.ShapeDtypeStruct((ROUTE_SUBLANES, ROUTER_LANES), F32)),
        grid=(b, t // tm),
        in_specs=[tile, tile, mem, mem, _resident((D_MODEL, D_MODEL)), _resident((1, D_MODEL)),
                  _resident((D_MODEL, ROUTER_LANES)), _resident((1, ROUTER_LANES))],
        out_specs=(tile, pl.BlockSpec((1, tm) + ROW_TILE_SHAPE, lambda i, j: (i, j, 0, 0)), small, small,
                   _resident((ROUTE_SUBLANES, ROUTER_LANES))),
        compiler_params=_cparams("arbitrary", "arbitrary"),
        name="xattn_router",
    )(qx, x1, mk, mv, wxo, g_ffn.reshape(1, D_MODEL), wr, br)


ROW_GROUP = 8
DMA_QUEUES = 2
COMBINE_STRIPS = 8
ROUTE_SUBLANES = 8


def _route_kernel(idx_ref, total_ref, dest_ref, carry_ref, start_ref):
    tm = idx_ref.shape[0]
    idx = idx_ref[...]
    lane = _iota((tm, ROUTER_LANES), 1)
    picks = [idx[:, j:j + 1] for j in range(TOP_K)]
    onehot = jnp.zeros((tm, ROUTER_LANES), F32)
    for pick in picks:
        onehot = onehot + (lane == pick).astype(F32)

    @pl.when(pl.program_id(0) == 0)
    def _():
        carry_ref[...] = jnp.zeros_like(carry_ref)
        padded = jnp.floor((total_ref[...] + (MOE_BLOCK - 1.0)) * (1.0 / MOE_BLOCK)) * MOE_BLOCK
        earlier = (_iota((ROUTER_LANES, ROUTER_LANES), 0) < _iota((ROUTER_LANES, ROUTER_LANES), 1))
        start_ref[...] = _dot_exact_rhs(padded, earlier.astype(BF16))

    below = (_iota((tm, tm), 1) < _iota((tm, tm), 0)).astype(BF16)
    row = _dot(below, onehot) + (carry_ref[0:1, :] + start_ref[0:1, :])
    dest = jnp.zeros((tm, ROUTER_LANES), F32)
    for j, pick in enumerate(picks):
        mine = jnp.sum(jnp.where(lane == pick, row, 0.0), axis=1, keepdims=True)
        dest = jnp.where(lane == j, mine, dest)
    dest_ref[0] = dest.T[:ROUTE_SUBLANES].astype(jnp.int32)
    carry_ref[...] += jnp.broadcast_to(jnp.sum(onehot, axis=0, keepdims=True), carry_ref.shape)


def moe_route(idx, total, tm):
    n = idx.shape[0]
    n_tiles = n // tm
    return pl.pallas_call(
        _route_kernel,
        out_shape=jax.ShapeDtypeStruct((n_tiles, ROUTE_SUBLANES, tm), jnp.int32),
        grid=(n_tiles,),
        in_specs=[pl.BlockSpec((tm, ROUTER_LANES), lambda t: (t, 0)), _resident((ROUTE_SUBLANES, ROUTER_LANES))],
        out_specs=pl.BlockSpec((1, ROUTE_SUBLANES, tm), lambda t: (t, 0, 0)),
        scratch_shapes=[pltpu.VMEM((ROUTE_SUBLANES, ROUTER_LANES), F32),
                        pltpu.VMEM((ROUTE_SUBLANES, ROUTER_LANES), F32)],
        compiler_params=_cparams("arbitrary"),
        name="moe_route",
    )(idx, total)


def _block_tables(counts, n_blocks):
    cnt = counts[0, :N_EXPERTS].astype(jnp.int32)
    padded = (cnt + MOE_BLOCK - 1) // MOE_BLOCK * MOE_BLOCK
    pends = jnp.cumsum(padded).astype(jnp.int32)
    block_start = jnp.arange(n_blocks, dtype=jnp.int32) * MOE_BLOCK
    block_expert = jnp.minimum(jnp.sum((pends[None, :] <= block_start[:, None]).astype(jnp.int32), axis=1),
                               N_EXPERTS - 1)
    n_used = (pends[-1:] // MOE_BLOCK).astype(jnp.int32)
    return pends, block_expert, n_used


def _dispatch_kernel(bounds, pends_ref, dest_ref, *rest):
    hf_refs = rest[:len(bounds) - 1]
    xs_hbm, zbuf, sem = rest[len(bounds) - 1:]
    i = pl.program_id(0)

    @pl.when(i == 0)
    def _():
        zbuf[...] = jnp.zeros_like(zbuf)
        n_blocks = xs_hbm.shape[0] // MOE_BLOCK
        n_used = pends_ref[N_EXPERTS - 1] // MOE_BLOCK
        for wait in (False, True):
            def unused_block(blk, carry):
                cp = pltpu.make_async_copy(
                    zbuf, xs_hbm.at[pl.ds(pl.multiple_of(blk * MOE_BLOCK, MOE_BLOCK), MOE_BLOCK)], sem)
                cp.wait() if wait else cp.start()
                return carry

            lax.fori_loop(n_used, n_blocks, unused_block, 0)
            for e in range(N_EXPERTS):
                end = pends_ref[e]
                begin = pends_ref[e - 1] if e else 0

                @pl.when(end > begin)
                def _():
                    last = pl.multiple_of(end - MOE_BLOCK, MOE_BLOCK)
                    cp = pltpu.make_async_copy(zbuf, xs_hbm.at[pl.ds(last, MOE_BLOCK)], sem)
                    cp.wait() if wait else cp.start()

    for hf_ref, lo, hi in zip(hf_refs, bounds[:-1], bounds[1:]):
        @pl.when((i >= lo) & (i < hi))
        def _():
            tm = hf_ref.shape[0]
            for j in range(TOP_K):
                def issue(g, carry):
                    base = pl.multiple_of(g * ROW_GROUP, ROW_GROUP)
                    rows = hf_ref.at[pl.ds(base, ROW_GROUP)]
                    for u in range(ROW_GROUP):
                        pltpu.make_async_copy(rows.at[pl.ds(u, 1)], xs_hbm.at[pl.ds(dest_ref[j * tm + base + u], 1)],
                                              sem).start(priority=u % DMA_QUEUES)
                    return carry

                lax.fori_loop(0, tm // ROW_GROUP, issue, 0)
            for j in range(TOP_K):
                pltpu.make_async_copy(hf_ref, xs_hbm.at[pl.ds(0, tm)], sem).wait()


def moe_dispatch(hfs, bounds, dest, pends, n_rows):
    n_tiles, _, tm = dest.shape
    hf_spec = lambda lo, hi: pl.BlockSpec((tm,) + ROW_TILE_SHAPE,
                                          lambda i, pe: (jnp.clip(i - lo, 0, hi - lo - 1), 0, 0))
    grid_spec = pltpu.PrefetchScalarGridSpec(
        num_scalar_prefetch=1,
        grid=(n_tiles,),
        in_specs=[pl.BlockSpec((ROUTE_SUBLANES * tm,), lambda i, pe: (i,), memory_space=pltpu.SMEM)]
        + [hf_spec(lo, hi) for lo, hi in zip(bounds[:-1], bounds[1:])],
        out_specs=pl.BlockSpec(memory_space=pl.ANY),
        scratch_shapes=[pltpu.VMEM((MOE_BLOCK,) + ROW_TILE_SHAPE, F32), pltpu.SemaphoreType.DMA(())],
    )
    return pl.pallas_call(
        functools.partial(_dispatch_kernel, tuple(bounds)),
        out_shape=jax.ShapeDtypeStruct((n_rows,) + ROW_TILE_SHAPE, F32),
        grid_spec=grid_spec,
        compiler_params=_cparams_gather(),
        name="moe_dispatch",
    )(pends, dest.reshape(-1), *hfs)


W1_CHUNKS = 4
W2_CHUNKS = 2


def _moe_kernel(be_ref, nused_ref, x_ref, *rest):
    w1 = rest[:W1_CHUNKS]
    b1_ref = rest[W1_CHUNKS]
    w2 = rest[W1_CHUNKS + 1:W1_CHUNKS + 1 + W2_CHUNKS]
    b2_ref, o_ref, w1_bf, w2_bf = rest[W1_CHUNKS + 1 + W2_CHUNKS:]
    i = pl.program_id(0)
    half = W1_CHUNKS // 2
    wide = 2 * D_FF // W1_CHUNKS

    @pl.when((i < nused_ref[0]) & ((i == 0) | (be_ref[i] != be_ref[jnp.maximum(i - 1, 0)])))
    def _():
        for c in range(W1_CHUNKS):
            w1_bf[c] = w1[c][0].astype(BF16)
        for c in range(W2_CHUNKS):
            w2_bf[c] = w2[c][0].astype(BF16)

    @pl.when(i < nused_ref[0])
    def _():
        x = x_ref[...].reshape(MOE_BLOCK, D_MODEL).astype(BF16)
        parts = [None] * half

        def ff_slice(c):
            glu = jnp.dot(x, w1_bf[c], preferred_element_type=F32) + b1_ref[0, :, c * wide:(c + 1) * wide]
            lin = (jnp.dot(x, w1_bf[half + c], preferred_element_type=F32)
                   + b1_ref[0, :, D_FF + c * wide:D_FF + (c + 1) * wide])
            yield
            glu = jnp.minimum(glu, SWIGLU_LIMIT)
            lin = jnp.clip(lin, -SWIGLU_LIMIT, SWIGLU_LIMIT)
            act = glu * jax.nn.sigmoid(SWIGLU_ALPHA * glu) * (lin + 1.0)
            parts[c] = _dot(act, w2_bf[c])

        _interleave(ff_slice(c) for c in range(half))
        o_ref[...] = (b2_ref[0] + sum(parts[1:], parts[0])).reshape(o_ref.shape)

    @pl.when(i >= nused_ref[0])
    def _():
        o_ref[...] = jnp.zeros_like(o_ref)


def moe_experts(xs, block_expert, n_used, w1, b1, w2, b2):
    n_blocks = block_expert.shape[0]
    assert W1_CHUNKS // 2 == W2_CHUNKS
    c1 = 2 * D_FF // W1_CHUNKS
    c2 = D_FF // W2_CHUNKS
    w1_spec = lambda c: pl.BlockSpec((1, D_MODEL, c1), lambda i, be, nu: (be[i], 0, c))
    w2_spec = lambda c: pl.BlockSpec((1, c2, D_MODEL), lambda i, be, nu: (be[i], c, 0))
    grid_spec = pltpu.PrefetchScalarGridSpec(
        num_scalar_prefetch=2,
        grid=(n_blocks,),
        in_specs=[pl.BlockSpec((MOE_BLOCK,) + ROW_TILE_SHAPE, lambda i, be, nu: (jnp.minimum(i, nu[0] - 1), 0, 0))]
        + [w1_spec(c) for c in range(W1_CHUNKS)]
        + [pl.BlockSpec((1, 1, 2 * D_FF), lambda i, be, nu: (be[i], 0, 0))]
        + [w2_spec(c) for c in range(W2_CHUNKS)]
        + [pl.BlockSpec((1, 1, D_MODEL), lambda i, be, nu: (be[i], 0, 0))],
        out_specs=pl.BlockSpec((MOE_BLOCK,) + ROW_TILE_SHAPE, lambda i, be, nu: (i, 0, 0)),
        scratch_shapes=[pltpu.VMEM((W1_CHUNKS, D_MODEL, c1), BF16), pltpu.VMEM((W2_CHUNKS, c2, D_MODEL), BF16)],
    )
    return pl.pallas_call(
        _moe_kernel,
        out_shape=jax.ShapeDtypeStruct((n_blocks * MOE_BLOCK,) + ROW_TILE_SHAPE, F32),
        grid_spec=grid_spec,
        compiler_params=_cparams("arbitrary"),
        name="moe_experts",
    )(block_expert, n_used, xs, *([w1] * W1_CHUNKS), b1.reshape(N_EXPERTS, 1, 2 * D_FF),
      *([w2] * W2_CHUNKS), b2.reshape(N_EXPERTS, 1, D_MODEL))


def _start_combine_gather(dest_ref, os_hbm, dst, sem):
    tm = dst.shape[0] // TOP_K
    for j in range(TOP_K):
        def issue(g, carry):
            base = pl.multiple_of(g * ROW_GROUP, ROW_GROUP)
            rows = dst.at[pl.ds(j * tm + base, ROW_GROUP)]
            for u in range(ROW_GROUP):
                pltpu.make_async_copy(os_hbm.at[pl.ds(dest_ref[j * tm + base + u], 1)], rows.at[pl.ds(u, 1)],
                                      sem).start(priority=u % DMA_QUEUES)
            return carry

        lax.fori_loop(0, tm // ROW_GROUP, issue, 0)


def _combine_kernel(n_tiles, dest_ref, dest_next_ref, os_hbm, x2_ref, gates_ref, gfin_ref, y_ref, buf, sem):
    tm = x2_ref.shape[0]
    i = pl.program_id(0)
    slot = i % 2

    @pl.when(i == 0)
    def _():
        _start_combine_gather(dest_ref, os_hbm, buf.at[0], sem.at[0])

    pltpu.make_async_copy(os_hbm.at[pl.ds(0, TOP_K * tm)], buf.at[slot], sem.at[slot]).wait()
    gates = gates_ref[...]
    nxt = buf.at[1 - slot]
    strip = tm // COMBINE_STRIPS
    for s in range(COMBINE_STRIPS):
        lo = s * strip
        y = x2_ref[lo:lo + strip]
        for j in range(TOP_K):
            y = y + gates[lo:lo + strip, j:j + 1] * buf[slot, j * tm + lo:j * tm + lo + strip].reshape(strip, D_MODEL)
        y_ref[lo:lo + strip] = _rms(y, gfin_ref[...])
        for j in range(TOP_K):
            for r in range(lo, lo + strip):
                pltpu.make_async_copy(os_hbm.at[pl.ds(dest_next_ref[j * tm + r], 1)], nxt.at[pl.ds(j * tm + r, 1)],
                                      sem.at[1 - slot]).start(priority=r % DMA_QUEUES)

    @pl.when(i == n_tiles - 1)
    def _():
        pltpu.make_async_copy(os_hbm.at[pl.ds(0, TOP_K * tm)], nxt, sem.at[1 - slot]).wait()


def moe_combine(os_rows, dest, x2, gates, g_final):
    n = x2.shape[0]
    n_tiles, _, tm = dest.shape
    dest_spec = lambda step: pl.BlockSpec(
        (ROUTE_SUBLANES * tm,), lambda i: (jnp.minimum(i + step, n_tiles - 1),), memory_space=pltpu.SMEM)
    dest = dest.reshape(-1)
    return pl.pallas_call(
        functools.partial(_combine_kernel, n_tiles),
        out_shape=jax.ShapeDtypeStruct((n, D_MODEL), F32),
        grid=(n_tiles,),
        in_specs=[dest_spec(0), dest_spec(1),
                  pl.BlockSpec(memory_space=pl.ANY),
                  pl.BlockSpec((tm, D_MODEL), lambda i: (i, 0)),
                  pl.BlockSpec((tm, ROUTER_LANES), lambda i: (i, 0)),
                  _resident((1, D_MODEL))],
        out_specs=pl.BlockSpec((tm, D_MODEL), lambda i: (i, 0)),
        scratch_shapes=[pltpu.VMEM((2, TOP_K * tm) + ROW_TILE_SHAPE, F32), pltpu.SemaphoreType.DMA((2,))],
        compiler_params=_cparams_gather(),
        name="moe_combine",
    )(dest, dest, os_rows, x2, gates, g_final.reshape(1, D_MODEL))


def moe_ffn(groups, w):
    sizes = [g[0].shape[0] for g in groups]
    n = sum(sizes)
    tm = next(c for c in (256, 128, 64, 32, 16, 8) if all(s % c == 0 for s in sizes))
    counts = sum(g[4] for g in groups[1:]) + groups[0][4]
    dest = moe_route(jnp.concatenate([g[2] for g in groups], axis=0), counts, tm)
    n_blocks = -(-(n * TOP_K + N_EXPERTS * (MOE_BLOCK - 1)) // MOE_BLOCK)
    pends, block_expert, n_used = _block_tables(counts, n_blocks)
    bounds = [0]
    for s in sizes:
        bounds.append(bounds[-1] + s // tm)
    xs = moe_dispatch([g[1] for g in groups], bounds, dest, pends, n_blocks * MOE_BLOCK)
    os_rows = moe_experts(xs, block_expert, n_used, w["w_e1"], w["b_e1"], w["w_e2"], w["b_e2"])
    return [moe_combine(os_rows, dest[lo:hi], g[0], g[3], w["g_final"])
            for g, lo, hi in zip(groups, bounds[:-1], bounds[1:])]


def _mixers(x, prev0, s0_t, k_past, v_past, mk, mv, w, chunk, prep_tile, tq):
    b, t, _ = x.shape
    n = b * t
    xf = x.reshape(n, D_MODEL)
    q, k_sb, v_sb, rw, gate, k_heads, v_heads = in_proj(xf, w["g_mix"], w["w_in"])
    shift = norm_rows(x[:, -1], w["g_mix"])
    prep = rw_prep(rw.reshape(b, t, D_RW_COLS), prev0, w["rw"], chunk, prep_tile)
    o_rw, s_t = rw_chunks(prep, s0_t, w["gn_w"], w["gn_b"], chunk)
    seq = lambda a: a.reshape(b, t, D_MIX)
    o_sb = stick_breaking(seq(q), seq(k_sb), seq(v_sb), k_past, v_past, tq)
    x1, qx = merge_out(o_rw.reshape(n, D_MIX), o_sb.reshape(n, D_MIX), gate, xf,
                       w["wb0"], w["wb1"], w["w_out"], w["g_xattn"], w["w_xq"])
    x2, hf, idx, gates, counts = xattn_router(qx.reshape(b, t, D_MODEL), x1.reshape(b, t, D_MODEL), mk, mv,
                                              w["w_xo"], w["g_ffn"], w["w_router"], w["b_router"])
    tok = (x2.reshape(n, D_MODEL), hf.reshape((n,) + ROW_TILE_SHAPE), idx.reshape(n, ROUTER_LANES),
           gates.reshape(n, ROUTER_LANES), counts)
    return tok, shift, s_t, k_heads, v_heads


def _state_to_t(s):
    b = s.shape[0]
    return s.transpose(0, 3, 1, 2).reshape(b, HEAD_DIM, D_MIX)


def _state_from_t(s_t):
    b = s_t.shape[0]
    return s_t.reshape(b, HEAD_DIM, N_HEADS, HEAD_DIM).transpose(0, 2, 3, 1)


def kernel(x_prompt, x_sample, state_rw_shift, state_rw_wkv, cache_sb_k, cache_sb_v, cache_mem_k, cache_mem_v, mem_prompt, g_mix, w_in, rw_mu, rw_w0, rw_w_up, rw_a0, rw_a_up, rw_g_up, rw_k_k, rw_k_a, rw_r_k, rw_gn_w, rw_gn_b, w_branch, w_out, g_xattn, g_mem, w_xq, w_mk, w_mv, w_xo, g_ffn, w_router, b_router, w_e1, b_e1, w_e2, b_e2, g_final):
    assert g_mix.shape[0] == 1, "single-layer trunk"
    row = lambda a: a.reshape(1, -1)
    w = dict(
        g_mix=g_mix[0], w_in=w_in[0].astype(BF16),
        rw=dict(mu=row(rw_mu[0]), w0=row(rw_w0[0]), a0=row(rw_a0[0]), k_k=row(rw_k_k[0]), k_a=row(rw_k_a[0]),
                r_k=row(rw_r_k[0]), w_up=rw_w_up[0].astype(BF16), a_up=rw_a_up[0].astype(BF16),
                g_up=rw_g_up[0].astype(BF16)),
        gn_w=row(rw_gn_w[0]), gn_b=row(rw_gn_b[0]),
        wb0=w_branch[0, 0].astype(BF16), wb1=w_branch[0, 1].astype(BF16), w_out=w_out[0].astype(BF16),
        g_xattn=g_xattn[0], w_xq=w_xq[0].astype(BF16), w_xo=w_xo[0].astype(BF16),
        g_ffn=g_ffn[0], w_router=w_router[0], b_router=b_router[0],
        w_e1=w_e1[0], b_e1=b_e1[0], w_e2=w_e2[0], b_e2=b_e2[0],
        g_final=g_final,
    )
    bp, t, _ = x_prompt.shape
    bs, ts, _ = x_sample.shape
    n_mem = mem_prompt.shape[1]

    mk_p, mv_p = mem_kv(mem_prompt.reshape(bp * n_mem, D_MODEL), g_mem[0],
                        w_mk[0].astype(BF16), w_mv[0].astype(BF16))
    mk_p = mk_p.reshape(bp, n_mem, D_MODEL)
    mv_p = mv_p.reshape(bp, n_mem, D_MODEL)
    tok_p, sh_p, st_p, k_p, v_p = _mixers(
        x_prompt, jnp.zeros((bp, 1, D_RW_COLS), F32), jnp.zeros((bp, HEAD_DIM, D_MIX), F32),
        None, None, mk_p, mv_p, w, chunk=64, prep_tile=256, tq=256)

    prev_s = plain_proj(state_rw_shift[0], w["w_in"][:, SB_COLS:SB_COLS + D_RW_COLS])
    past = cache_sb_k.shape[2]
    tok_s, sh_s, st_s, k_s, v_s = _mixers(
        x_sample, prev_s.reshape(bs, 1, D_RW_COLS), _state_to_t(state_rw_wkv[0]),
        cache_sb_k[0].reshape(bs, past, D_MIX), cache_sb_v[0].reshape(bs, past, D_MIX),
        cache_mem_k[0].reshape(bs, n_mem, D_MODEL), cache_mem_v[0].reshape(bs, n_mem, D_MODEL),
        w, chunk=ts, prep_tile=ts, tq=ts)

    y_p, y_s = moe_ffn([tok_p, tok_s], w)

    heads = lambda a, b_, t_: a.reshape(1, b_, t_, N_HEADS, HEAD_DIM)
    xh = lambda a: a.reshape(1, bp, n_mem, N_X_HEADS, X_HEAD_DIM)
    return (y_p.reshape(bp, t, D_MODEL), y_s.reshape(bs, ts, D_MODEL),
            sh_p[None], _state_from_t(st_p)[None], heads(k_p, bp, t), heads(v_p, bp, t), xh(mk_p), xh(mv_p),
            sh_s[None], _state_from_t(st_s)[None], heads(k_s, bs, ts), heads(v_s, bs, ts))
```

```python
import functools

import jax
import jax.numpy as jnp
from jax import lax
from jax.experimental import pallas as pl
from jax.experimental.pallas import tpu as pltpu

F32 = jnp.float32
BF16 = jnp.bfloat16

D_MODEL = 1024
HEAD_DIM = 64
N_HEADS = 8
D_MIX = N_HEADS * HEAD_DIM
LORA_W, LORA_A, LORA_G = 64, 64, 128
D_RW_COLS = 3 * D_MIX + LORA_W + LORA_A + LORA_G
N_X_HEADS = 4
X_HEAD_DIM = 256
N_EXPERTS = 32
TOP_K = 4
D_FF = 1024
SWIGLU_LIMIT = 7.0
SWIGLU_ALPHA = 1.702
MOE_BLOCK = 512
RMS_EPS = 1e-5
GN_EPS = 64e-5

ROW_TILE = 256
ROW_TILE_SHAPE = (8, D_MODEL // 8)
RW_GROUP = 4
RW_LANES = RW_GROUP * HEAD_DIM
RW_SEQS = 4
VMEM_LIMIT = 56 * 1024 * 1024


def _cparams(*sem):
    return pltpu.CompilerParams(dimension_semantics=sem, vmem_limit_bytes=VMEM_LIMIT)


def _cparams_gather():
    return pltpu.CompilerParams(dimension_semantics=("arbitrary",), vmem_limit_bytes=VMEM_LIMIT,
                                disable_bounds_checks=True)


def _resident(shape):
    nd = len(shape)
    return pl.BlockSpec(shape, lambda *_: (0,) * nd)


_NN = ((1,), (0,))
_NT = ((1,), (1,))
_TN = ((0,), (0,))


def _dot(a, b, dims=_NN):
    return lax.dot_general(a.astype(BF16), b.astype(BF16), (dims, ((), ())),
                           preferred_element_type=F32)


def _split(x, n):
    parts, rem = [], x
    for i in range(n):
        p = rem.astype(BF16)
        parts.append(p)
        if i + 1 < n:
            rem = rem - p.astype(F32)
    return parts


def _dot_exact_rhs(a, b_bf16, dims=_NN, n=3):
    out = None
    for p in _split(a, n):
        t = lax.dot_general(p, b_bf16, (dims, ((), ())), preferred_element_type=F32)
        out = t if out is None else out + t
    return out


def _dot_exact_lhs(a_bf16, b, dims=_NN, n=3):
    out = None
    for p in _split(b, n):
        t = lax.dot_general(a_bf16, p, (dims, ((), ())), preferred_element_type=F32)
        out = t if out is None else out + t
    return out


def _dot3(a, b, dims=_NN):
    ah, al = _split(a, 2)
    bh, bl = _split(b, 2)
    dn = (dims, ((), ()))
    return (lax.dot_general(ah, bh, dn, preferred_element_type=F32)
            + (lax.dot_general(ah, bl, dn, preferred_element_type=F32)
               + lax.dot_general(al, bh, dn, preferred_element_type=F32)))


def _rms(x, g):
    return x * lax.rsqrt(jnp.mean(x * x, axis=-1, keepdims=True) + RMS_EPS) * g


def _iota(shape, dim):
    return lax.broadcasted_iota(jnp.int32, shape, dim)


def _block_ones(n, width):
    return (_iota((n, n), 0) // width == _iota((n, n), 1) // width).astype(BF16)


def _bd_rows(x, nblk, width):
    blk = _iota(x.shape, 1) // width
    zero = jnp.zeros_like(x)
    return jnp.concatenate([jnp.where(blk == h, x, zero) for h in range(nblk)], axis=0)


def _norm_rows_kernel(x_ref, g_ref, o_ref):
    o_ref[...] = _rms(x_ref[...], g_ref[...])


def norm_rows(x, g):
    r, d = x.shape
    return pl.pallas_call(
        _norm_rows_kernel,
        out_shape=jax.ShapeDtypeStruct((r, d), F32),
        grid=(1,),
        in_specs=[_resident((r, d)), _resident((1, d))],
        out_specs=_resident((r, d)),
        name="norm_rows",
    )(x, g.reshape(1, d))


SB_COLS = 3 * D_MIX
GATE_COLS = 2 * D_MODEL
D_IN = SB_COLS + D_RW_COLS + GATE_COLS


def _in_proj_kernel(x_ref, g_ref, w_ref, q_ref, k_ref, v_ref, rw_ref, gate_ref, kh_ref, vh_ref):
    h = _rms(x_ref[...], g_ref[...]).astype(BF16)

    def proj(lo, hi):
        return jnp.dot(h, w_ref[:, lo:hi], preferred_element_type=F32)

    q_ref[...] = proj(0, D_MIX).astype(BF16)
    k = proj(D_MIX, 2 * D_MIX)
    v = proj(2 * D_MIX, SB_COLS)
    k_ref[...] = k.astype(BF16)
    v_ref[...] = v.astype(BF16)
    kh_ref[...] = k.reshape(kh_ref.shape)
    vh_ref[...] = v.reshape(vh_ref.shape)
    rw_ref[...] = proj(SB_COLS, SB_COLS + D_RW_COLS)
    gate_ref[...] = proj(SB_COLS + D_RW_COLS, D_IN).astype(BF16)


def in_proj(x, g, w_bf16):
    n = x.shape[0]
    tm = min(ROW_TILE, n)
    row = lambda width: pl.BlockSpec((tm, width), lambda i: (i, 0))
    heads = pl.BlockSpec((tm, N_HEADS, HEAD_DIM), lambda i: (i, 0, 0))
    return pl.pallas_call(
        _in_proj_kernel,
        out_shape=(jax.ShapeDtypeStruct((n, D_MIX), BF16),
                   jax.ShapeDtypeStruct((n, D_MIX), BF16),
                   jax.ShapeDtypeStruct((n, D_MIX), BF16),
                   jax.ShapeDtypeStruct((n, D_RW_COLS), F32),
                   jax.ShapeDtypeStruct((n, GATE_COLS), BF16),
                   jax.ShapeDtypeStruct((n, N_HEADS, HEAD_DIM), F32),
                   jax.ShapeDtypeStruct((n, N_HEADS, HEAD_DIM), F32)),
        grid=(n // tm,),
        in_specs=[row(D_MODEL), _resident((1, D_MODEL)), _resident((D_MODEL, D_IN))],
        out_specs=(row(D_MIX), row(D_MIX), row(D_MIX), row(D_RW_COLS), row(GATE_COLS), heads, heads),
        compiler_params=_cparams("parallel"),
        name="in_proj",
    )(x, g.reshape(1, D_MODEL), w_bf16)


def _plain_proj_kernel(x_ref, w_ref, o_ref):
    o_ref[...] = _dot(x_ref[...], w_ref[...])


def plain_proj(x, w_bf16):
    r, m = x.shape[0], w_bf16.shape[1]
    return pl.pallas_call(
        _plain_proj_kernel,
        out_shape=jax.ShapeDtypeStruct((r, m), F32),
        grid=(1,),
        in_specs=[_resident(x.shape), _resident(w_bf16.shape)],
        out_specs=_resident((r, m)),
        name="plain_proj",
    )(x, w_bf16)


def _rw_prep_kernel(chunk, rw_ref, prev0_ref, mu_ref, w0_ref, a0_ref, kk_ref, ka_ref, rk_ref,
                    wup_ref, aup_ref, gup_ref,
                    rt_ref, at_ref, bt_ref, kt_ref, bh_ref, kh_ref, v_ref, g_ref, bonus_ref, wc_ref,
                    carry_ref):
    tt = rw_ref.shape[1]

    @pl.when(pl.program_id(1) == 0)
    def _():
        carry_ref[...] = prev0_ref[0]

    p = rw_ref[0]
    prev = jnp.where(_iota(p.shape, 0) == 0, carry_ref[...], pltpu.roll(p, 1, 0))
    carry_ref[...] = p[tt - 1:tt]
    xs = p + mu_ref[...] * (prev - p)
    r = xs[:, 0:D_MIX]
    k = xs[:, D_MIX:2 * D_MIX]
    v = xs[:, 2 * D_MIX:3 * D_MIX]
    lo = 3 * D_MIX
    wd = xs[:, lo:lo + LORA_W]
    ad = xs[:, lo + LORA_W:lo + LORA_W + LORA_A]
    gd = xs[:, lo + LORA_W + LORA_A:D_RW_COLS]

    pre_w = w0_ref[...] + _dot(jnp.tanh(wd), wup_ref[...])
    w_log = -jax.nn.softplus(-pre_w) - 0.5
    logw = -jnp.exp(w_log)
    a = jax.nn.sigmoid(a0_ref[...] + _dot(ad, aup_ref[...]))
    g_ref[0] = _dot(jax.nn.sigmoid(gd), gup_ref[...])

    head_ones = _block_ones(D_MIX, HEAD_DIM)
    kk = k * kk_ref[...]
    ss = _dot_exact_rhs(kk * kk, head_ones, n=2)
    kk = kk * jnp.minimum(lax.rsqrt(ss), 1e12)
    k2 = k * (1.0 + (a - 1.0) * ka_ref[...])
    nb = -(kk * a)
    bonus_ref[0] = _dot_exact_rhs(r * k2 * rk_ref[...], head_ones, n=2) * v
    v_ref[0] = v.astype(BF16)

    ti, tj = _iota((tt, tt), 0), _iota((tt, tt), 1)
    same = ti // chunk == tj // chunk
    cum = _dot_exact_lhs((same & (tj <= ti)).astype(BF16), logw)
    tot = jnp.concatenate(
        [jnp.broadcast_to(cum[(c + 1) * chunk - 1:(c + 1) * chunk], (chunk, D_MIX)) for c in range(tt // chunk)],
        axis=0)
    e_in = jnp.exp(cum)
    e_out = jnp.exp(-cum)
    e_end = jnp.exp(tot - cum)
    rt_ref[0] = (r * e_in).astype(BF16)
    at_ref[0] = (kk * jnp.exp(cum - logw)).astype(BF16)
    bt_ref[0] = (nb * e_out).astype(BF16)
    kt_ref[0] = (k2 * e_out).astype(BF16)
    bh_ref[0] = (nb * e_end).astype(BF16)
    kh_ref[0] = (k2 * e_end).astype(BF16)
    etot = jnp.exp(tot)
    for c in range(tt // chunk):
        wc_ref[0, c] = etot[c * chunk:c * chunk + 1]


def rw_prep(rw, prev0, p, chunk, tt):
    b, t, _ = rw.shape
    seq = lambda width, dt: jax.ShapeDtypeStruct((b, t, width), dt)
    tile = lambda width: pl.BlockSpec((1, tt, width), lambda i, j: (i, j, 0))
    vec = lambda width: _resident((1, width))
    n_c = tt // chunk
    return pl.pallas_call(
        functools.partial(_rw_prep_kernel, chunk),
        out_shape=tuple(seq(D_MIX, BF16) for _ in range(7))
        + (seq(D_MIX, F32), seq(D_MIX, F32), jax.ShapeDtypeStruct((b, t // chunk, 1, D_MIX), F32)),
        grid=(b, t // tt),
        in_specs=[tile(D_RW_COLS), pl.BlockSpec((1, 1, D_RW_COLS), lambda i, j: (i, 0, 0)),
                  vec(D_RW_COLS), vec(D_MIX), vec(D_MIX), vec(D_MIX), vec(D_MIX), vec(D_MIX),
                  _resident((LORA_W, D_MIX)), _resident((LORA_A, D_MIX)), _resident((LORA_G, D_MIX))],
        out_specs=tuple(tile(D_MIX) for _ in range(9))
        + (pl.BlockSpec((1, n_c, 1, D_MIX), lambda i, j: (i, j, 0, 0)),),
        scratch_shapes=[pltpu.VMEM((1, D_RW_COLS), F32)],
        compiler_params=_cparams("parallel", "arbitrary"),
        name="rw_prep",
    )(rw, prev0, p["mu"], p["w0"], p["a0"], p["k_k"], p["k_a"], p["r_k"],
      p["w_up"], p["a_up"], p["g_up"])


def _unit_lower_inverse(a, chunk):
    shape = a.shape
    eye = (_iota(shape, 1) % chunk == _iota(shape, 0)).astype(F32)
    res = eye + a
    power = a.astype(BF16)
    for _ in range(chunk.bit_length() - 2):
        power = _dot(power, _bd_rows(power, RW_GROUP, chunk)).astype(BF16)
        yield
        res = res + _dot(res, _bd_rows(power, RW_GROUP, chunk))
    return res


def _interleave(chains):
    live = list(chains)
    while live:
        live = [c for c in live if next(c, StopIteration) is not StopIteration]


def _rw_chunk_kernel(chunk, rt_ref, at_ref, bt_ref, kt_ref, bh_ref, kh_ref, v_ref, g_ref, bonus_ref,
                     wc_ref, s0_ref, gnw_ref, gnb_ref, o_ref, s_out_ref, st_ref):
    c = pl.program_id(1)

    @pl.when(c == 0)
    def _():
        st_ref[...] = s0_ref[...]

    cat = (chunk, RW_GROUP * chunk)
    col_t = _iota(cat, 1) % chunk
    strict = col_t < _iota(cat, 0)
    incl = col_t <= _iota(cat, 0)
    head_ones = _block_ones(RW_LANES, HEAD_DIM)
    bd = lambda x: _bd_rows(x, RW_GROUP, HEAD_DIM)
    n_cat = RW_GROUP * chunk
    def chain(bi, grp):
        sl = slice(grp * RW_LANES, (grp + 1) * RW_LANES)
        rt, at, bt, kt = rt_ref[bi, :, sl], at_ref[bi, :, sl], bt_ref[bi, :, sl], kt_ref[bi, :, sl]
        bh, kh, v = bh_ref[bi, :, sl], kh_ref[bi, :, sl], v_ref[bi, :, sl]
        st = st_ref[bi, :, sl]

        gram = _dot(jnp.concatenate([at, rt], axis=0),
                    jnp.concatenate([bd(bt), bd(kt)], axis=0), _NT)
        yield
        zero = jnp.zeros(cat, F32)
        a_ab = jnp.where(strict, gram[:chunk, :n_cat], zero)
        a_ak = jnp.where(strict, gram[:chunk, n_cat:], zero)
        a_r = jnp.concatenate([jnp.where(incl, gram[chunk:, :n_cat], zero),
                               jnp.where(incl, gram[chunk:, n_cat:], zero)], axis=1).astype(BF16)
        v_bd = bd(v)
        x = _dot(a_ak, v_bd).astype(BF16)
        t_inv = yield from _unit_lower_inverse(a_ab, chunk)
        yield
        ua = _dot(t_inv, jnp.concatenate([bd(x), bd(at)], axis=1))
        st_bd = bd(st.astype(BF16))
        yield
        from_state = _dot(jnp.concatenate([ua[:, RW_LANES:].astype(BF16), rt], axis=0), st_bd)
        yield
        u = (ua[:, :RW_LANES] + from_state[:chunk]).astype(BF16)
        y = from_state[chunk:] + _dot(a_r, jnp.concatenate([bd(u), v_bd], axis=0))
        m = _dot(jnp.concatenate([bh, kh], axis=0), jnp.concatenate([u, v], axis=0), _TN)
        yield
        lane_head = _iota((HEAD_DIM, RW_LANES), 1) // HEAD_DIM
        fold = jnp.zeros((HEAD_DIM, RW_LANES), F32)
        for h in range(RW_GROUP):
            fold = fold + jnp.where(lane_head == h, m[h * HEAD_DIM:(h + 1) * HEAD_DIM], 0.0)
        diag = _iota((HEAD_DIM, RW_LANES), 1) % HEAD_DIM == _iota((HEAD_DIM, RW_LANES), 0)
        w_rows = jnp.where(diag, wc_ref[bi, 0, :, sl], 0.0)
        w_t = _dot_exact_rhs(w_rows, head_ones)
        st_ref[bi, :, sl] = st * w_t + fold

        mean = _dot_exact_rhs(y, head_ones, n=2) * (1.0 / HEAD_DIM)
        yield
        d = y - mean
        var = _dot_exact_rhs(d * d, head_ones, n=2) * (1.0 / HEAD_DIM)
        yield
        yn = d * lax.rsqrt(var + GN_EPS) * gnw_ref[:, sl] + gnb_ref[:, sl]
        o_ref[bi, :, sl] = ((yn + bonus_ref[bi, :, sl]) * g_ref[bi, :, sl]).astype(BF16)

    _interleave(chain(bi, grp) for bi in range(st_ref.shape[0]) for grp in range(N_HEADS // RW_GROUP))

    @pl.when(c == pl.num_programs(1) - 1)
    def _():
        s_out_ref[...] = st_ref[...]


def rw_chunks(prep, s0_t, gn_w, gn_b, chunk):
    rt, at, bt, kt, bh, kh, v, g, bonus, wc = prep
    b, t, _ = rt.shape
    nb = RW_SEQS if b % RW_SEQS == 0 else 1
    tile = pl.BlockSpec((nb, chunk, D_MIX), lambda i, j: (i, j, 0))
    state = pl.BlockSpec((nb, HEAD_DIM, D_MIX), lambda i, j: (i, 0, 0))
    return pl.pallas_call(
        functools.partial(_rw_chunk_kernel, chunk),
        out_shape=(jax.ShapeDtypeStruct((b, t, D_MIX), BF16),
                   jax.ShapeDtypeStruct((b, HEAD_DIM, D_MIX), F32)),
        grid=(b // nb, t // chunk),
        in_specs=[tile] * 9 + [pl.BlockSpec((nb, 1, 1, D_MIX), lambda i, j: (i, j, 0, 0)), state,
                               _resident((1, D_MIX)), _resident((1, D_MIX))],
        out_specs=(tile, state),
        scratch_shapes=[pltpu.VMEM((nb, HEAD_DIM, D_MIX), F32)],
        compiler_params=_cparams("parallel", "arbitrary"),
        name="rw_chunks",
    )(rt, at, bt, kt, bh, kh, v, g, bonus, wc, s0_t, gn_w, gn_b)


SB_PAIR = 2 * HEAD_DIM
SB_PAST_BLOCK = 256
SB_VISITS = 4


def _sb_kernel(tq, n_past, n_q, q_ref, kn_ref, vn_ref, *rest):
    if n_past:
        kp_ref, vp_ref, o_ref, acc_ref, c_ref = rest
    else:
        o_ref, acc_ref, c_ref = rest
    qi = pl.program_id(2)
    q2 = q_ref[0] * (HEAD_DIM ** -0.5)
    lane = _iota(q2.shape, 1)
    zero_q = jnp.zeros_like(q2)
    q_st = jnp.concatenate([jnp.where(lane < HEAD_DIM, q2, zero_q), jnp.where(lane >= HEAD_DIM, q2, zero_q)],
                           axis=0)
    acc_ref[...] = jnp.zeros_like(acc_ref)
    c_ref[...] = jnp.zeros_like(c_ref)

    def visit(k_blk, v_blk, masked):
        tk = k_blk.shape[0]
        later = (_iota((tk, tk), 0) > _iota((tk, tk), 1)).astype(BF16)
        z = _dot(q_st, k_blk, _NT)
        sp = jnp.maximum(z, 0.0) + jnp.log(1.0 + jnp.exp(-jnp.abs(z)))
        if masked:
            before = _iota((2 * tq, tk), 1) < _iota((2 * tq, tk), 0) % tq
            drop = jnp.where(before, sp, 0.0)
        else:
            drop = sp
        srev = _dot(drop, later)
        yield
        c = c_ref[...]
        att = jnp.exp((z - sp) - (c + srev))
        if masked:
            att = jnp.where(before, att, 0.0)
        acc_ref[...] += _dot(att, v_blk)
        c_ref[...] = c + (srev[:, 0:1] + drop[:, 0:1])

    def new_block(blk, masked=False):
        s = pl.multiple_of(blk * tq, tq)
        return visit(kn_ref[0, pl.ds(s, tq), :], vn_ref[0, pl.ds(s, tq), :], masked)

    lead = SB_VISITS - 1
    for n_with in range(min(lead, n_q - 1) + 1):
        @pl.when((qi == n_with) if n_with < lead else (qi >= lead))
        def _():
            _interleave([new_block(qi, masked=True)] + [new_block(qi - 1 - u) for u in range(n_with)])

    left = jnp.maximum(qi - lead, 0)

    def earlier_group(i, carry):
        _interleave(new_block(left - 1 - u - SB_VISITS * i) for u in range(SB_VISITS))
        return carry

    lax.fori_loop(0, left // SB_VISITS, earlier_group, 0)
    rest = left % SB_VISITS

    @pl.when(rest >= 2)
    def _():
        _interleave([new_block(rest - 1), new_block(rest - 2)])

    @pl.when(rest % 2 == 1)
    def _():
        _interleave([new_block(0)])

    if n_past:
        def past_block(blk):
            s = pl.multiple_of(blk * SB_PAST_BLOCK, SB_PAST_BLOCK)
            return visit(kp_ref[0, pl.ds(s, SB_PAST_BLOCK), :], vp_ref[0, pl.ds(s, SB_PAST_BLOCK), :], False)

        def past_group(i, carry):
            _interleave(past_block(n_past - 1 - u - SB_VISITS * i) for u in range(SB_VISITS))
            return carry

        lax.fori_loop(0, n_past // SB_VISITS, past_group, 0)
        _interleave(past_block(blk) for blk in reversed(range(n_past % SB_VISITS)))

    o_ref[0] = jnp.where(lane < HEAD_DIM, acc_ref[:tq], acc_ref[tq:]).astype(BF16)


def stick_breaking(q, k_new, v_new, k_past, v_past, tq):
    b, t, _ = q.shape
    n_pairs = D_MIX // SB_PAIR
    qspec = pl.BlockSpec((1, tq, SB_PAIR), lambda i, p, j: (i, j, p))
    seq = lambda length: pl.BlockSpec((1, length, SB_PAIR), lambda i, p, j: (i, 0, p))
    args, specs, n_past = [q, k_new, v_new], [qspec, seq(t), seq(t)], 0
    if k_past is not None:
        past_len = k_past.shape[1]
        n_past = past_len // SB_PAST_BLOCK
        args += [k_past, v_past]
        specs += [seq(past_len), seq(past_len)]
    return pl.pallas_call(
        functools.partial(_sb_kernel, tq, n_past, t // tq),
        out_shape=jax.ShapeDtypeStruct((b, t, D_MIX), BF16),
        grid=(b, n_pairs, t // tq),
        in_specs=specs,
        out_specs=qspec,
        scratch_shapes=[pltpu.VMEM((2 * tq, SB_PAIR), F32), pltpu.VMEM((2 * tq, 1), F32)],
        compiler_params=_cparams("parallel", "parallel", "arbitrary"),
        name="stick_breaking",
    )(*args)


def _merge_kernel(orw_ref, osb_ref, gate_ref, x_ref, wb0_ref, wb1_ref, wout_ref, gx_ref, wxq_ref,
                  x1_ref, qx_ref):
    g = jax.nn.sigmoid(gate_ref[...].astype(F32))
    mixed = (g[:, :D_MODEL] * jnp.dot(orw_ref[...], wb0_ref[...], preferred_element_type=F32)
             + g[:, D_MODEL:] * jnp.dot(osb_ref[...], wb1_ref[...], preferred_element_type=F32))
    x1 = x_ref[...] + _dot(mixed, wout_ref[...])
    x1_ref[...] = x1
    qx_ref[...] = _dot(_rms(x1, gx_ref[...]), wxq_ref[...]).astype(BF16)


def merge_out(o_rw, o_sb, gate, x, wb0, wb1, wout, gx, wxq):
    n = x.shape[0]
    tm = min(ROW_TILE, n)
    row = lambda width: pl.BlockSpec((tm, width), lambda i: (i, 0))
    return pl.pallas_call(
        _merge_kernel,
        out_shape=(jax.ShapeDtypeStruct((n, D_MODEL), F32), jax.ShapeDtypeStruct((n, D_MODEL), BF16)),
        grid=(n // tm,),
        in_specs=[row(D_MIX), row(D_MIX), row(GATE_COLS), row(D_MODEL),
                  _resident((D_MIX, D_MODEL)), _resident((D_MIX, D_MODEL)), _resident((D_MODEL, D_MODEL)),
                  _resident((1, D_MODEL)), _resident((D_MODEL, D_MODEL))],
        out_specs=(row(D_MODEL), row(D_MODEL)),
        compiler_params=_cparams("parallel"),
        name="merge_out",
    )(o_rw, o_sb, gate, x, wb0, wb1, wout, gx.reshape(1, D_MODEL), wxq)


def _mem_kv_kernel(m_ref, g_ref, wk_ref, wv_ref, k_ref, v_ref):
    h = _rms(m_ref[...], g_ref[...]).astype(BF16)
    k_ref[...] = jnp.dot(h, wk_ref[...], preferred_element_type=F32)
    v_ref[...] = jnp.dot(h, wv_ref[...], preferred_element_type=F32)


def mem_kv(mem, g, wk, wv):
    n = mem.shape[0]
    tm = min(ROW_TILE, n)
    row = pl.BlockSpec((tm, D_MODEL), lambda i: (i, 0))
    return pl.pallas_call(
        _mem_kv_kernel,
        out_shape=(jax.ShapeDtypeStruct((n, D_MODEL), F32),) * 2,
        grid=(n // tm,),
        in_specs=[row, _resident((1, D_MODEL)), _resident((D_MODEL, D_MODEL)), _resident((D_MODEL, D_MODEL))],
        out_specs=(row, row),
        compiler_params=_cparams("parallel"),
        name="mem_kv",
    )(mem, g.reshape(1, D_MODEL), wk, wv)


ROUTER_LANES = 128


def _xattn_kernel(q_ref, x1_ref, mk_ref, mv_ref, wxo_ref, gf_ref, wr_ref, br_ref,
                  x2_ref, hf_ref, idx_ref, gates_ref, counts_ref):
    q = q_ref[0]
    heads = [None] * N_X_HEADS

    def head(h):
        sl = slice(h * X_HEAD_DIM, (h + 1) * X_HEAD_DIM)
        s = _dot(q[:, sl], mk_ref[0, :, sl], _NT) * (X_HEAD_DIM ** -0.5)
        yield
        e = jnp.exp(s - jnp.max(s, axis=-1, keepdims=True))
        p = e * (1.0 / jnp.sum(e, axis=-1, keepdims=True))
        heads[h] = _dot(p, mv_ref[0, :, sl])

    _interleave(head(h) for h in range(N_X_HEADS))
    x2 = x1_ref[0] + _dot(jnp.concatenate(heads, axis=1), wxo_ref[...])
    x2_ref[0] = x2
    hf = _rms(x2, gf_ref[...])
    hf_ref[0] = hf.reshape(hf_ref.shape[1:])
    logits = _dot3(hf, wr_ref[...]) + br_ref[...]
    lane = _iota(logits.shape, 1)
    vals, idx_out = [], jnp.zeros(logits.shape, jnp.int32)
    chosen = jnp.zeros(logits.shape, F32)
    for j in range(TOP_K):
        m = jnp.max(logits, axis=-1, keepdims=True)
        pick = jnp.min(jnp.where(logits == m, lane, ROUTER_LANES), axis=-1, keepdims=True)
        vals.append(m)
        idx_out = jnp.where(lane == j, pick, idx_out)
        chosen = chosen + (lane == pick).astype(F32)
        logits = jnp.where(lane == pick, -jnp.inf, logits)

    @pl.when((pl.program_id(0) == 0) & (pl.program_id(1) == 0))
    def _():
        counts_ref[...] = jnp.zeros_like(counts_ref)

    counts_ref[...] += jnp.broadcast_to(jnp.sum(chosen, axis=0, keepdims=True), counts_ref.shape)
    exps = [jnp.exp(v - vals[0]) for v in vals]
    denom = exps[0] + exps[1] + exps[2] + exps[3]
    gates = jnp.zeros(logits.shape, F32)
    for j in range(TOP_K):
        gates = jnp.where(lane == j, exps[j] / denom, gates)
    idx_ref[0] = idx_out
    gates_ref[0] = gates


def xattn_router(qx, x1, mk, mv, wxo, g_ffn, w_router, b_router):
    b, t, _ = x1.shape
    tm = min(ROW_TILE, t)
    n_mem = mk.shape[1]
    wr = jnp.zeros((D_MODEL, ROUTER_LANES), F32).at[:, :N_EXPERTS].set(w_router)
    br = jnp.full((1, ROUTER_LANES), -jnp.inf, F32).at[0, :N_EXPERTS].set(b_router)
    tile = pl.BlockSpec((1, tm, D_MODEL), lambda i, j: (i, j, 0))
    mem = pl.BlockSpec((1, n_mem, D_MODEL), lambda i, j: (i, 0, 0))
    small = pl.BlockSpec((1, tm, ROUTER_LANES), lambda i, j: (i, j, 0))
    return pl.pallas_call(
        _xattn_kernel,
        out_shape=(jax.ShapeDtypeStruct((b, t, D_MODEL), F32), jax.ShapeDtypeStruct((b, t) + ROW_TILE_SHAPE, F32),
                   jax.ShapeDtypeStruct((b, t, ROUTER_LANES), jnp.int32),
                   jax.ShapeDtypeStruct((b, t, ROUTER_LANES), F32),
                   jax.ShapeDtypeStruct((ROUTE_SUBLANES, ROUTER_LANES), F32)),
        grid=(b, t // tm),
        in_specs=[tile, tile, mem, mem, _resident((D_MODEL, D_MODEL)), _resident((1, D_MODEL)),
                  _resident((D_MODEL, ROUTER_LANES)), _resident((1, ROUTER_LANES))],
        out_specs=(tile, pl.BlockSpec((1, tm) + ROW_TILE_SHAPE, lambda i, j: (i, j, 0, 0)), small, small,
                   _resident((ROUTE_SUBLANES, ROUTER_LANES))),
        compiler_params=_cparams("arbitrary", "arbitrary"),
        name="xattn_router",
    )(qx, x1, mk, mv, wxo, g_ffn.reshape(1, D_MODEL), wr, br)


ROW_GROUP = 8
DMA_QUEUES = 2
COMBINE_STRIPS = 8
ROUTE_SUBLANES = 8


def _route_kernel(idx_ref, total_ref, dest_ref, carry_ref, start_ref):
    tm = idx_ref.shape[0]
    idx = idx_ref[...]
    lane = _iota((tm, ROUTER_LANES), 1)
    picks = [idx[:, j:j + 1] for j in range(TOP_K)]
    onehot = jnp.zeros((tm, ROUTER_LANES), F32)
    for pick in picks:
        onehot = onehot + (lane == pick).astype(F32)

    @pl.when(pl.program_id(0) == 0)
    def _():
        carry_ref[...] = jnp.zeros_like(carry_ref)
        padded = jnp.floor((total_ref[...] + (MOE_BLOCK - 1.0)) * (1.0 / MOE_BLOCK)) * MOE_BLOCK
        earlier = (_iota((ROUTER_LANES, ROUTER_LANES), 0) < _iota((ROUTER_LANES, ROUTER_LANES), 1))
        start_ref[...] = _dot_exact_rhs(padded, earlier.astype(BF16))

    below = (_iota((tm, tm), 1) < _iota((tm, tm), 0)).astype(BF16)
    row = _dot(below, onehot) + (carry_ref[0:1, :] + start_ref[0:1, :])
    dest = jnp.zeros((tm, ROUTER_LANES), F32)
    for j, pick in enumerate(picks):
        mine = jnp.sum(jnp.where(lane == pick, row, 0.0), axis=1, keepdims=True)
        dest = jnp.where(lane == j, mine, dest)
    dest_ref[0] = dest.T[:ROUTE_SUBLANES].astype(jnp.int32)
    carry_ref[...] += jnp.broadcast_to(jnp.sum(onehot, axis=0, keepdims=True), carry_ref.shape)


def moe_route(idx, total, tm):
    n = idx.shape[0]
    n_tiles = n // tm
    return pl.pallas_call(
        _route_kernel,
        out_shape=jax.ShapeDtypeStruct((n_tiles, ROUTE_SUBLANES, tm), jnp.int32),
        grid=(n_tiles,),
        in_specs=[pl.BlockSpec((tm, ROUTER_LANES), lambda t: (t, 0)), _resident((ROUTE_SUBLANES, ROUTER_LANES))],
        out_specs=pl.BlockSpec((1, ROUTE_SUBLANES, tm), lambda t: (t, 0, 0)),
        scratch_shapes=[pltpu.VMEM((ROUTE_SUBLANES, ROUTER_LANES), F32),
                        pltpu.VMEM((ROUTE_SUBLANES, ROUTER_LANES), F32)],
        compiler_params=_cparams("arbitrary"),
        name="moe_route",
    )(idx, total)


def _block_tables(counts, n_blocks):
    cnt = counts[0, :N_EXPERTS].astype(jnp.int32)
    padded = (cnt + MOE_BLOCK - 1) // MOE_BLOCK * MOE_BLOCK
    pends = jnp.cumsum(padded).astype(jnp.int32)
    block_start = jnp.arange(n_blocks, dtype=jnp.int32) * MOE_BLOCK
    block_expert = jnp.minimum(jnp.sum((pends[None, :] <= block_start[:, None]).astype(jnp.int32), axis=1),
                               N_EXPERTS - 1)
    n_used = (pends[-1:] // MOE_BLOCK).astype(jnp.int32)
    return pends, block_expert, n_used


def _dispatch_kernel(bounds, pends_ref, dest_ref, *rest):
    hf_refs = rest[:len(bounds) - 1]
    xs_hbm, zbuf, sem = rest[len(bounds) - 1:]
    i = pl.program_id(0)

    @pl.when(i == 0)
    def _():
        zbuf[...] = jnp.zeros_like(zbuf)
        n_blocks = xs_hbm.shape[0] // MOE_BLOCK
        n_used = pends_ref[N_EXPERTS - 1] // MOE_BLOCK
        for wait in (False, True):
            def unused_block(blk, carry):
                cp = pltpu.make_async_copy(
                    zbuf, xs_hbm.at[pl.ds(pl.multiple_of(blk * MOE_BLOCK, MOE_BLOCK), MOE_BLOCK)], sem)
                cp.wait() if wait else cp.start()
                return carry

            lax.fori_loop(n_used, n_blocks, unused_block, 0)
            for e in range(N_EXPERTS):
                end = pends_ref[e]
                begin = pends_ref[e - 1] if e else 0

                @pl.when(end > begin)
                def _():
                    last = pl.multiple_of(end - MOE_BLOCK, MOE_BLOCK)
                    cp = pltpu.make_async_copy(zbuf, xs_hbm.at[pl.ds(last, MOE_BLOCK)], sem)
                    cp.wait() if wait else cp.start()

    for hf_ref, lo, hi in zip(hf_refs, bounds[:-1], bounds[1:]):
        @pl.when((i >= lo) & (i < hi))
        def _():
            tm = hf_ref.shape[0]
            for j in range(TOP_K):
                def issue(g, carry):
                    base = pl.multiple_of(g * ROW_GROUP, ROW_GROUP)
                    rows = hf_ref.at[pl.ds(base, ROW_GROUP)]
                    for u in range(ROW_GROUP):
                        pltpu.make_async_copy(rows.at[pl.ds(u, 1)], xs_hbm.at[pl.ds(dest_ref[j * tm + base + u], 1)],
                                              sem).start(priority=u % DMA_QUEUES)
                    return carry

                lax.fori_loop(0, tm // ROW_GROUP, issue, 0)
            for j in range(TOP_K):
                pltpu.make_async_copy(hf_ref, xs_hbm.at[pl.ds(0, tm)], sem).wait()


def moe_dispatch(hfs, bounds, dest, pends, n_rows):
    n_tiles, _, tm = dest.shape
    hf_spec = lambda lo, hi: pl.BlockSpec((tm,) + ROW_TILE_SHAPE,
                                          lambda i, pe: (jnp.clip(i - lo, 0, hi - lo - 1), 0, 0))
    grid_spec = pltpu.PrefetchScalarGridSpec(
        num_scalar_prefetch=1,
        grid=(n_tiles,),
        in_specs=[pl.BlockSpec((ROUTE_SUBLANES * tm,), lambda i, pe: (i,), memory_space=pltpu.SMEM)]
        + [hf_spec(lo, hi) for lo, hi in zip(bounds[:-1], bounds[1:])],
        out_specs=pl.BlockSpec(memory_space=pl.ANY),
        scratch_shapes=[pltpu.VMEM((MOE_BLOCK,) + ROW_TILE_SHAPE, F32), pltpu.SemaphoreType.DMA(())],
    )
    return pl.pallas_call(
        functools.partial(_dispatch_kernel, tuple(bounds)),
        out_shape=jax.ShapeDtypeStruct((n_rows,) + ROW_TILE_SHAPE, F32),
        grid_spec=grid_spec,
        compiler_params=_cparams_gather(),
        name="moe_dispatch",
    )(pends, dest.reshape(-1), *hfs)


W1_CHUNKS = 4
W2_CHUNKS = 2


def _moe_kernel(be_ref, nused_ref, x_ref, *rest):
    w1 = rest[:W1_CHUNKS]
    b1_ref = rest[W1_CHUNKS]
    w2 = rest[W1_CHUNKS + 1:W1_CHUNKS + 1 + W2_CHUNKS]
    b2_ref, o_ref, w1_bf, w2_bf = rest[W1_CHUNKS + 1 + W2_CHUNKS:]
    i = pl.program_id(0)
    half = W1_CHUNKS // 2
    wide = 2 * D_FF // W1_CHUNKS

    @pl.when((i < nused_ref[0]) & ((i == 0) | (be_ref[i] != be_ref[jnp.maximum(i - 1, 0)])))
    def _():
        for c in range(W1_CHUNKS):
            w1_bf[c] = w1[c][0].astype(BF16)
        for c in range(W2_CHUNKS):
            w2_bf[c] = w2[c][0].astype(BF16)

    @pl.when(i < nused_ref[0])
    def _():
        x = x_ref[...].reshape(MOE_BLOCK, D_MODEL).astype(BF16)
        parts = [None] * half

        def ff_slice(c):
            glu = jnp.dot(x, w1_bf[c], preferred_element_type=F32) + b1_ref[0, :, c * wide:(c + 1) * wide]
            lin = (jnp.dot(x, w1_bf[half + c], preferred_element_type=F32)
                   + b1_ref[0, :, D_FF + c * wide:D_FF + (c + 1) * wide])
            yield
            glu = jnp.minimum(glu, SWIGLU_LIMIT)
            lin = jnp.clip(lin, -SWIGLU_LIMIT, SWIGLU_LIMIT)
            act = glu * jax.nn.sigmoid(SWIGLU_ALPHA * glu) * (lin + 1.0)
            parts[c] = _dot(act, w2_bf[c])

        _interleave(ff_slice(c) for c in range(half))
        o_ref[...] = (b2_ref[0] + sum(parts[1:], parts[0])).reshape(o_ref.shape)

    @pl.when(i >= nused_ref[0])
    def _():
        o_ref[...] = jnp.zeros_like(o_ref)


def moe_experts(xs, block_expert, n_used, w1, b1, w2, b2):
    n_blocks = block_expert.shape[0]
    assert W1_CHUNKS // 2 == W2_CHUNKS
    c1 = 2 * D_FF // W1_CHUNKS
    c2 = D_FF // W2_CHUNKS
    w1_spec = lambda c: pl.BlockSpec((1, D_MODEL, c1), lambda i, be, nu: (be[i], 0, c))
    w2_spec = lambda c: pl.BlockSpec((1, c2, D_MODEL), lambda i, be, nu: (be[i], c, 0))
    grid_spec = pltpu.PrefetchScalarGridSpec(
        num_scalar_prefetch=2,
        grid=(n_blocks,),
        in_specs=[pl.BlockSpec((MOE_BLOCK,) + ROW_TILE_SHAPE, lambda i, be, nu: (jnp.minimum(i, nu[0] - 1), 0, 0))]
        + [w1_spec(c) for c in range(W1_CHUNKS)]
        + [pl.BlockSpec((1, 1, 2 * D_FF), lambda i, be, nu: (be[i], 0, 0))]
        + [w2_spec(c) for c in range(W2_CHUNKS)]
        + [pl.BlockSpec((1, 1, D_MODEL), lambda i, be, nu: (be[i], 0, 0))],
        out_specs=pl.BlockSpec((MOE_BLOCK,) + ROW_TILE_SHAPE, lambda i, be, nu: (i, 0, 0)),
        scratch_shapes=[pltpu.VMEM((W1_CHUNKS, D_MODEL, c1), BF16), pltpu.VMEM((W2_CHUNKS, c2, D_MODEL), BF16)],
    )
    return pl.pallas_call(
        _moe_kernel,
        out_shape=jax.ShapeDtypeStruct((n_blocks * MOE_BLOCK,) + ROW_TILE_SHAPE, F32),
        grid_spec=grid_spec,
        compiler_params=_cparams("arbitrary"),
        name="moe_experts",
    )(block_expert, n_used, xs, *([w1] * W1_CHUNKS), b1.reshape(N_EXPERTS, 1, 2 * D_FF),
      *([w2] * W2_CHUNKS), b2.reshape(N_EXPERTS, 1, D_MODEL))


def _start_combine_gather(dest_ref, os_hbm, dst, sem):
    tm = dst.shape[0] // TOP_K
    for j in range(TOP_K):
        def issue(g, carry):
            base = pl.multiple_of(g * ROW_GROUP, ROW_GROUP)
            rows = dst.at[pl.ds(j * tm + base, ROW_GROUP)]
            for u in range(ROW_GROUP):
                pltpu.make_async_copy(os_hbm.at[pl.ds(dest_ref[j * tm + base + u], 1)], rows.at[pl.ds(u, 1)],
                                      sem).start(priority=u % DMA_QUEUES)
            return carry

        lax.fori_loop(0, tm // ROW_GROUP, issue, 0)


def _combine_kernel(n_tiles, dest_ref, dest_next_ref, os_hbm, x2_ref, gates_ref, gfin_ref, y_ref, buf, sem):
    tm = x2_ref.shape[0]
    i = pl.program_id(0)
    slot = i % 2

    @pl.when(i == 0)
    def _():
        _start_combine_gather(dest_ref, os_hbm, buf.at[0], sem.at[0])

    pltpu.make_async_copy(os_hbm.at[pl.ds(0, TOP_K * tm)], buf.at[slot], sem.at[slot]).wait()
    gates = gates_ref[...]
    nxt = buf.at[1 - slot]
    strip = tm // COMBINE_STRIPS
    for s in range(COMBINE_STRIPS):
        lo = s * strip
        y = x2_ref[lo:lo + strip]
        for j in range(TOP_K):
            y = y + gates[lo:lo + strip, j:j + 1] * buf[slot, j * tm + lo:j * tm + lo + strip].reshape(strip, D_MODEL)
        y_ref[lo:lo + strip] = _rms(y, gfin_ref[...])
        for j in range(TOP_K):
            for r in range(lo, lo + strip):
                pltpu.make_async_copy(os_hbm.at[pl.ds(dest_next_ref[j * tm + r], 1)], nxt.at[pl.ds(j * tm + r, 1)],
                                      sem.at[1 - slot]).start(priority=r % DMA_QUEUES)

    @pl.when(i == n_tiles - 1)
    def _():
        pltpu.make_async_copy(os_hbm.at[pl.ds(0, TOP_K * tm)], nxt, sem.at[1 - slot]).wait()


def moe_combine(os_rows, dest, x2, gates, g_final):
    n = x2.shape[0]
    n_tiles, _, tm = dest.shape
    dest_spec = lambda step: pl.BlockSpec(
        (ROUTE_SUBLANES * tm,), lambda i: (jnp.minimum(i + step, n_tiles - 1),), memory_space=pltpu.SMEM)
    dest = dest.reshape(-1)
    return pl.pallas_call(
        functools.partial(_combine_kernel, n_tiles),
        out_shape=jax.ShapeDtypeStruct((n, D_MODEL), F32),
        grid=(n_tiles,),
        in_specs=[dest_spec(0), dest_spec(1),
                  pl.BlockSpec(memory_space=pl.ANY),
                  pl.BlockSpec((tm, D_MODEL), lambda i: (i, 0)),
                  pl.BlockSpec((tm, ROUTER_LANES), lambda i: (i, 0)),
                  _resident((1, D_MODEL))],
        out_specs=pl.BlockSpec((tm, D_MODEL), lambda i: (i, 0)),
        scratch_shapes=[pltpu.VMEM((2, TOP_K * tm) + ROW_TILE_SHAPE, F32), pltpu.SemaphoreType.DMA((2,))],
        compiler_params=_cparams_gather(),
        name="moe_combine",
    )(dest, dest, os_rows, x2, gates, g_final.reshape(1, D_MODEL))


def moe_ffn(groups, w):
    sizes = [g[0].shape[0] for g in groups]
    n = sum(sizes)
    tm = next(c for c in (256, 128, 64, 32, 16, 8) if all(s % c == 0 for s in sizes))
    counts = sum(g[4] for g in groups[1:]) + groups[0][4]
    dest = moe_route(jnp.concatenate([g[2] for g in groups], axis=0), counts, tm)
    n_blocks = -(-(n * TOP_K + N_EXPERTS * (MOE_BLOCK - 1)) // MOE_BLOCK)
    pends, block_expert, n_used = _block_tables(counts, n_blocks)
    bounds = [0]
    for s in sizes:
        bounds.append(bounds[-1] + s // tm)
    xs = moe_dispatch([g[1] for g in groups], bounds, dest, pends, n_blocks * MOE_BLOCK)
    os_rows = moe_experts(xs, block_expert, n_used, w["w_e1"], w["b_e1"], w["w_e2"], w["b_e2"])
    return [moe_combine(os_rows, dest[lo:hi], g[0], g[3], w["g_final"])
            for g, lo, hi in zip(groups, bounds[:-1], bounds[1:])]


def _mixers(x, prev0, s0_t, k_past, v_past, mk, mv, w, chunk, prep_tile, tq):
    b, t, _ = x.shape
    n = b * t
    xf = x.reshape(n, D_MODEL)
    q, k_sb, v_sb, rw, gate, k_heads, v_heads = in_proj(xf, w["g_mix"], w["w_in"])
    shift = norm_rows(x[:, -1], w["g_mix"])
    prep = rw_prep(rw.reshape(b, t, D_RW_COLS), prev0, w["rw"], chunk, prep_tile)
    o_rw, s_t = rw_chunks(prep, s0_t, w["gn_w"], w["gn_b"], chunk)
    seq = lambda a: a.reshape(b, t, D_MIX)
    o_sb = stick_breaking(seq(q), seq(k_sb), seq(v_sb), k_past, v_past, tq)
    x1, qx = merge_out(o_rw.reshape(n, D_MIX), o_sb.reshape(n, D_MIX), gate, xf,
                       w["wb0"], w["wb1"], w["w_out"], w["g_xattn"], w["w_xq"])
    x2, hf, idx, gates, counts = xattn_router(qx.reshape(b, t, D_MODEL), x1.reshape(b, t, D_MODEL), mk, mv,
                                              w["w_xo"], w["g_ffn"], w["w_router"], w["b_router"])
    tok = (x2.reshape(n, D_MODEL), hf.reshape((n,) + ROW_TILE_SHAPE), idx.reshape(n, ROUTER_LANES),
           gates.reshape(n, ROUTER_LANES), counts)
    return tok, shift, s_t, k_heads, v_heads


def _state_to_t(s):
    b = s.shape[0]
    return s.transpose(0, 3, 1, 2).reshape(b, HEAD_DIM, D_MIX)


def _state_from_t(s_t):
    b = s_t.shape[0]
    return s_t.reshape(b, HEAD_DIM, N_HEADS, HEAD_DIM).transpose(0, 2, 3, 1)


def kernel(x_prompt, x_sample, state_rw_shift, state_rw_wkv, cache_sb_k, cache_sb_v, cache_mem_k, cache_mem_v, mem_prompt, g_mix, w_in, rw_mu, rw_w0, rw_w_up, rw_a0, rw_a_up, rw_g_up, rw_k_k, rw_k_a, rw_r_k, rw_gn_w, rw_gn_b, w_branch, w_out, g_xattn, g_mem, w_xq, w_mk, w_mv, w_xo, g_ffn, w_router, b_router, w_e1, b_e1, w_e2, b_e2, g_final):
    assert g_mix.shape[0] == 1, "single-layer trunk"
    row = lambda a: a.reshape(1, -1)
    w = dict(
        g_mix=g_mix[0], w_in=w_in[0].astype(BF16),
        rw=dict(mu=row(rw_mu[0]), w0=row(rw_w0[0]), a0=row(rw_a0[0]), k_k=row(rw_k_k[0]), k_a=row(rw_k_a[0]),
                r_k=row(rw_r_k[0]), w_up=rw_w_up[0].astype(BF16), a_up=rw_a_up[0].astype(BF16),
                g_up=rw_g_up[0].astype(BF16)),
        gn_w=row(rw_gn_w[0]), gn_b=row(rw_gn_b[0]),
        wb0=w_branch[0, 0].astype(BF16), wb1=w_branch[0, 1].astype(BF16), w_out=w_out[0].astype(BF16),
        g_xattn=g_xattn[0], w_xq=w_xq[0].astype(BF16), w_xo=w_xo[0].astype(BF16),
        g_ffn=g_ffn[0], w_router=w_router[0], b_router=b_router[0],
        w_e1=w_e1[0], b_e1=b_e1[0], w_e2=w_e2[0], b_e2=b_e2[0],
        g_final=g_final,
    )
    bp, t, _ = x_prompt.shape
    bs, ts, _ = x_sample.shape
    n_mem = mem_prompt.shape[1]

    mk_p, mv_p = mem_kv(mem_prompt.reshape(bp * n_mem, D_MODEL), g_mem[0],
                        w_mk[0].astype(BF16), w_mv[0].astype(BF16))
    mk_p = mk_p.reshape(bp, n_mem, D_MODEL)
    mv_p = mv_p.reshape(bp, n_mem, D_MODEL)
    tok_p, sh_p, st_p, k_p, v_p = _mixers(
        x_prompt, jnp.zeros((bp, 1, D_RW_COLS), F32), jnp.zeros((bp, HEAD_DIM, D_MIX), F32),
        None, None, mk_p, mv_p, w, chunk=64, prep_tile=256, tq=256)

    prev_s = plain_proj(state_rw_shift[0], w["w_in"][:, SB_COLS:SB_COLS + D_RW_COLS])
    past = cache_sb_k.shape[2]
    tok_s, sh_s, st_s, k_s, v_s = _mixers(
        x_sample, prev_s.reshape(bs, 1, D_RW_COLS), _state_to_t(state_rw_wkv[0]),
        cache_sb_k[0].reshape(bs, past, D_MIX), cache_sb_v[0].reshape(bs, past, D_MIX),
        cache_mem_k[0].reshape(bs, n_mem, D_MODEL), cache_mem_v[0].reshape(bs, n_mem, D_MODEL),
        w, chunk=ts, prep_tile=ts, tq=ts)

    y_p, y_s = moe_ffn([tok_p, tok_s], w)

    heads = lambda a, b_, t_: a.reshape(1, b_, t_, N_HEADS, HEAD_DIM)
    xh = lambda a: a.reshape(1, bp, n_mem, N_X_HEADS, X_HEAD_DIM)
    return (y_p.reshape(bp, t, D_MODEL), y_s.reshape(bs, ts, D_MODEL),
            sh_p[None], _state_from_t(st_p)[None], heads(k_p, bp, t), heads(v_p, bp, t), xh(mk_p), xh(mv_p),
            sh_s[None], _state_from_t(st_s)[None], heads(k_s, bs, ts), heads(v_s, bs, ts))
```
